```python
import math
import jax, jax.numpy as jnp
from jax import lax
import numpy as np

D_MODEL = 1024
BATCH = 1
SEQ = 16384
DEPTH = 2

MLA_HEADS = 8
MLA_Q_LORA = 256
MLA_KV_LORA = 128
MLA_NOPE = 64
MLA_ROPE = 32
MLA_V = 64
Q_BLOCK = 128
ROPE_THETA = 10000.0
SWA_HEADS = 8
SWA_KV_HEADS = 2
SWA_HD = 64
WINDOW = 128
BLOCK = 128
REL_BUCKETS = 32
REL_MAX_DIST = 128
MEM_LEN = 256
MEM_HEADS = 4
MEM_HD = 128
N_BRANCH = 3
D_FF = 4 * D_MODEL
EPS = 1e-6

IN_SIZES = (
    MLA_Q_LORA,
    MLA_KV_LORA + MLA_ROPE,
    SWA_HEADS * SWA_HD,
    SWA_KV_HEADS * SWA_HD,
    SWA_KV_HEADS * SWA_HD,
    MEM_HEADS * MEM_HD,
    N_BRANCH * D_MODEL,
)
IN_COLS = sum(IN_SIZES)

kernel_name = "hybrid_mla_swa_mem_gated_block"


def rms_norm(x, g):
    x32 = x.astype(jnp.float32)
    y = x32 * lax.rsqrt(jnp.mean(x32 * x32, axis=-1, keepdims=True) + EPS)
    return y.astype(x.dtype) * g


def split_points():
    pts, acc = [], 0
    for s in IN_SIZES[:-1]:
        acc += s
        pts.append(acc)
    return pts


def rope_tables(seq):
    pos = jnp.arange(seq, dtype=jnp.float32)
    inv = 1.0 / (ROPE_THETA ** (jnp.arange(0, MLA_ROPE, 2, dtype=jnp.float32) / MLA_ROPE))
    ang = pos[:, None] * inv[None, :]
    return jnp.cos(ang), jnp.sin(ang)


def apply_rope(t, cos, sin):
    cos = cos.astype(t.dtype)
    sin = sin.astype(t.dtype)
    t1, t2 = jnp.split(t, 2, axis=-1)
    return jnp.concatenate([t1 * cos - t2 * sin, t2 * cos + t1 * sin], axis=-1)


def t5_bucket(dist):
    n = jnp.maximum(dist, 0)
    max_exact = REL_BUCKETS // 2
    nf = jnp.maximum(n, 1).astype(jnp.float32)
    large = max_exact + (jnp.log(nf / max_exact) / math.log(REL_MAX_DIST / max_exact)
                         * (REL_BUCKETS - max_exact)).astype(jnp.int32)
    large = jnp.minimum(large, REL_BUCKETS - 1)
    return jnp.where(n < max_exact, n, large)


def mla_attention(q_nope, q_rope, k_nope, k_rope, v):
    B, S = q_nope.shape[0], q_nope.shape[1]
    nb = S // Q_BLOCK
    scale = (MLA_NOPE + MLA_ROPE) ** -0.5
    k_pos = jnp.arange(S)

    def to_blocks(t):
        return jnp.moveaxis(t.reshape(B, nb, Q_BLOCK, *t.shape[2:]), 1, 0)

    def one_block(args):
        qn, qr, i = args
        s = (jnp.einsum('bqhd,bkhd->bhqk', qn, k_nope)
             + jnp.einsum('bqhd,bkd->bhqk', qr, k_rope)).astype(jnp.float32) * scale
        q_pos = i * Q_BLOCK + jnp.arange(Q_BLOCK)
        s = jnp.where(k_pos[None, :] <= q_pos[:, None], s, -jnp.inf)
        p = jax.nn.softmax(s, axis=-1).astype(v.dtype)
        return jnp.einsum('bhqk,bkhd->bqhd', p, v)

    out = lax.map(one_block, (to_blocks(q_nope), to_blocks(q_rope), jnp.arange(nb)))
    return jnp.moveaxis(out, 0, 1).reshape(B, S, MLA_HEADS * MLA_V)


def swa_attention(q, k, v, rel_bias, sinks):
    B, S = q.shape[0], q.shape[1]
    nb = S // BLOCK
    G, R = SWA_KV_HEADS, SWA_HEADS // SWA_KV_HEADS
    qb = q.reshape(B, nb, BLOCK, G, R, SWA_HD)
    kb = k.reshape(B, nb, BLOCK, G, SWA_HD)
    vb = v.reshape(B, nb, BLOCK, G, SWA_HD)

    def with_prev(t):
        prev = jnp.pad(t, ((0, 0), (1, 0), (0, 0), (0, 0), (0, 0)))[:, :-1]
        return jnp.concatenate([prev, t], axis=2)

    kk, vv = with_prev(kb), with_prev(vb)
    s = jnp.einsum('bnqgrd,bnkgd->bngrqk', qb, kk).astype(jnp.float32) * (SWA_HD ** -0.5)

    qi = jnp.arange(BLOCK)[:, None]
    kj = jnp.arange(2 * BLOCK)[None, :]
    dist = qi + BLOCK - kj
    bias = rel_bias.astype(jnp.float32)[t5_bucket(dist)]
    bias = jnp.transpose(bias, (2, 0, 1)).reshape(G, R, BLOCK, 2 * BLOCK)
    band = (dist >= 0) & (dist < WINDOW)
    has_prev = (jnp.arange(nb) > 0)[:, None, None] | (kj >= BLOCK)[None]
    valid = band[None] & has_prev
    s = jnp.where(valid[None, :, None, None], s + bias, -jnp.inf)

    sink = jnp.broadcast_to(sinks.astype(jnp.float32).reshape(1, 1, G, R, 1, 1),
                            s.shape[:-1] + (1,))
    p = jax.nn.softmax(jnp.concatenate([s, sink], axis=-1), axis=-1)[..., :-1].astype(v.dtype)
    o = jnp.einsum('bngrqk,bnkgd->bnqgrd', p, vv)
    return o.reshape(B, S, SWA_HEADS * SWA_HD)


def mem_attention(q, km, vm):
    B, S = q.shape[0], q.shape[1]
    s = jnp.einsum('bshd,bmhd->bhsm', q, km).astype(jnp.float32) * (MEM_HD ** -0.5)
    p = jax.nn.softmax(s, axis=-1).astype(vm.dtype)
    return jnp.einsum('bhsm,bmhd->bshd', p, vm).reshape(B, S, MEM_HEADS * MEM_HD)


def setup_inputs(seed: int = 0) -> dict:
    key = jax.random.key(seed)
    ks = jax.random.split(key, 24)

    def nrm(k, shape, scale):
        return jax.random.normal(k, shape, jnp.float32) * scale

    def gain(k, shape):
        return 1.0 + 0.02 * jax.random.normal(k, shape, jnp.float32)

    L, D = DEPTH, D_MODEL
    return {
        "x": nrm(ks[0], (BATCH, SEQ, D), 1.0),
        "mem": nrm(ks[1], (BATCH, MEM_LEN, D), 1.0),
        "rel_bias": nrm(ks[2], (REL_BUCKETS, SWA_HEADS), 0.5),
        "attn_norm": gain(ks[3], (L, D)),
        "mem_norm": gain(ks[4], (L, D)),
        "w_in": nrm(ks[5], (L, D, IN_COLS), D ** -0.5),
        "b_gate": nrm(ks[6], (L, N_BRANCH * D), 0.02),
        "mla_q_norm": gain(ks[7], (L, MLA_Q_LORA)),
        "w_uq": nrm(ks[8], (L, MLA_Q_LORA, MLA_HEADS * (MLA_NOPE + MLA_ROPE)), MLA_Q_LORA ** -0.5),
        "mla_kv_norm": gain(ks[9], (L, MLA_KV_LORA)),
        "w_ukv": nrm(ks[10], (L, MLA_KV_LORA, MLA_HEADS * (MLA_NOPE + MLA_V)), MLA_KV_LORA ** -0.5),
        "attn_sinks": nrm(ks[11], (L, SWA_HEADS), 0.5),
        "w_mem_kv": nrm(ks[12], (L, D, 2 * MEM_HEADS * MEM_HD), D ** -0.5),
        "w_o_mla": nrm(ks[13], (L, MLA_HEADS * MLA_V, D), (MLA_HEADS * MLA_V) ** -0.5),
        "w_o_swa": nrm(ks[14], (L, SWA_HEADS * SWA_HD, D), (SWA_HEADS * SWA_HD) ** -0.5),
        "w_o_mem": nrm(ks[15], (L, MEM_HEADS * MEM_HD, D), (MEM_HEADS * MEM_HD) ** -0.5),
        "w_out": nrm(ks[16], (L, D, D), D ** -0.5),
        "mlp_norm": gain(ks[17], (L, D)),
        "w_up": nrm(ks[18], (L, D, D_FF), D ** -0.5),
        "w_down": nrm(ks[19], (L, D_FF, D), D_FF ** -0.5),
        "final_norm": gain(ks[20], (D,)),
    }


def reference(x, mem, rel_bias, attn_norm, mem_norm, w_in, b_gate, mla_q_norm, w_uq,
              mla_kv_norm, w_ukv, attn_sinks, w_mem_kv, w_o_mla, w_o_swa, w_o_mem,
              w_out, mlp_norm, w_up, w_down, final_norm):
    B, S, D = x.shape
    cos, sin = rope_tables(S)
    pts = split_points()

    for l in range(DEPTH):
        h = rms_norm(x, attn_norm[l])
        proj = h @ w_in[l]
        c_q, kv_a, q_s, k_s, v_s, q_m, gates = jnp.split(proj, pts, axis=-1)

        c_q = rms_norm(c_q, mla_q_norm[l])
        q = (c_q @ w_uq[l]).reshape(B, S, MLA_HEADS, MLA_NOPE + MLA_ROPE)
        q_nope = q[..., :MLA_NOPE]
        q_pe = apply_rope(q[..., MLA_NOPE:], cos[:, None, :], sin[:, None, :])
        c_kv = rms_norm(kv_a[..., :MLA_KV_LORA], mla_kv_norm[l])
        k_pe = apply_rope(kv_a[..., MLA_KV_LORA:], cos, sin)
        kv = (c_kv @ w_ukv[l]).reshape(B, S, MLA_HEADS, MLA_NOPE + MLA_V)
        o_mla = mla_attention(q_nope, q_pe, kv[..., :MLA_NOPE], k_pe, kv[..., MLA_NOPE:])

        o_swa = swa_attention(q_s.reshape(B, S, SWA_HEADS, SWA_HD),
                              k_s.reshape(B, S, SWA_KV_HEADS, SWA_HD),
                              v_s.reshape(B, S, SWA_KV_HEADS, SWA_HD),
                              rel_bias, attn_sinks[l])

        mn = rms_norm(mem, mem_norm[l])
        kvm = (mn @ w_mem_kv[l]).reshape(mem.shape[0], MEM_LEN, 2, MEM_HEADS, MEM_HD)
        o_mem = mem_attention(q_m.reshape(B, S, MEM_HEADS, MEM_HD), kvm[:, :, 0], kvm[:, :, 1])

        g = jax.nn.sigmoid(gates + b_gate[l]).reshape(B, S, N_BRANCH, D)
        y = (g[..., 0, :] * (o_mla @ w_o_mla[l])
             + g[..., 1, :] * (o_swa @ w_o_swa[l])
             + g[..., 2, :] * (o_mem @ w_o_mem[l]))
        x = x + y @ w_out[l]

        h = rms_norm(x, mlp_norm[l])
        x = x + jnp.square(jax.nn.relu(h @ w_up[l])) @ w_down[l]

    return rms_norm(x, final_norm)
```

```python
import functools
import math

import jax
import jax.numpy as jnp
from jax import lax
from jax.experimental import pallas as pl
from jax.experimental.pallas import tpu as pltpu

F32 = jnp.float32
BF16 = jnp.bfloat16

D_MODEL = 1024
MLA_HEADS = 8
MLA_Q_LORA = 256
MLA_KV_LORA = 128
MLA_NOPE = 64
MLA_ROPE = 32
MLA_V = 64
ROPE_THETA = 10000.0
SWA_HEADS = 8
SWA_KV_HEADS = 2
SWA_HD = 64
WINDOW = 128
BLOCK = 128
REL_BUCKETS = 32
REL_MAX_DIST = 128
MEM_LEN = 256
MEM_HEADS = 4
MEM_HD = 128
N_BRANCH = 3
D_FF = 4 * D_MODEL
EPS = 1e-6

HEAD_PAD = 128
NEG_BIG = -1e30

PROJ_TM = 512
MLA_TQ = 512
MLA_TK = 512
SWA_TQ = 512
MEM_TQ = 512
POST_TM = 512
FF_CHUNK = 1024
VMEM_LIMIT = 56 * 1024 * 1024

_C_CQ = 0
_C_CKV = _C_CQ + MLA_Q_LORA
_C_KPE = _C_CKV + MLA_KV_LORA
_C_KROT = _C_KPE + HEAD_PAD
_C_QS = _C_KROT + HEAD_PAD
_C_KS = _C_QS + SWA_HEADS * SWA_HD
_C_VS = _C_KS + SWA_KV_HEADS * SWA_HD
_C_QM = _C_VS + SWA_KV_HEADS * SWA_HD
_C_GATE = _C_QM + MEM_HEADS * MEM_HD
_C_END = _C_GATE + N_BRANCH * D_MODEL


def _rms(x, g):
    return x * lax.rsqrt(jnp.mean(x * x, axis=-1, keepdims=True) + EPS) * g


def _const_spec(shape):
    zeros = (0,) * len(shape)
    return pl.BlockSpec(shape, lambda *_: zeros, pipeline_mode=pl.Buffered(1))


def _dot(a, b):
    return jnp.dot(a, b, preferred_element_type=F32)


def _dot_nt(a, b):
    return lax.dot_general(a, b, (((1,), (1,)), ((), ())), preferred_element_type=F32)


def _proj_kernel(x_ref, g_ref, w1_ref, qn_ref, wuq_ref, kvn_ref, wkv_ref, bg_ref, tab_ref,
                 q_ref, k_ref, v_ref, qs_ref, ks_ref, vs_ref, qm_ref, gate_ref):
    h = _rms(x_ref[...], g_ref[...]).astype(BF16)

    def mm(lo, hi):
        return _dot(h, w1_ref[:, lo:hi])

    tab = tab_ref[...]
    q_cos, q_sin = tab[:, 0:HEAD_PAD], tab[:, HEAD_PAD:2 * HEAD_PAD]
    k_cos, k_sin = tab[:, 2 * HEAD_PAD:3 * HEAD_PAD], tab[:, 3 * HEAD_PAD:4 * HEAD_PAD]
    hw = MLA_HEADS * HEAD_PAD

    cq = _rms(mm(_C_CQ, _C_CKV), qn_ref[...]).astype(BF16)
    qq = _dot(cq, wuq_ref[...])
    for hh in range(MLA_HEADS):
        lo = hh * HEAD_PAD
        q_ref[hh] = (qq[:, lo:lo + HEAD_PAD] * q_cos
                     + qq[:, hw + lo:hw + lo + HEAD_PAD] * q_sin).astype(BF16)

    ckv = _rms(mm(_C_CKV, _C_KPE), kvn_ref[...]).astype(BF16)
    kv = _dot(ckv, wkv_ref[...])
    kpe = mm(_C_KPE, _C_KROT) * k_cos + mm(_C_KROT, _C_QS) * k_sin
    for hh in range(MLA_HEADS):
        lo = hh * HEAD_PAD
        k_ref[hh] = (kv[:, lo:lo + HEAD_PAD] + kpe).astype(BF16)
        v_ref[hh] = kv[:, hw + lo:hw + lo + HEAD_PAD].astype(BF16)

    qs_ref[...] = (mm(_C_QS, _C_KS) * (SWA_HD ** -0.5)).astype(BF16)
    ks_ref[...] = mm(_C_KS, _C_VS).astype(BF16)
    vs_ref[...] = mm(_C_VS, _C_QM).astype(BF16)
    qm_ref[...] = (mm(_C_QM, _C_GATE) * (MEM_HD ** -0.5)).astype(BF16)
    for c in range(N_BRANCH):
        lo = _C_GATE + c * D_MODEL
        z = mm(lo, lo + D_MODEL) + bg_ref[:, c * D_MODEL:(c + 1) * D_MODEL]
        gate_ref[:, c * D_MODEL:(c + 1) * D_MODEL] = (1.0 / (1.0 + jnp.exp(-z))).astype(BF16)


def _proj_call(x, g, w1, qn, wuq, kvn, wkv, bg, tab):
    s = x.shape[0]
    tm = PROJ_TM
    hw = MLA_HEADS * HEAD_PAD
    head_spec = pl.BlockSpec((MLA_HEADS, tm, HEAD_PAD), lambda i: (0, i, 0))
    head_shape = jax.ShapeDtypeStruct((MLA_HEADS, s, HEAD_PAD), BF16)

    def row_spec(n):
        return pl.BlockSpec((tm, n), lambda i: (i, 0))

    def row_shape(n):
        return jax.ShapeDtypeStruct((s, n), BF16)

    n_qs, n_kv, n_qm, n_g = SWA_HEADS * SWA_HD, SWA_KV_HEADS * SWA_HD, MEM_HEADS * MEM_HD, N_BRANCH * D_MODEL
    return pl.pallas_call(
        _proj_kernel,
        grid=(s // tm,),
        in_specs=[
            row_spec(D_MODEL),
            _const_spec((1, D_MODEL)),
            _const_spec((D_MODEL, _C_END)),
            _const_spec((1, MLA_Q_LORA)),
            _const_spec((MLA_Q_LORA, 2 * hw)),
            _const_spec((1, MLA_KV_LORA)),
            _const_spec((MLA_KV_LORA, 2 * hw)),
            _const_spec((1, n_g)),
            row_spec(4 * HEAD_PAD),
        ],
        out_specs=[head_spec, head_spec, head_spec,
                   row_spec(n_qs), row_spec(n_kv), row_spec(n_kv), row_spec(n_qm), row_spec(n_g)],
        out_shape=[head_shape, head_shape, head_shape,
                   row_shape(n_qs), row_shape(n_kv), row_shape(n_kv), row_shape(n_qm), row_shape(n_g)],
        compiler_params=pltpu.CompilerParams(
            dimension_semantics=("arbitrary",), vmem_limit_bytes=VMEM_LIMIT),
        name="proj",
    )(x, g, w1, qn, wuq, kvn, wkv, bg, tab)


def _mla_kernel(q_ref, k_ref, v_ref, o_ref):
    i = pl.program_id(1)
    tq, tk = MLA_TQ, MLA_TK
    q = q_ref[0]

    def step(j, carry, masked):
        m, l, acc = carry
        start = pl.multiple_of(j * tk, tk)
        kc = k_ref[0, pl.ds(start, tk), :]
        vc = v_ref[0, pl.ds(start, tk), :]
        s = _dot_nt(q, kc)
        if masked:
            row = lax.broadcasted_iota(jnp.int32, (tq, tk), 0)
            col = lax.broadcasted_iota(jnp.int32, (tq, tk), 1)
            s = jnp.where(col <= row, s, NEG_BIG)
        m_new = jnp.maximum(m, jnp.max(s, axis=-1, keepdims=True))
        alpha = jnp.exp(m - m_new)
        p = jnp.exp(s - m_new)
        l = alpha * l + jnp.sum(p, axis=-1, keepdims=True)
        acc = alpha * acc + _dot(p.astype(BF16), vc)
        return m_new, l, acc

    init = (jnp.full((tq, 1), NEG_BIG, F32), jnp.zeros((tq, 1), F32), jnp.zeros((tq, HEAD_PAD), F32))
    carry = lax.fori_loop(0, i, lambda j, c: step(j, c, False), init)
    m, l, acc = step(i, carry, True)
    o_ref[0] = (acc[:, :MLA_V] / l).astype(BF16)


def _mla_call(q, k, v):
    s = q.shape[1]
    tq = MLA_TQ
    return pl.pallas_call(
        _mla_kernel,
        grid=(MLA_HEADS, s // tq),
        in_specs=[
            pl.BlockSpec((1, tq, HEAD_PAD), lambda h, i: (h, i, 0)),
            pl.BlockSpec((1, s, HEAD_PAD), lambda h, i: (h, 0, 0)),
            pl.BlockSpec((1, s, HEAD_PAD), lambda h, i: (h, 0, 0)),
        ],
        out_specs=pl.BlockSpec((1, tq, MLA_V), lambda h, i: (h, i, 0)),
        out_shape=jax.ShapeDtypeStruct((MLA_HEADS, s, MLA_V), BF16),
        compiler_params=pltpu.CompilerParams(
            dimension_semantics=("arbitrary", "arbitrary"), vmem_limit_bytes=VMEM_LIMIT),
        name="mla_attn",
    )(q, k, v)


def _swa_kernel(sink_ref, q_ref, k_ref, kp_ref, v_ref, vp_ref, bias_ref, o_ref):
    i = pl.program_id(0)
    nblk = SWA_TQ // BLOCK
    rep = SWA_HEADS // SWA_KV_HEADS
    kfull = jnp.concatenate([kp_ref[...], k_ref[...]], axis=0)
    vfull = jnp.concatenate([vp_ref[...], v_ref[...]], axis=0)
    col = lax.broadcasted_iota(jnp.int32, (rep * BLOCK, 2 * BLOCK), 1)
    for b in range(nblk):
        r0 = b * BLOCK
        has_prev = (i * nblk + b) > 0
        valid = jnp.logical_or(col >= BLOCK, has_prev)
        for g in range(SWA_KV_HEADS):
            kk = kfull[r0:r0 + 2 * BLOCK, g * SWA_HD:(g + 1) * SWA_HD]
            vv = vfull[r0:r0 + 2 * BLOCK, g * SWA_HD:(g + 1) * SWA_HD]
            q4 = jnp.concatenate(
                [q_ref[r0:r0 + BLOCK, (g * rep + r) * SWA_HD:(g * rep + r + 1) * SWA_HD]
                 for r in range(rep)], axis=0)
            s = _dot_nt(q4, kk) + bias_ref[g]
            s = jnp.where(valid, s, NEG_BIG)
            sink = jnp.concatenate(
                [jnp.full((BLOCK, 1), sink_ref[g * rep + r], F32) for r in range(rep)], axis=0)
            m = jnp.maximum(jnp.max(s, axis=-1, keepdims=True), sink)
            p = jnp.exp(s - m)
            den = jnp.sum(p, axis=-1, keepdims=True) + jnp.exp(sink - m)
            o = _dot(p.astype(BF16), vv) / den
            for r in range(rep):
                hh = g * rep + r
                o_ref[r0:r0 + BLOCK, hh * SWA_HD:(hh + 1) * SWA_HD] = (
                    o[r * BLOCK:(r + 1) * BLOCK].astype(BF16))


def _swa_call(sinks, qs, ks, vs, bias):
    s = qs.shape[0]
    tq = SWA_TQ
    nblk = tq // BLOCK
    n_kv = SWA_KV_HEADS * SWA_HD
    rep = SWA_HEADS // SWA_KV_HEADS
    own = pl.BlockSpec((tq, n_kv), lambda i: (i, 0))
    prev = pl.BlockSpec((BLOCK, n_kv), lambda i: (jnp.maximum(i * nblk - 1, 0), 0))
    return pl.pallas_call(
        _swa_kernel,
        grid=(s // tq,),
        in_specs=[
            pl.BlockSpec(memory_space=pltpu.SMEM),
            pl.BlockSpec((tq, SWA_HEADS * SWA_HD), lambda i: (i, 0)),
            own, prev, own, prev,
            _const_spec((SWA_KV_HEADS, rep * BLOCK, 2 * BLOCK)),
        ],
        out_specs=pl.BlockSpec((tq, SWA_HEADS * SWA_HD), lambda i: (i, 0)),
        out_shape=jax.ShapeDtypeStruct((s, SWA_HEADS * SWA_HD), BF16),
        compiler_params=pltpu.CompilerParams(
            dimension_semantics=("arbitrary",), vmem_limit_bytes=VMEM_LIMIT),
        name="swa_attn",
    )(sinks, qs, ks, ks, vs, vs, bias)


def _memkv_kernel(mem_ref, g_ref, w_ref, o_ref):
    mn = _rms(mem_ref[...], g_ref[...]).astype(BF16)
    o_ref[...] = _dot(mn, w_ref[...]).astype(BF16)


def _memkv_call(mem, g, w):
    n = 2 * MEM_HEADS * MEM_HD
    return pl.pallas_call(
        _memkv_kernel,
        out_shape=jax.ShapeDtypeStruct((MEM_LEN, n), BF16),
        compiler_params=pltpu.CompilerParams(vmem_limit_bytes=VMEM_LIMIT),
        name="mem_kv",
    )(mem, g, w)


def _mem_kernel(q_ref, kv_ref, o_ref):
    nk = MEM_HEADS * MEM_HD
    for hh in range(MEM_HEADS):
        lo = hh * MEM_HD
        s = _dot_nt(q_ref[:, lo:lo + MEM_HD], kv_ref[:, lo:lo + MEM_HD])
        m = jnp.max(s, axis=-1, keepdims=True)
        p = jnp.exp(s - m)
        den = jnp.sum(p, axis=-1, keepdims=True)
        o = _dot(p.astype(BF16), kv_ref[:, nk + lo:nk + lo + MEM_HD]) / den
        o_ref[:, lo:lo + MEM_HD] = o.astype(BF16)


def _mem_call(qm, kvm):
    s = qm.shape[0]
    tq = MEM_TQ
    n = MEM_HEADS * MEM_HD
    return pl.pallas_call(
        _mem_kernel,
        grid=(s // tq,),
        in_specs=[pl.BlockSpec((tq, n), lambda i: (i, 0)), _const_spec((MEM_LEN, 2 * n))],
        out_specs=pl.BlockSpec((tq, n), lambda i: (i, 0)),
        out_shape=jax.ShapeDtypeStruct((s, n), BF16),
        compiler_params=pltpu.CompilerParams(
            dimension_semantics=("arbitrary",), vmem_limit_bytes=VMEM_LIMIT),
        name="mem_attn",
    )(qm, kvm)


def _post_kernel(x_ref, oa_ref, ob_ref, oc_ref, gate_ref, wa_ref, wb_ref, wc_ref, wout_ref,
                 gn_ref, wup_ref, wdn_ref, fn_ref, o_ref, *, final):
    d = D_MODEL
    y = (gate_ref[:, 0:d].astype(F32) * _dot(oa_ref[...], wa_ref[...])
         + gate_ref[:, d:2 * d].astype(F32) * _dot(ob_ref[...], wb_ref[...])
         + gate_ref[:, 2 * d:3 * d].astype(F32) * _dot(oc_ref[...], wc_ref[...]))
    x1 = x_ref[...] + _dot(y.astype(BF16), wout_ref[...])
    h = _rms(x1, gn_ref[...]).astype(BF16)
    acc = x1
    for c in range(D_FF // FF_CHUNK):
        lo = c * FF_CHUNK
        u = jnp.maximum(_dot(h, wup_ref[:, lo:lo + FF_CHUNK]), 0.0)
        acc = acc + _dot((u * u).astype(BF16), wdn_ref[lo:lo + FF_CHUNK, :])
    if final:
        acc = _rms(acc, fn_ref[...])
    o_ref[...] = acc


def _post_call(x, oa, ob, oc, gates, wa, wb, wc, wout, gn, wup, wdn, fn, final):
    s = x.shape[0]
    tm = POST_TM
    d = D_MODEL

    def row_spec(n):
        return pl.BlockSpec((tm, n), lambda i: (i, 0))

    return pl.pallas_call(
        functools.partial(_post_kernel, final=final),
        grid=(s // tm,),
        in_specs=[
            row_spec(d), row_spec(oa.shape[1]), row_spec(ob.shape[1]), row_spec(oc.shape[1]),
            row_spec(N_BRANCH * d),
            _const_spec(wa.shape), _const_spec(wb.shape), _const_spec(wc.shape), _const_spec((d, d)),
            _const_spec((1, d)), _const_spec((d, D_FF)), _const_spec((D_FF, d)), _const_spec((1, d)),
        ],
        out_specs=row_spec(d),
        out_shape=jax.ShapeDtypeStruct((s, d), F32),
        compiler_params=pltpu.CompilerParams(
            dimension_semantics=("arbitrary",), vmem_limit_bytes=VMEM_LIMIT),
        name="post",
    )(x, oa, ob, oc, gates, wa, wb, wc, wout, gn, wup, wdn, fn)


def _rot_cols(w):
    half = MLA_ROPE // 2
    return jnp.concatenate([-w[..., half:], w[..., :half]], axis=-1)


def _pad_cols(w, left, total):
    return jnp.pad(w, [(0, 0)] * (w.ndim - 1) + [(left, total - left - w.shape[-1])])


def _layer_weights(w_in, w_uq, w_ukv):
    sizes = (MLA_Q_LORA, MLA_KV_LORA, MLA_ROPE, SWA_HEADS * SWA_HD, SWA_KV_HEADS * SWA_HD,
             SWA_KV_HEADS * SWA_HD, MEM_HEADS * MEM_HD, N_BRANCH * D_MODEL)
    offs = [0]
    for n in sizes:
        offs.append(offs[-1] + n)
    cq, ckv, kpe, qs, ks, vs, qm, gt = (w_in[:, offs[t]:offs[t + 1]] for t in range(len(sizes)))
    w1 = jnp.concatenate(
        [cq, ckv, _pad_cols(kpe, MLA_NOPE, HEAD_PAD), _pad_cols(_rot_cols(kpe), MLA_NOPE, HEAD_PAD),
         qs, ks, vs, qm, gt], axis=1).astype(BF16)

    uq = w_uq.reshape(MLA_Q_LORA, MLA_HEADS, MLA_NOPE + MLA_ROPE)
    uq_plain = _pad_cols(uq, 0, HEAD_PAD)
    uq_rot = _pad_cols(_rot_cols(uq[..., MLA_NOPE:]), MLA_NOPE, HEAD_PAD)
    wuq = jnp.concatenate([uq_plain.reshape(MLA_Q_LORA, -1), uq_rot.reshape(MLA_Q_LORA, -1)],
                          axis=1).astype(BF16)

    ukv = w_ukv.reshape(MLA_KV_LORA, MLA_HEADS, MLA_NOPE + MLA_V)
    uk = _pad_cols(ukv[..., :MLA_NOPE], 0, HEAD_PAD)
    uv = _pad_cols(ukv[..., MLA_NOPE:], 0, HEAD_PAD)
    wkv = jnp.concatenate([uk.reshape(MLA_KV_LORA, -1), uv.reshape(MLA_KV_LORA, -1)],
                          axis=1).astype(BF16)
    return w1, wuq, wkv


def _rope_table(seq):
    pos = jnp.arange(seq, dtype=F32)
    inv = 1.0 / (ROPE_THETA ** (jnp.arange(0, MLA_ROPE, 2, dtype=F32) / MLA_ROPE))
    ang = pos[:, None] * inv[None, :]
    cos = jnp.concatenate([jnp.cos(ang)] * 2, axis=-1)
    sin = jnp.concatenate([jnp.sin(ang)] * 2, axis=-1)
    cos_pad = _pad_cols(cos, MLA_NOPE, HEAD_PAD)
    sin_pad = _pad_cols(sin, MLA_NOPE, HEAD_PAD)
    nope = _pad_cols(jnp.ones((seq, MLA_NOPE), F32), 0, HEAD_PAD)
    scale = (MLA_NOPE + MLA_ROPE) ** -0.5
    return jnp.concatenate([scale * (nope + cos_pad), scale * sin_pad, cos_pad, sin_pad], axis=-1)


def _t5_bucket(dist):
    n = jnp.maximum(dist, 0)
    max_exact = REL_BUCKETS // 2
    nf = jnp.maximum(n, 1).astype(F32)
    large = max_exact + (jnp.log(nf / max_exact) / math.log(REL_MAX_DIST / max_exact)
                         * (REL_BUCKETS - max_exact)).astype(jnp.int32)
    large = jnp.minimum(large, REL_BUCKETS - 1)
    return jnp.where(n < max_exact, n, large)


def _swa_bias(rel_bias):
    qi = jnp.arange(BLOCK)[:, None]
    kj = jnp.arange(2 * BLOCK)[None, :]
    dist = qi + BLOCK - kj
    bias = rel_bias.astype(F32)[_t5_bucket(dist)]
    band = (dist >= 0) & (dist < WINDOW)
    bias = jnp.where(band[..., None], bias, NEG_BIG)
    bias = jnp.transpose(bias, (2, 0, 1))
    rep = SWA_HEADS // SWA_KV_HEADS
    return bias.reshape(SWA_KV_HEADS, rep * BLOCK, 2 * BLOCK)


def kernel(x, mem, rel_bias, attn_norm, mem_norm, w_in, b_gate, mla_q_norm, w_uq, mla_kv_norm,
           w_ukv, attn_sinks, w_mem_kv, w_o_mla, w_o_swa, w_o_mem, w_out, mlp_norm, w_up, w_down,
           final_norm):
    batch, seq, d = x.shape
    assert batch == 1 and d == D_MODEL and mem.shape == (1, MEM_LEN, D_MODEL)
    depth = w_in.shape[0]
    xs = x[0]
    mem2 = mem[0]
    tab = _rope_table(seq)
    bias = _swa_bias(rel_bias)
    fn = final_norm.reshape(1, d)

    for l in range(depth):
        w1, wuq, wkv = _layer_weights(w_in[l], w_uq[l], w_ukv[l])
        q, k, v, qs, ks, vs, qm, gates = _proj_call(
            xs, attn_norm[l].reshape(1, d), w1, mla_q_norm[l].reshape(1, -1), wuq,
            mla_kv_norm[l].reshape(1, -1), wkv, b_gate[l].reshape(1, -1), tab)
        o_mla = _mla_call(q, k, v)
        o_mla = jnp.transpose(o_mla, (1, 0, 2)).reshape(seq, MLA_HEADS * MLA_V)
        o_swa = _swa_call(attn_sinks[l], qs, ks, vs, bias)
        kvm = _memkv_call(mem2, mem_norm[l].reshape(1, d), w_mem_kv[l].astype(BF16))
        o_mem = _mem_call(qm, kvm)
        xs = _post_call(
            xs, o_mla, o_swa, o_mem, gates,
            w_o_mla[l].astype(BF16), w_o_swa[l].astype(BF16), w_o_mem[l].astype(BF16),
            w_out[l].astype(BF16), mlp_norm[l].reshape(1, d), w_up[l].astype(BF16),
            w_down[l].astype(BF16), fn, final=(l == depth - 1))
    return xs[None]
```

```python
import functools
import math

import jax
import jax.numpy as jnp
from jax import lax
from jax.experimental import pallas as pl
from jax.experimental.pallas import tpu as pltpu

F32 = jnp.float32
BF16 = jnp.bfloat16

D_MODEL = 1024
MLA_HEADS = 8
MLA_Q_LORA = 256
MLA_KV_LORA = 128
MLA_NOPE = 64
MLA_ROPE = 32
MLA_V = 64
ROPE_THETA = 10000.0
SWA_HEADS = 8
SWA_KV_HEADS = 2
SWA_HD = 64
WINDOW = 128
BLOCK = 128
REL_BUCKETS = 32
REL_MAX_DIST = 128
MEM_LEN = 256
MEM_HEADS = 4
MEM_HD = 128
N_BRANCH = 3
D_FF = 4 * D_MODEL
EPS = 1e-6

HEAD_PAD = 128
BF16_SUBLANES = 16
MLA_ACC_ROWS = MLA_V + BF16_SUBLANES
NEG_BIG = -1e30

PROJ_TM = 512
MLA_TQ = 1024
MLA_TK = 512
SWA_TQ = 512
MEM_TQ = 512
POST_TM = 512
FF_CHUNK = 1024
VMEM_LIMIT = 56 * 1024 * 1024

_C_CQ = 0
_C_CKV = _C_CQ + MLA_Q_LORA
_C_KPE = _C_CKV + MLA_KV_LORA
_C_KROT = _C_KPE + HEAD_PAD
_C_QS = _C_KROT + HEAD_PAD
_C_KS = _C_QS + SWA_HEADS * SWA_HD
_C_VS = _C_KS + SWA_KV_HEADS * SWA_HD
_C_QM = _C_VS + SWA_KV_HEADS * SWA_HD
_C_GATE = _C_QM + MEM_HEADS * MEM_HD
_C_END = _C_GATE + N_BRANCH * D_MODEL


def _rms(x, g):
    return x * lax.rsqrt(jnp.mean(x * x, axis=-1, keepdims=True) + EPS) * g


def _const_spec(shape):
    zeros = (0,) * len(shape)
    return pl.BlockSpec(shape, lambda *_: zeros, pipeline_mode=pl.Buffered(1))


def _dot(a, b):
    return jnp.dot(a, b, preferred_element_type=F32)


def _dot_nt(a, b):
    return lax.dot_general(a, b, (((1,), (1,)), ((), ())), preferred_element_type=F32)


def _proj_kernel(x_ref, g_ref, w1_ref, qn_ref, wuq_ref, kvn_ref, wk_ref, wvt_ref, bg_ref, tab_ref,
                 q_ref, k_ref, vt_ref, qs_ref, ks_ref, vs_ref, qm_ref, gate_ref):
    h = _rms(x_ref[...], g_ref[...]).astype(BF16)

    def mm(lo, hi):
        return _dot(h, w1_ref[:, lo:hi])

    tab = tab_ref[...]
    q_cos, q_sin = tab[:, 0:HEAD_PAD], tab[:, HEAD_PAD:2 * HEAD_PAD]
    k_cos, k_sin = tab[:, 2 * HEAD_PAD:3 * HEAD_PAD], tab[:, 3 * HEAD_PAD:4 * HEAD_PAD]
    hw = MLA_HEADS * HEAD_PAD

    cq = _rms(mm(_C_CQ, _C_CKV), qn_ref[...]).astype(BF16)
    qq = _dot(cq, wuq_ref[...])
    for hh in range(MLA_HEADS):
        lo = hh * HEAD_PAD
        q_ref[hh] = (qq[:, lo:lo + HEAD_PAD] * q_cos
                     + qq[:, hw + lo:hw + lo + HEAD_PAD] * q_sin).astype(BF16)

    ckv = _rms(mm(_C_CKV, _C_KPE), kvn_ref[...]).astype(BF16)
    kn = _dot(ckv, wk_ref[...])
    kpe = mm(_C_KPE, _C_KROT) * k_cos + mm(_C_KROT, _C_QS) * k_sin
    for hh in range(MLA_HEADS):
        lo = hh * HEAD_PAD
        k_ref[hh] = (kn[:, lo:lo + HEAD_PAD] + kpe).astype(BF16)
    vt = _dot_nt(wvt_ref[...], ckv)
    for hh in range(MLA_HEADS):
        vt_ref[hh] = vt[hh * MLA_V:(hh + 1) * MLA_V, :].astype(BF16)

    qs_ref[...] = (mm(_C_QS, _C_KS) * (SWA_HD ** -0.5)).astype(BF16)
    ks_ref[...] = mm(_C_KS, _C_VS).astype(BF16)
    vs_ref[...] = mm(_C_VS, _C_QM).astype(BF16)
    qm_ref[...] = (mm(_C_QM, _C_GATE) * (MEM_HD ** -0.5)).astype(BF16)
    for c in range(N_BRANCH):
        lo = _C_GATE + c * D_MODEL
        z = mm(lo, lo + D_MODEL) + bg_ref[:, c * D_MODEL:(c + 1) * D_MODEL]
        gate_ref[:, c * D_MODEL:(c + 1) * D_MODEL] = (1.0 / (1.0 + jnp.exp(-z))).astype(BF16)


def _proj_call(x, g, w1, qn, wuq, kvn, wk, wvt, bg, tab):
    s = x.shape[0]
    tm = PROJ_TM
    hw = MLA_HEADS * HEAD_PAD
    head_spec = pl.BlockSpec((MLA_HEADS, tm, HEAD_PAD), lambda i: (0, i, 0))
    head_shape = jax.ShapeDtypeStruct((MLA_HEADS, s, HEAD_PAD), BF16)
    vt_spec = pl.BlockSpec((MLA_HEADS, MLA_V, tm), lambda i: (0, 0, i))
    vt_shape = jax.ShapeDtypeStruct((MLA_HEADS, MLA_V, s), BF16)

    def row_spec(n):
        return pl.BlockSpec((tm, n), lambda i: (i, 0))

    def row_shape(n):
        return jax.ShapeDtypeStruct((s, n), BF16)

    n_qs, n_kv, n_qm, n_g = SWA_HEADS * SWA_HD, SWA_KV_HEADS * SWA_HD, MEM_HEADS * MEM_HD, N_BRANCH * D_MODEL
    return pl.pallas_call(
        _proj_kernel,
        grid=(s // tm,),
        in_specs=[
            row_spec(D_MODEL),
            _const_spec((1, D_MODEL)),
            _const_spec((D_MODEL, _C_END)),
            _const_spec((1, MLA_Q_LORA)),
            _const_spec((MLA_Q_LORA, 2 * hw)),
            _const_spec((1, MLA_KV_LORA)),
            _const_spec((MLA_KV_LORA, hw)),
            _const_spec((MLA_HEADS * MLA_V, MLA_KV_LORA)),
            _const_spec((1, n_g)),
            row_spec(4 * HEAD_PAD),
        ],
        out_specs=[head_spec, head_spec, vt_spec,
                   row_spec(n_qs), row_spec(n_kv), row_spec(n_kv), row_spec(n_qm), row_spec(n_g)],
        out_shape=[head_shape, head_shape, vt_shape,
                   row_shape(n_qs), row_shape(n_kv), row_shape(n_kv), row_shape(n_qm), row_shape(n_g)],
        compiler_params=pltpu.CompilerParams(
            dimension_semantics=("arbitrary",), vmem_limit_bytes=VMEM_LIMIT),
        name="proj",
    )(x, g, w1, qn, wuq, kvn, wk, wvt, bg, tab)


def _mla_kernel(q_ref, k_ref, vt_ref, o_ref, sa_scr, sb_scr, acc_scr):
    i = pl.program_id(1)
    tq, tk = MLA_TQ, MLA_TK
    assert tq == 2 * tk
    q = q_ref[0]
    key = lax.broadcasted_iota(jnp.int32, (tk, tq), 0)
    qry = lax.broadcasted_iota(jnp.int32, (tk, tq), 1)

    def scores(j, buf, mask):
        start = pl.multiple_of(j * tk, tk)
        s = _dot_nt(k_ref[0, pl.ds(start, tk), :], q)
        if mask is not None:
            s = jnp.where(mask, s, NEG_BIG)
        buf[...] = s
        return jnp.max(s, axis=0, keepdims=True)

    ones_rows = (lax.broadcasted_iota(jnp.int32, (MLA_ACC_ROWS - MLA_V, tk), 0) == 0).astype(BF16)

    def accumulate(j, buf, m, cm):
        start = pl.multiple_of(j * tk, tk)
        m_new = jnp.maximum(m, cm)
        alpha = jnp.exp2(m - m_new)
        p = jnp.exp2(buf[...] - m_new).astype(BF16)
        v1 = jnp.concatenate([vt_ref[0, :, pl.ds(start, tk)], ones_rows], axis=0)
        acc_scr[...] = alpha * acc_scr[...] + _dot(v1, p)
        return m_new

    def trip(t, carry, mask_next):
        m, cm_a = carry
        cm_b = scores(2 * t + 1, sb_scr, None)
        m = accumulate(2 * t, sa_scr, m, cm_a)
        cm_a = scores(2 * t + 2, sa_scr, mask_next)
        m = accumulate(2 * t + 1, sb_scr, m, cm_b)
        return m, cm_a

    diag_a = key <= qry
    diag_b = key + tk <= qry
    acc_scr[...] = jnp.zeros_like(acc_scr)
    cm_a = scores(0, sa_scr, jnp.logical_or(diag_a, i > 0))
    carry = (jnp.full((1, tq), NEG_BIG, F32), cm_a)
    n_plain = jnp.maximum(i - 1, 0)
    carry = lax.fori_loop(0, n_plain, lambda t, c: trip(t, c, None), carry)
    carry = lax.fori_loop(n_plain, i, lambda t, c: trip(t, c, diag_a), carry)
    m, cm_a = carry
    cm_b = scores(2 * i + 1, sb_scr, diag_b)
    m = accumulate(2 * i, sa_scr, m, cm_a)
    m = accumulate(2 * i + 1, sb_scr, m, cm_b)
    o_ref[0] = (acc_scr[0:MLA_V, :] / acc_scr[MLA_V:MLA_V + 1, :]).astype(BF16)


def _mla_call(q, k, vt):
    s = q.shape[1]
    tq = MLA_TQ
    return pl.pallas_call(
        _mla_kernel,
        grid=(MLA_HEADS, s // tq),
        in_specs=[
            pl.BlockSpec((1, tq, HEAD_PAD), lambda h, i: (h, i, 0)),
            pl.BlockSpec((1, s, HEAD_PAD), lambda h, i: (h, 0, 0)),
            pl.BlockSpec((1, MLA_V, s), lambda h, i: (h, 0, 0)),
        ],
        out_specs=pl.BlockSpec((1, MLA_V, tq), lambda h, i: (h, 0, i)),
        out_shape=jax.ShapeDtypeStruct((MLA_HEADS, MLA_V, s), BF16),
        scratch_shapes=[pltpu.VMEM((MLA_TK, tq), F32), pltpu.VMEM((MLA_TK, tq), F32),
                        pltpu.VMEM((MLA_ACC_ROWS, tq), F32)],
        compiler_params=pltpu.CompilerParams(
            dimension_semantics=("arbitrary", "arbitrary"), vmem_limit_bytes=VMEM_LIMIT),
        name="mla_attn",
    )(q, k, vt)


def _swa_kernel(sink_ref, q_ref, k_ref, kp_ref, v_ref, vp_ref, bias_ref, o_ref):
    i = pl.program_id(0)
    nblk = SWA_TQ // BLOCK
    rep = SWA_HEADS // SWA_KV_HEADS
    kfull = jnp.concatenate([kp_ref[...], k_ref[...]], axis=0)
    vfull = jnp.concatenate([vp_ref[...], v_ref[...]], axis=0)
    col = lax.broadcasted_iota(jnp.int32, (rep * BLOCK, 2 * BLOCK), 1)
    for b in range(nblk):
        r0 = b * BLOCK
        has_prev = (i * nblk + b) > 0
        valid = jnp.logical_or(col >= BLOCK, has_prev)
        for g in range(SWA_KV_HEADS):
            kk = kfull[r0:r0 + 2 * BLOCK, g * SWA_HD:(g + 1) * SWA_HD]
            vv = vfull[r0:r0 + 2 * BLOCK, g * SWA_HD:(g + 1) * SWA_HD]
            q4 = jnp.concatenate(
                [q_ref[r0:r0 + BLOCK, (g * rep + r) * SWA_HD:(g * rep + r + 1) * SWA_HD]
                 for r in range(rep)], axis=0)
            s = _dot_nt(q4, kk) + bias_ref[g]
            s = jnp.where(valid, s, NEG_BIG)
            sink = jnp.concatenate(
                [jnp.full((BLOCK, 1), sink_ref[g * rep + r], F32) for r in range(rep)], axis=0)
            m = jnp.maximum(jnp.max(s, axis=-1, keepdims=True), sink)
            p = jnp.exp(s - m)
            den = jnp.sum(p, axis=-1, keepdims=True) + jnp.exp(sink - m)
            o = _dot(p.astype(BF16), vv) / den
            for r in range(rep):
                hh = g * rep + r
                o_ref[r0:r0 + BLOCK, hh * SWA_HD:(hh + 1) * SWA_HD] = (
                    o[r * BLOCK:(r + 1) * BLOCK].astype(BF16))


def _swa_call(sinks, qs, ks, vs, bias):
    s = qs.shape[0]
    tq = SWA_TQ
    nblk = tq // BLOCK
    n_kv = SWA_KV_HEADS * SWA_HD
    rep = SWA_HEADS // SWA_KV_HEADS
    own = pl.BlockSpec((tq, n_kv), lambda i: (i, 0))
    prev = pl.BlockSpec((BLOCK, n_kv), lambda i: (jnp.maximum(i * nblk - 1, 0), 0))
    return pl.pallas_call(
        _swa_kernel,
        grid=(s // tq,),
        in_specs=[
            pl.BlockSpec(memory_space=pltpu.SMEM),
            pl.BlockSpec((tq, SWA_HEADS * SWA_HD), lambda i: (i, 0)),
            own, prev, own, prev,
            _const_spec((SWA_KV_HEADS, rep * BLOCK, 2 * BLOCK)),
        ],
        out_specs=pl.BlockSpec((tq, SWA_HEADS * SWA_HD), lambda i: (i, 0)),
        out_shape=jax.ShapeDtypeStruct((s, SWA_HEADS * SWA_HD), BF16),
        compiler_params=pltpu.CompilerParams(
            dimension_semantics=("arbitrary",), vmem_limit_bytes=VMEM_LIMIT),
        name="swa_attn",
    )(sinks, qs, ks, ks, vs, vs, bias)


def _memkv_kernel(mem_ref, g_ref, w_ref, o_ref):
    mn = _rms(mem_ref[...], g_ref[...]).astype(BF16)
    o_ref[...] = _dot(mn, w_ref[...]).astype(BF16)


def _memkv_call(mem, g, w):
    n = 2 * MEM_HEADS * MEM_HD
    return pl.pallas_call(
        _memkv_kernel,
        out_shape=jax.ShapeDtypeStruct((MEM_LEN, n), BF16),
        compiler_params=pltpu.CompilerParams(vmem_limit_bytes=VMEM_LIMIT),
        name="mem_kv",
    )(mem, g, w)


def _mem_kernel(q_ref, kv_ref, o_ref):
    nk = MEM_HEADS * MEM_HD
    for hh in range(MEM_HEADS):
        lo = hh * MEM_HD
        s = _dot_nt(q_ref[:, lo:lo + MEM_HD], kv_ref[:, lo:lo + MEM_HD])
        m = jnp.max(s, axis=-1, keepdims=True)
        p = jnp.exp(s - m)
        den = jnp.sum(p, axis=-1, keepdims=True)
        o = _dot(p.astype(BF16), kv_ref[:, nk + lo:nk + lo + MEM_HD]) / den
        o_ref[:, lo:lo + MEM_HD] = o.astype(BF16)


def _mem_call(qm, kvm):
    s = qm.shape[0]
    tq = MEM_TQ
    n = MEM_HEADS * MEM_HD
    return pl.pallas_call(
        _mem_kernel,
        grid=(s // tq,),
        in_specs=[pl.BlockSpec((tq, n), lambda i: (i, 0)), _const_spec((MEM_LEN, 2 * n))],
        out_specs=pl.BlockSpec((tq, n), lambda i: (i, 0)),
        out_shape=jax.ShapeDtypeStruct((s, n), BF16),
        compiler_params=pltpu.CompilerParams(
            dimension_semantics=("arbitrary",), vmem_limit_bytes=VMEM_LIMIT),
        name="mem_attn",
    )(qm, kvm)


def _post_kernel(x_ref, oa_ref, ob_ref, oc_ref, gate_ref, wa_ref, wb_ref, wc_ref, wout_ref,
                 gn_ref, wup_ref, wdn_ref, fn_ref, o_ref, *, final):
    d = D_MODEL
    y = (gate_ref[:, 0:d].astype(F32) * _dot(oa_ref[...], wa_ref[...])
         + gate_ref[:, d:2 * d].astype(F32) * _dot(ob_ref[...], wb_ref[...])
         + gate_ref[:, 2 * d:3 * d].astype(F32) * _dot(oc_ref[...], wc_ref[...]))
    x1 = x_ref[...] + _dot(y.astype(BF16), wout_ref[...])
    h = _rms(x1, gn_ref[...]).astype(BF16)
    acc = x1
    for c in range(D_FF // FF_CHUNK):
        lo = c * FF_CHUNK
        u = jnp.maximum(_dot(h, wup_ref[:, lo:lo + FF_CHUNK]), 0.0)
        acc = acc + _dot((u * u).astype(BF16), wdn_ref[lo:lo + FF_CHUNK, :])
    if final:
        acc = _rms(acc, fn_ref[...])
    o_ref[...] = acc


def _post_call(x, oa, ob, oc, gates, wa, wb, wc, wout, gn, wup, wdn, fn, final):
    s = x.shape[0]
    tm = POST_TM
    d = D_MODEL

    def row_spec(n):
        return pl.BlockSpec((tm, n), lambda i: (i, 0))

    return pl.pallas_call(
        functools.partial(_post_kernel, final=final),
        grid=(s // tm,),
        in_specs=[
            row_spec(d), row_spec(oa.shape[1]), row_spec(ob.shape[1]), row_spec(oc.shape[1]),
            row_spec(N_BRANCH * d),
            _const_spec(wa.shape), _const_spec(wb.shape), _const_spec(wc.shape), _const_spec((d, d)),
            _const_spec((1, d)), _const_spec((d, D_FF)), _const_spec((D_FF, d)), _const_spec((1, d)),
        ],
        out_specs=row_spec(d),
        out_shape=jax.ShapeDtypeStruct((s, d), F32),
        compiler_params=pltpu.CompilerParams(
            dimension_semantics=("arbitrary",), vmem_limit_bytes=VMEM_LIMIT),
        name="post",
    )(x, oa, ob, oc, gates, wa, wb, wc, wout, gn, wup, wdn, fn)


def _rot_cols(w):
    half = MLA_ROPE // 2
    return jnp.concatenate([-w[..., half:], w[..., :half]], axis=-1)


def _pad_cols(w, left, total):
    return jnp.pad(w, [(0, 0)] * (w.ndim - 1) + [(left, total - left - w.shape[-1])])


def _layer_weights(w_in, w_uq, w_ukv):
    sizes = (MLA_Q_LORA, MLA_KV_LORA, MLA_ROPE, SWA_HEADS * SWA_HD, SWA_KV_HEADS * SWA_HD,
             SWA_KV_HEADS * SWA_HD, MEM_HEADS * MEM_HD, N_BRANCH * D_MODEL)
    offs = [0]
    for n in sizes:
        offs.append(offs[-1] + n)
    cq, ckv, kpe, qs, ks, vs, qm, gt = (w_in[:, offs[t]:offs[t + 1]] for t in range(len(sizes)))
    w1 = jnp.concatenate(
        [cq, ckv, _pad_cols(kpe, MLA_NOPE, HEAD_PAD), _pad_cols(_rot_cols(kpe), MLA_NOPE, HEAD_PAD),
         qs, ks, vs, qm, gt], axis=1).astype(BF16)

    uq = w_uq.reshape(MLA_Q_LORA, MLA_HEADS, MLA_NOPE + MLA_ROPE)
    uq_plain = _pad_cols(uq, 0, HEAD_PAD)
    uq_rot = _pad_cols(_rot_cols(uq[..., MLA_NOPE:]), MLA_NOPE, HEAD_PAD)
    wuq = jnp.concatenate([uq_plain.reshape(MLA_Q_LORA, -1), uq_rot.reshape(MLA_Q_LORA, -1)],
                          axis=1).astype(BF16)

    ukv = w_ukv.reshape(MLA_KV_LORA, MLA_HEADS, MLA_NOPE + MLA_V)
    wk = _pad_cols(ukv[..., :MLA_NOPE], 0, HEAD_PAD).reshape(MLA_KV_LORA, -1).astype(BF16)
    wvt = ukv[..., MLA_NOPE:].reshape(MLA_KV_LORA, -1).T.astype(BF16)
    return w1, wuq, wk, wvt


def _rope_table(seq):
    pos = jnp.arange(seq, dtype=F32)
    inv = 1.0 / (ROPE_THETA ** (jnp.arange(0, MLA_ROPE, 2, dtype=F32) / MLA_ROPE))
    ang = pos[:, None] * inv[None, :]
    cos = jnp.concatenate([jnp.cos(ang)] * 2, axis=-1)
    sin = jnp.concatenate([jnp.sin(ang)] * 2, axis=-1)
    cos_pad = _pad_cols(cos, MLA_NOPE, HEAD_PAD)
    sin_pad = _pad_cols(sin, MLA_NOPE, HEAD_PAD)
    nope = _pad_cols(jnp.ones((seq, MLA_NOPE), F32), 0, HEAD_PAD)
    scale = (MLA_NOPE + MLA_ROPE) ** -0.5 * math.log2(math.e)
    return jnp.concatenate([scale * (nope + cos_pad), scale * sin_pad, cos_pad, sin_pad], axis=-1)


def _t5_bucket(dist):
    n = jnp.maximum(dist, 0)
    max_exact = REL_BUCKETS // 2
    nf = jnp.maximum(n, 1).astype(F32)
    large = max_exact + (jnp.log(nf / max_exact) / math.log(REL_MAX_DIST / max_exact)
                         * (REL_BUCKETS - max_exact)).astype(jnp.int32)
    large = jnp.minimum(large, REL_BUCKETS - 1)
    return jnp.where(n < max_exact, n, large)


def _swa_bias(rel_bias):
    qi = jnp.arange(BLOCK)[:, None]
    kj = jnp.arange(2 * BLOCK)[None, :]
    dist = qi + BLOCK - kj
    bias = rel_bias.astype(F32)[_t5_bucket(dist)]
    band = (dist >= 0) & (dist < WINDOW)
    bias = jnp.where(band[..., None], bias, NEG_BIG)
    bias = jnp.transpose(bias, (2, 0, 1))
    rep = SWA_HEADS // SWA_KV_HEADS
    return bias.reshape(SWA_KV_HEADS, rep * BLOCK, 2 * BLOCK)


def kernel(x, mem, rel_bias, attn_norm, mem_norm, w_in, b_gate, mla_q_norm, w_uq, mla_kv_norm,
           w_ukv, attn_sinks, w_mem_kv, w_o_mla, w_o_swa, w_o_mem, w_out, mlp_norm, w_up, w_down,
           final_norm):
    batch, seq, d = x.shape
    assert batch == 1 and d == D_MODEL and mem.shape == (1, MEM_LEN, D_MODEL)
    depth = w_in.shape[0]
    xs = x[0]
    mem2 = mem[0]
    tab = _rope_table(seq)
    bias = _swa_bias(rel_bias)
    fn = final_norm.reshape(1, d)

    for l in range(depth):
        w1, wuq, wk, wvt = _layer_weights(w_in[l], w_uq[l], w_ukv[l])
        q, k, vt, qs, ks, vs, qm, gates = _proj_call(
            xs, attn_norm[l].reshape(1, d), w1, mla_q_norm[l].reshape(1, -1), wuq,
            mla_kv_norm[l].reshape(1, -1), wk, wvt, b_gate[l].reshape(1, -1), tab)
        o_mla = _mla_call(q, k, vt)
        o_mla = o_mla.reshape(MLA_HEADS * MLA_V, seq).T
        o_swa = _swa_call(attn_sinks[l], qs, ks, vs, bias)
        kvm = _memkv_call(mem2, mem_norm[l].reshape(1, d), w_mem_kv[l].astype(BF16))
        o_mem = _mem_call(qm, kvm)
        xs = _post_call(
            xs, o_mla, o_swa, o_mem, gates,
            w_o_mla[l].astype(BF16), w_o_swa[l].astype(BF16), w_o_mem[l].astype(BF16),
            w_out[l].astype(BF16), mlp_norm[l].reshape(1, d), w_up[l].astype(BF16),
            w_down[l].astype(BF16), fn, final=(l == depth - 1))
    return xs[None]
```

```python
import functools
import math

import jax
import jax.numpy as jnp
from jax import lax
from jax.experimental import pallas as pl
from jax.experimental.pallas import tpu as pltpu

F32 = jnp.float32
BF16 = jnp.bfloat16

D_MODEL = 1024
MLA_HEADS = 8
MLA_Q_LORA = 256
MLA_KV_LORA = 128
MLA_NOPE = 64
MLA_ROPE = 32
MLA_V = 64
ROPE_THETA = 10000.0
SWA_HEADS = 8
SWA_KV_HEADS = 2
SWA_HD = 64
WINDOW = 128
BLOCK = 128
REL_BUCKETS = 32
REL_MAX_DIST = 128
MEM_LEN = 256
MEM_HEADS = 4
MEM_HD = 128
N_BRANCH = 3
D_FF = 4 * D_MODEL
EPS = 1e-6

HEAD_PAD = 128
BF16_SUBLANES = 16
MLA_ACC_ROWS = MLA_V + BF16_SUBLANES
NEG_BIG = -1e30
LOG2E = math.log2(math.e)
LANES = 128

PROJ_TM = 512
MLA_TQ = 1024
MLA_TK = 512
MLA_TC = 256
MLA_LOOKAHEAD = 2
SWA_TQ = 512
SWA_LOOKAHEAD = 5
MEM_TQ = 512
POST_TM = 512
FF_CHUNK = 1024
VMEM_LIMIT = 56 * 1024 * 1024

_C_CQ = 0
_C_CKV = _C_CQ + MLA_Q_LORA
_C_KPE = _C_CKV + MLA_KV_LORA
_C_KROT = _C_KPE + HEAD_PAD
_C_QS = _C_KROT + HEAD_PAD
_C_KS = _C_QS + SWA_HEADS * SWA_HD
_C_KSW = _C_KS + SWA_KV_HEADS * SWA_HD
_C_QM = _C_KSW + SWA_KV_HEADS * SWA_HD
_C_GATE = _C_QM + MEM_HEADS * MEM_HD
_C_END = _C_GATE + N_BRANCH * D_MODEL


def _rms(x, g):
    return x * lax.rsqrt(jnp.mean(x * x, axis=-1, keepdims=True) + EPS) * g


def _const_spec(shape):
    zeros = (0,) * len(shape)
    return pl.BlockSpec(shape, lambda *_: zeros, pipeline_mode=pl.Buffered(1))


def _dot(a, b):
    return jnp.dot(a, b, preferred_element_type=F32)


def _dot_nt(a, b):
    return lax.dot_general(a, b, (((1,), (1,)), ((), ())), preferred_element_type=F32)


def _proj_kernel(x_ref, g_ref, w1_ref, qn_ref, wuq_ref, kvn_ref, wk_ref, wvt_ref, wvst_ref, bg_ref,
                 tab_ref, q_ref, k_ref, vt_ref, qs_ref, ks_ref, ksw_ref, vst_ref, qm_ref, gate_ref):
    h = _rms(x_ref[...], g_ref[...]).astype(BF16)

    def mm(lo, hi):
        return _dot(h, w1_ref[:, lo:hi])

    tab = tab_ref[...]
    q_cos, q_sin = tab[:, 0:HEAD_PAD], tab[:, HEAD_PAD:2 * HEAD_PAD]
    k_cos, k_sin = tab[:, 2 * HEAD_PAD:3 * HEAD_PAD], tab[:, 3 * HEAD_PAD:4 * HEAD_PAD]
    hw = MLA_HEADS * HEAD_PAD

    cq = _rms(mm(_C_CQ, _C_CKV), qn_ref[...]).astype(BF16)
    qq = _dot(cq, wuq_ref[...])
    for hh in range(MLA_HEADS):
        lo = hh * HEAD_PAD
        q_ref[hh] = (qq[:, lo:lo + HEAD_PAD] * q_cos
                     + qq[:, hw + lo:hw + lo + HEAD_PAD] * q_sin).astype(BF16)

    ckv = _rms(mm(_C_CKV, _C_KPE), kvn_ref[...]).astype(BF16)
    kn = _dot(ckv, wk_ref[...])
    kpe = mm(_C_KPE, _C_KROT) * k_cos + mm(_C_KROT, _C_QS) * k_sin
    for hh in range(MLA_HEADS):
        lo = hh * HEAD_PAD
        k_ref[hh] = (kn[:, lo:lo + HEAD_PAD] + kpe).astype(BF16)
    vt = _dot_nt(wvt_ref[...], ckv)
    for hh in range(MLA_HEADS):
        vt_ref[hh] = vt[hh * MLA_V:(hh + 1) * MLA_V, :].astype(BF16)

    qs_ref[...] = (mm(_C_QS, _C_KS) * (SWA_HD ** -0.5 * LOG2E)).astype(BF16)
    ks_ref[...] = mm(_C_KS, _C_KSW).astype(BF16)
    ksw_ref[...] = mm(_C_KSW, _C_QM).astype(BF16)
    vst_ref[...] = _dot_nt(wvst_ref[...], h).astype(BF16)
    qm_ref[...] = (mm(_C_QM, _C_GATE) * (MEM_HD ** -0.5)).astype(BF16)
    for c in range(N_BRANCH):
        lo = _C_GATE + c * D_MODEL
        z = mm(lo, lo + D_MODEL) + bg_ref[:, c * D_MODEL:(c + 1) * D_MODEL]
        gate_ref[:, c * D_MODEL:(c + 1) * D_MODEL] = (1.0 / (1.0 + jnp.exp(-z))).astype(BF16)


def _proj_call(x, g, w1, qn, wuq, kvn, wk, wvt, wvst, bg, tab):
    s = x.shape[0]
    tm = PROJ_TM
    hw = MLA_HEADS * HEAD_PAD
    head_spec = pl.BlockSpec((MLA_HEADS, tm, HEAD_PAD), lambda i: (0, i, 0))
    head_shape = jax.ShapeDtypeStruct((MLA_HEADS, s, HEAD_PAD), BF16)
    vt_spec = pl.BlockSpec((MLA_HEADS, MLA_V, tm), lambda i: (0, 0, i))
    vt_shape = jax.ShapeDtypeStruct((MLA_HEADS, MLA_V, s), BF16)

    def row_spec(n):
        return pl.BlockSpec((tm, n), lambda i: (i, 0))

    def row_shape(n):
        return jax.ShapeDtypeStruct((s, n), BF16)

    n_qs, n_kv, n_qm, n_g = SWA_HEADS * SWA_HD, SWA_KV_HEADS * SWA_HD, MEM_HEADS * MEM_HD, N_BRANCH * D_MODEL
    return pl.pallas_call(
        _proj_kernel,
        grid=(s // tm,),
        in_specs=[
            row_spec(D_MODEL),
            _const_spec((1, D_MODEL)),
            _const_spec((D_MODEL, _C_END)),
            _const_spec((1, MLA_Q_LORA)),
            _const_spec((MLA_Q_LORA, 2 * hw)),
            _const_spec((1, MLA_KV_LORA)),
            _const_spec((MLA_KV_LORA, hw)),
            _const_spec((MLA_HEADS * MLA_V, MLA_KV_LORA)),
            _const_spec((n_kv, D_MODEL)),
            _const_spec((1, n_g)),
            row_spec(4 * HEAD_PAD),
        ],
        out_specs=[head_spec, head_spec, vt_spec,
                   row_spec(n_qs), row_spec(n_kv), row_spec(n_kv),
                   pl.BlockSpec((n_kv, tm), lambda i: (0, i)), row_spec(n_qm), row_spec(n_g)],
        out_shape=[head_shape, head_shape, vt_shape,
                   row_shape(n_qs), row_shape(n_kv), row_shape(n_kv),
                   jax.ShapeDtypeStruct((n_kv, s), BF16), row_shape(n_qm), row_shape(n_g)],
        compiler_params=pltpu.CompilerParams(
            dimension_semantics=("arbitrary",), vmem_limit_bytes=VMEM_LIMIT),
        name="proj",
    )(x, g, w1, qn, wuq, kvn, wk, wvt, wvst, bg, tab)


def _mla_kernel(q_ref, k_ref, vt_ref, o_ref, sa_scr, sb_scr, acc_scr):
    i = pl.program_id(1)
    tq, tk, tc = MLA_TQ, MLA_TK, MLA_TC
    nc = tq // tc
    assert tq == 2 * tk and tk == 2 * tc
    key = lax.broadcasted_iota(jnp.int32, (tk, tc), 0)
    lane = lax.broadcasted_iota(jnp.int32, (tk, tc), 1)
    ones_rows = (lax.broadcasted_iota(jnp.int32, (MLA_ACC_ROWS - MLA_V, tk), 0) == 0).astype(BF16)

    def scores(j, buf, c, key_off=None, or_valid=None):
        start = pl.multiple_of(j * tk, tk)
        s = _dot_nt(k_ref[0, pl.ds(start, tk), :], q_ref[0, c * tc:(c + 1) * tc, :])
        if key_off is not None:
            valid = key + key_off <= lane + c * tc
            if or_valid is not None:
                valid = jnp.logical_or(valid, or_valid)
            s = jnp.where(valid, s, NEG_BIG)
        buf[:, c * tc:(c + 1) * tc] = s
        return jnp.max(s, axis=0, keepdims=True)

    def accumulate(j, buf, c, m, cm):
        start = pl.multiple_of(j * tk, tk)
        m_new = jnp.maximum(m, cm)
        alpha = jnp.exp2(m - m_new)
        p = jnp.exp2(buf[:, c * tc:(c + 1) * tc] - m_new).astype(BF16)
        v1 = jnp.concatenate([vt_ref[0, :, pl.ds(start, tk)], ones_rows], axis=0)
        acc_scr[:, c * tc:(c + 1) * tc] = (
            alpha * acc_scr[:, c * tc:(c + 1) * tc] + _dot(v1, p))
        return m_new

    def trip(t, carry, next_on_diagonal):
        m, cm_a = list(carry[0]), carry[1]
        cm_b, cm_next = [None] * nc, [None] * nc
        todo = [("b", c) for c in range(nc)] + [("next", c) for c in range(nc)]

        def issue():
            which, c = todo.pop(0)
            if which == "b":
                cm_b[c] = scores(2 * t + 1, sb_scr, c)
            else:
                masked = next_on_diagonal and c * tc < tk
                cm_next[c] = scores(2 * t + 2, sa_scr, c, key_off=0 if masked else None)

        for _ in range(MLA_LOOKAHEAD):
            issue()
        for which, c in [("a", c) for c in range(nc)] + [("b", c) for c in range(nc)]:
            if which == "a":
                m[c] = accumulate(2 * t, sa_scr, c, m[c], cm_a[c])
            else:
                m[c] = accumulate(2 * t + 1, sb_scr, c, m[c], cm_b[c])
            if todo:
                issue()
        return tuple(m), tuple(cm_next)

    acc_scr[...] = jnp.zeros_like(acc_scr)
    cm_a = tuple(scores(0, sa_scr, c, key_off=0, or_valid=i > 0) for c in range(nc))
    carry = (tuple(jnp.full((1, tc), NEG_BIG, F32) for _ in range(nc)), cm_a)
    n_plain = jnp.maximum(i - 1, 0)
    carry = lax.fori_loop(0, n_plain, lambda t, c: trip(t, c, False), carry)
    carry = lax.fori_loop(n_plain, i, lambda t, c: trip(t, c, True), carry)
    m, cm_a = list(carry[0]), carry[1]
    late = [c for c in range(nc) if (c + 1) * tc > tk]
    cm_b = {c: scores(2 * i + 1, sb_scr, c, key_off=tk) for c in late}
    for c in range(nc):
        m[c] = accumulate(2 * i, sa_scr, c, m[c], cm_a[c])
    for c in late:
        m[c] = accumulate(2 * i + 1, sb_scr, c, m[c], cm_b[c])
    o_ref[0] = (acc_scr[0:MLA_V, :] / acc_scr[MLA_V:MLA_V + 1, :]).astype(BF16)


def _mla_call(q, k, vt):
    s = q.shape[1]
    tq = MLA_TQ
    return pl.pallas_call(
        _mla_kernel,
        grid=(MLA_HEADS, s // tq),
        in_specs=[
            pl.BlockSpec((1, tq, HEAD_PAD), lambda h, i: (h, i, 0)),
            pl.BlockSpec((1, s, HEAD_PAD), lambda h, i: (h, 0, 0)),
            pl.BlockSpec((1, MLA_V, s), lambda h, i: (h, 0, 0)),
        ],
        out_specs=pl.BlockSpec((1, MLA_V, tq), lambda h, i: (h, 0, i)),
        out_shape=jax.ShapeDtypeStruct((MLA_HEADS, MLA_V, s), BF16),
        scratch_shapes=[pltpu.VMEM((MLA_TK, tq), F32), pltpu.VMEM((MLA_TK, tq), F32),
                        pltpu.VMEM((MLA_ACC_ROWS, tq), F32)],
        compiler_params=pltpu.CompilerParams(
            dimension_semantics=("arbitrary", "arbitrary"), vmem_limit_bytes=VMEM_LIMIT),
        name="mla_attn",
    )(q, k, vt)


def _swa_kernel(relb_ref, sink_ref, bkt_ref, q_ref, k_ref, kp_ref, ksw_ref, kswp_ref, vt_ref, vtp_ref,
                o_ref, bias_scr):
    i = pl.program_id(0)
    tq, hd, rep = SWA_TQ, SWA_HD, SWA_HEADS // SWA_KV_HEADS
    win = 2 * BLOCK
    assert 2 * hd == LANES and rep == 4 and BLOCK == LANES

    @pl.when(i == 0)
    def _build_bias():
        bkt = bkt_ref[...]
        for hh in range(SWA_HEADS):
            t = jnp.full((win, BLOCK), NEG_BIG, F32)
            for b in range(REL_BUCKETS):
                t = jnp.where(bkt == b, relb_ref[b * SWA_HEADS + hh] * LOG2E, t)
            g, r = divmod(hh, rep)
            bias_scr[2 * g + r % 2, :, (r // 2) * BLOCK:(r // 2 + 1) * BLOCK] = t

    lo_half = lax.broadcasted_iota(jnp.int32, (BLOCK + tq, LANES), 1) < hd
    kfull = jnp.concatenate([kp_ref[...], k_ref[...]], axis=0)
    kswfull = jnp.concatenate([kswp_ref[...], ksw_ref[...]], axis=0)
    zero = jnp.zeros_like(kfull)
    k_lo = (jnp.where(lo_half, kfull, zero), jnp.where(lo_half, kswfull, zero))
    k_hi = (jnp.where(lo_half, zero, kswfull), jnp.where(lo_half, zero, kfull))
    vfull = jnp.concatenate([vtp_ref[...], vt_ref[...]], axis=1)
    ones_rows = (lax.broadcasted_iota(jnp.int32, (BF16_SUBLANES, win), 0) == 0).astype(BF16)
    key_row = lax.broadcasted_iota(jnp.int32, (win, 2 * BLOCK), 0)
    first_valid = jnp.logical_or(key_row >= BLOCK, i > 0)

    sinks = [jnp.concatenate(
        [jnp.full((1, BLOCK), sink_ref[g * rep + half] * LOG2E, F32),
         jnp.full((1, BLOCK), sink_ref[g * rep + 2 + half] * LOG2E, F32)], axis=1)
        for g in range(SWA_KV_HEADS) for half in range(2)]

    tiles = [(b, g, half) for b in range(tq // BLOCK) for g in range(SWA_KV_HEADS) for half in range(2)]

    def tile_scores(b, g, half):
        r0 = b * BLOCK
        qg = jnp.concatenate([q_ref[r0:r0 + BLOCK, (2 * g) * LANES:(2 * g + 1) * LANES],
                              q_ref[r0:r0 + BLOCK, (2 * g + 1) * LANES:(2 * g + 2) * LANES]],
                             axis=0)
        kmat = (k_lo, k_hi)[half][g]
        s = _dot_nt(kmat[r0:r0 + win], qg) + bias_scr[2 * g + half]
        return jnp.where(first_valid, s, NEG_BIG) if b == 0 else s

    def tile_output(b, g, half, s):
        r0 = b * BLOCK
        sink = sinks[2 * g + half]
        v1 = jnp.concatenate([vfull[g * hd:(g + 1) * hd, r0:r0 + win], ones_rows], axis=0)
        m = jnp.maximum(jnp.max(s, axis=0, keepdims=True), sink)
        p = jnp.exp2(s - m).astype(BF16)
        ot = _dot(v1, p)
        den = ot[hd:hd + 1] + jnp.exp2(sink - m)
        return ot[0:hd] / den

    pending = [tile_scores(*tile) for tile in tiles[:SWA_LOOKAHEAD]]
    outs = {}
    for t, (b, g, half) in enumerate(tiles):
        s_cur = pending.pop(0)
        if t + SWA_LOOKAHEAD < len(tiles):
            pending.append(tile_scores(*tiles[t + SWA_LOOKAHEAD]))
        outs[half] = tile_output(b, g, half, s_cur)
        if half == 1:
            r0 = b * BLOCK
            for pr in range(2):
                x = jnp.concatenate([outs[0][:, pr * BLOCK:(pr + 1) * BLOCK],
                                     outs[1][:, pr * BLOCK:(pr + 1) * BLOCK]], axis=0)
                o_ref[r0:r0 + BLOCK, (2 * g + pr) * LANES:(2 * g + pr + 1) * LANES] = x.T.astype(BF16)


def _swa_call(rel_bias, sinks, bkt, qs, ks, ksw, vst):
    s = qs.shape[0]
    tq = SWA_TQ
    nblk = tq // BLOCK
    n_kv = SWA_KV_HEADS * SWA_HD
    own = pl.BlockSpec((tq, n_kv), lambda i: (i, 0))
    prev = pl.BlockSpec((BLOCK, n_kv), lambda i: (jnp.maximum(i * nblk - 1, 0), 0))
    own_t = pl.BlockSpec((n_kv, tq), lambda i: (0, i))
    prev_t = pl.BlockSpec((n_kv, BLOCK), lambda i: (0, jnp.maximum(i * nblk - 1, 0)))
    smem = pl.BlockSpec(memory_space=pltpu.SMEM)
    return pl.pallas_call(
        _swa_kernel,
        grid=(s // tq,),
        in_specs=[
            smem, smem, _const_spec((2 * BLOCK, BLOCK)),
            pl.BlockSpec((tq, SWA_HEADS * SWA_HD), lambda i: (i, 0)),
            own, prev, own, prev, own_t, prev_t,
        ],
        out_specs=pl.BlockSpec((tq, SWA_HEADS * SWA_HD), lambda i: (i, 0)),
        out_shape=jax.ShapeDtypeStruct((s, SWA_HEADS * SWA_HD), BF16),
        scratch_shapes=[pltpu.VMEM((2 * SWA_KV_HEADS, 2 * BLOCK, 2 * BLOCK), F32)],
        compiler_params=pltpu.CompilerParams(
            dimension_semantics=("arbitrary",), vmem_limit_bytes=VMEM_LIMIT),
        name="swa_attn",
    )(rel_bias.reshape(-1), sinks, bkt, qs, ks, ks, ksw, ksw, vst, vst)


def _memkv_kernel(mem_ref, g_ref, w_ref, o_ref):
    mn = _rms(mem_ref[...], g_ref[...]).astype(BF16)
    o_ref[...] = _dot(mn, w_ref[...]).astype(BF16)


def _memkv_call(mem, g, w):
    n = 2 * MEM_HEADS * MEM_HD
    return pl.pallas_call(
        _memkv_kernel,
        out_shape=jax.ShapeDtypeStruct((MEM_LEN, n), BF16),
        compiler_params=pltpu.CompilerParams(vmem_limit_bytes=VMEM_LIMIT),
        name="mem_kv",
    )(mem, g, w)


def _mem_kernel(q_ref, kv_ref, o_ref):
    nk = MEM_HEADS * MEM_HD
    for hh in range(MEM_HEADS):
        lo = hh * MEM_HD
        s = _dot_nt(q_ref[:, lo:lo + MEM_HD], kv_ref[:, lo:lo + MEM_HD])
        m = jnp.max(s, axis=-1, keepdims=True)
        p = jnp.exp(s - m)
        den = jnp.sum(p, axis=-1, keepdims=True)
        o = _dot(p.astype(BF16), kv_ref[:, nk + lo:nk + lo + MEM_HD]) / den
        o_ref[:, lo:lo + MEM_HD] = o.astype(BF16)


def _mem_call(qm, kvm):
    s = qm.shape[0]
    tq = MEM_TQ
    n = MEM_HEADS * MEM_HD
    return pl.pallas_call(
        _mem_kernel,
        grid=(s // tq,),
        in_specs=[pl.BlockSpec((tq, n), lambda i: (i, 0)), _const_spec((MEM_LEN, 2 * n))],
        out_specs=pl.BlockSpec((tq, n), lambda i: (i, 0)),
        out_shape=jax.ShapeDtypeStruct((s, n), BF16),
        compiler_params=pltpu.CompilerParams(
            dimension_semantics=("arbitrary",), vmem_limit_bytes=VMEM_LIMIT),
        name="mem_attn",
    )(qm, kvm)


def _post_kernel(x_ref, oa_ref, ob_ref, oc_ref, gate_ref, wa_ref, wb_ref, wc_ref, wout_ref,
                 gn_ref, wup_ref, wdn_ref, fn_ref, o_ref, *, final):
    d = D_MODEL
    y = (gate_ref[:, 0:d].astype(F32) * _dot(oa_ref[...], wa_ref[...])
         + gate_ref[:, d:2 * d].astype(F32) * _dot(ob_ref[...], wb_ref[...])
         + gate_ref[:, 2 * d:3 * d].astype(F32) * _dot(oc_ref[...], wc_ref[...]))
    x1 = x_ref[...] + _dot(y.astype(BF16), wout_ref[...])
    h = _rms(x1, gn_ref[...]).astype(BF16)
    acc = x1
    for c in range(D_FF // FF_CHUNK):
        lo = c * FF_CHUNK
        u = jnp.maximum(_dot(h, wup_ref[:, lo:lo + FF_CHUNK]), 0.0)
        acc = acc + _dot((u * u).astype(BF16), wdn_ref[lo:lo + FF_CHUNK, :])
    if final:
        acc = _rms(acc, fn_ref[...])
    o_ref[...] = acc


def _post_call(x, oa, ob, oc, gates, wa, wb, wc, wout, gn, wup, wdn, fn, final):
    s = x.shape[0]
    tm = POST_TM
    d = D_MODEL

    def row_spec(n):
        return pl.BlockSpec((tm, n), lambda i: (i, 0))

    return pl.pallas_call(
        functools.partial(_post_kernel, final=final),
        grid=(s // tm,),
        in_specs=[
            row_spec(d), row_spec(oa.shape[1]), row_spec(ob.shape[1]), row_spec(oc.shape[1]),
            row_spec(N_BRANCH * d),
            _const_spec(wa.shape), _const_spec(wb.shape), _const_spec(wc.shape), _const_spec((d, d)),
            _const_spec((1, d)), _const_spec((d, D_FF)), _const_spec((D_FF, d)), _const_spec((1, d)),
        ],
        out_specs=row_spec(d),
        out_shape=jax.ShapeDtypeStruct((s, d), F32),
        compiler_params=pltpu.CompilerParams(
            dimension_semantics=("arbitrary",), vmem_limit_bytes=VMEM_LIMIT),
        name="post",
    )(x, oa, ob, oc, gates, wa, wb, wc, wout, gn, wup, wdn, fn)


def _rot_cols(w):
    half = MLA_ROPE // 2
    return jnp.concatenate([-w[..., half:], w[..., :half]], axis=-1)


def _pad_cols(w, left, total):
    return jnp.pad(w, [(0, 0)] * (w.ndim - 1) + [(left, total - left - w.shape[-1])])


def _layer_weights(w_in, w_uq, w_ukv):
    sizes = (MLA_Q_LORA, MLA_KV_LORA, MLA_ROPE, SWA_HEADS * SWA_HD, SWA_KV_HEADS * SWA_HD,
             SWA_KV_HEADS * SWA_HD, MEM_HEADS * MEM_HD, N_BRANCH * D_MODEL)
    offs = [0]
    for n in sizes:
        offs.append(offs[-1] + n)
    cq, ckv, kpe, qs, ks, vs, qm, gt = (w_in[:, offs[t]:offs[t + 1]] for t in range(len(sizes)))
    ks_swapped = jnp.concatenate([ks[:, SWA_HD:], ks[:, :SWA_HD]], axis=1)
    w1 = jnp.concatenate(
        [cq, ckv, _pad_cols(kpe, MLA_NOPE, HEAD_PAD), _pad_cols(_rot_cols(kpe), MLA_NOPE, HEAD_PAD),
         qs, ks, ks_swapped, qm, gt], axis=1).astype(BF16)
    wvst = vs.T.astype(BF16)

    uq = w_uq.reshape(MLA_Q_LORA, MLA_HEADS, MLA_NOPE + MLA_ROPE)
    uq_plain = _pad_cols(uq, 0, HEAD_PAD)
    uq_rot = _pad_cols(_rot_cols(uq[..., MLA_NOPE:]), MLA_NOPE, HEAD_PAD)
    wuq = jnp.concatenate([uq_plain.reshape(MLA_Q_LORA, -1), uq_rot.reshape(MLA_Q_LORA, -1)],
                          axis=1).astype(BF16)

    ukv = w_ukv.reshape(MLA_KV_LORA, MLA_HEADS, MLA_NOPE + MLA_V)
    wk = _pad_cols(ukv[..., :MLA_NOPE], 0, HEAD_PAD).reshape(MLA_KV_LORA, -1).astype(BF16)
    wvt = ukv[..., MLA_NOPE:].reshape(MLA_KV_LORA, -1).T.astype(BF16)
    return w1, wuq, wk, wvt, wvst


def _rope_table(seq):
    pos = jnp.arange(seq, dtype=F32)
    inv = 1.0 / (ROPE_THETA ** (jnp.arange(0, MLA_ROPE, 2, dtype=F32) / MLA_ROPE))
    ang = pos[:, None] * inv[None, :]
    cos = jnp.concatenate([jnp.cos(ang)] * 2, axis=-1)
    sin = jnp.concatenate([jnp.sin(ang)] * 2, axis=-1)
    cos_pad = _pad_cols(cos, MLA_NOPE, HEAD_PAD)
    sin_pad = _pad_cols(sin, MLA_NOPE, HEAD_PAD)
    nope = _pad_cols(jnp.ones((seq, MLA_NOPE), F32), 0, HEAD_PAD)
    scale = (MLA_NOPE + MLA_ROPE) ** -0.5 * math.log2(math.e)
    return jnp.concatenate([scale * (nope + cos_pad), scale * sin_pad, cos_pad, sin_pad], axis=-1)


def _t5_bucket(dist):
    n = jnp.maximum(dist, 0)
    max_exact = REL_BUCKETS // 2
    nf = jnp.maximum(n, 1).astype(F32)
    large = max_exact + (jnp.log(nf / max_exact) / math.log(REL_MAX_DIST / max_exact)
                         * (REL_BUCKETS - max_exact)).astype(jnp.int32)
    large = jnp.minimum(large, REL_BUCKETS - 1)
    return jnp.where(n < max_exact, n, large)


def _swa_bucket_table():
    kj = jnp.arange(2 * BLOCK)[:, None]
    qi = jnp.arange(BLOCK)[None, :]
    dist = qi + BLOCK - kj
    band = (dist >= 0) & (dist < WINDOW)
    return jnp.where(band, _t5_bucket(dist), -1).astype(jnp.int32)


def kernel(x, mem, rel_bias, attn_norm, mem_norm, w_in, b_gate, mla_q_norm, w_uq, mla_kv_norm,
           w_ukv, attn_sinks, w_mem_kv, w_o_mla, w_o_swa, w_o_mem, w_out, mlp_norm, w_up, w_down,
           final_norm):
    batch, seq, d = x.shape
    assert batch == 1 and d == D_MODEL and mem.shape == (1, MEM_LEN, D_MODEL)
    depth = w_in.shape[0]
    xs = x[0]
    mem2 = mem[0]
    tab = _rope_table(seq)
    bkt = _swa_bucket_table()
    fn = final_norm.reshape(1, d)

    for l in range(depth):
        w1, wuq, wk, wvt, wvst = _layer_weights(w_in[l], w_uq[l], w_ukv[l])
        q, k, vt, qs, ks, ksw, vst, qm, gates = _proj_call(
            xs, attn_norm[l].reshape(1, d), w1, mla_q_norm[l].reshape(1, -1), wuq,
            mla_kv_norm[l].reshape(1, -1), wk, wvt, wvst, b_gate[l].reshape(1, -1), tab)
        o_mla = _mla_call(q, k, vt)
        o_mla = o_mla.reshape(MLA_HEADS * MLA_V, seq).T
        o_swa = _swa_call(rel_bias.astype(F32), attn_sinks[l], bkt, qs, ks, ksw, vst)
        kvm = _memkv_call(mem2, mem_norm[l].reshape(1, d), w_mem_kv[l].astype(BF16))
        o_mem = _mem_call(qm, kvm)
        xs = _post_call(
            xs, o_mla, o_swa, o_mem, gates,
            w_o_mla[l].astype(BF16), w_o_swa[l].astype(BF16), w_o_mem[l].astype(BF16),
            w_out[l].astype(BF16), mlp_norm[l].reshape(1, d), w_up[l].astype(BF16),
            w_down[l].astype(BF16), fn, final=(l == depth - 1))
    return xs[None]
```

```python
import functools
import math

import jax
import jax.numpy as jnp
from jax import lax
from jax.experimental import pallas as pl
from jax.experimental.pallas import tpu as pltpu

F32 = jnp.float32
BF16 = jnp.bfloat16

D_MODEL = 1024
MLA_HEADS = 8
MLA_Q_LORA = 256
MLA_KV_LORA = 128
MLA_NOPE = 64
MLA_ROPE = 32
MLA_V = 64
ROPE_THETA = 10000.0
SWA_HEADS = 8
SWA_KV_HEADS = 2
SWA_HD = 64
WINDOW = 128
BLOCK = 128
REL_BUCKETS = 32
REL_MAX_DIST = 128
MEM_LEN = 256
MEM_HEADS = 4
MEM_HD = 128
N_BRANCH = 3
D_FF = 4 * D_MODEL
EPS = 1e-6

HEAD_PAD = 128
BF16_SUBLANES = 16
MLA_ACC_ROWS = MLA_V + BF16_SUBLANES
NEG_BIG = -1e30
LOG2E = math.log2(math.e)
LANES = 128

PROJ_TM = 512
MLA_TQ = 1024
MLA_TK = 512
MLA_TC = 256
MLA_HEADS_PER_STEP = 2
MLA_LOOKAHEAD = 3
SWA_TQ = 512
SWA_LOOKAHEAD = 5
MEM_TQ = 512
POST_TM = 512
FF_CHUNK = 1024
VMEM_LIMIT = 56 * 1024 * 1024

_C_CQ = 0
_C_CKV = _C_CQ + MLA_Q_LORA
_C_KPE = _C_CKV + MLA_KV_LORA
_C_KROT = _C_KPE + HEAD_PAD
_C_KS = _C_KROT + HEAD_PAD
_C_KSW = _C_KS + SWA_KV_HEADS * SWA_HD
_C_QS = _C_KSW + SWA_KV_HEADS * SWA_HD
_C_QM = _C_QS + SWA_HEADS * SWA_HD
_C_GATE = _C_QM + MEM_HEADS * MEM_HD
_C_END = _C_GATE + N_BRANCH * D_MODEL


def _rms(x, g):
    return x * lax.rsqrt(jnp.mean(x * x, axis=-1, keepdims=True) + EPS) * g


def _const_spec(shape):
    zeros = (0,) * len(shape)
    return pl.BlockSpec(shape, lambda *_: zeros, pipeline_mode=pl.Buffered(1))


def _dot(a, b):
    return jnp.dot(a, b, preferred_element_type=F32)


def _dot_nt(a, b):
    return lax.dot_general(a, b, (((1,), (1,)), ((), ())), preferred_element_type=F32)


def _proj_kernel(x_ref, g_ref, w1_ref, qn_ref, wuq_ref, kvn_ref, wk_ref, wvt_ref, wvst_ref, bg_ref,
                 tab_ref, q_ref, k_ref, vt_ref, qs_ref, ks_ref, ksw_ref, vst_ref, qm_ref, gate_ref):
    h = _rms(x_ref[...], g_ref[...]).astype(BF16)

    def mm(lo, hi):
        return _dot(h, w1_ref[:, lo:hi])

    k_cos, k_sin = tab_ref[:, 0:HEAD_PAD], tab_ref[:, HEAD_PAD:2 * HEAD_PAD]
    q_scale = (MLA_NOPE + MLA_ROPE) ** -0.5 * LOG2E
    nope = (lax.broadcasted_iota(jnp.int32, k_cos.shape, 1) < MLA_NOPE).astype(F32)
    q_cos, q_sin = (k_cos + nope) * q_scale, k_sin * q_scale
    hw = MLA_HEADS * HEAD_PAD

    mla_in = mm(_C_CQ, _C_KROT)
    swa_in = mm(_C_KROT, _C_QM)

    def part(block, base, lo, hi):
        return block[:, lo - base:hi - base]

    cq = _rms(part(mla_in, _C_CQ, _C_CQ, _C_CKV), qn_ref[...]).astype(BF16)
    qq = _dot(cq, wuq_ref[...])
    for hh in range(MLA_HEADS):
        lo = hh * HEAD_PAD
        q_ref[hh] = (qq[:, lo:lo + HEAD_PAD] * q_cos
                     + qq[:, hw + lo:hw + lo + HEAD_PAD] * q_sin).astype(BF16)

    ckv = _rms(part(mla_in, _C_CQ, _C_CKV, _C_KPE), kvn_ref[...]).astype(BF16)
    kn = _dot(ckv, wk_ref[...])
    kpe = (part(mla_in, _C_CQ, _C_KPE, _C_KROT) * k_cos
           + part(swa_in, _C_KROT, _C_KROT, _C_KS) * k_sin)
    for hh in range(MLA_HEADS):
        lo = hh * HEAD_PAD
        k_ref[hh] = (kn[:, lo:lo + HEAD_PAD] + kpe).astype(BF16)
    vt = _dot_nt(wvt_ref[...], ckv)
    for hh in range(MLA_HEADS):
        vt_ref[hh] = vt[hh * MLA_V:(hh + 1) * MLA_V, :].astype(BF16)

    qs_ref[...] = (part(swa_in, _C_KROT, _C_QS, _C_QM) * (SWA_HD ** -0.5 * LOG2E)).astype(BF16)
    ks_ref[...] = part(swa_in, _C_KROT, _C_KS, _C_KSW).astype(BF16)
    ksw_ref[...] = part(swa_in, _C_KROT, _C_KSW, _C_QS).astype(BF16)
    vst_ref[...] = _dot_nt(wvst_ref[...], h).astype(BF16)
    qm_ref[...] = (mm(_C_QM, _C_GATE) * (MEM_HD ** -0.5)).astype(BF16)
    for c in range(N_BRANCH):
        lo = _C_GATE + c * D_MODEL
        z = mm(lo, lo + D_MODEL) + bg_ref[:, c * D_MODEL:(c + 1) * D_MODEL]
        gate_ref[:, c * D_MODEL:(c + 1) * D_MODEL] = (1.0 / (1.0 + jnp.exp(-z))).astype(BF16)


def _proj_call(x, g, w1, qn, wuq, kvn, wk, wvt, wvst, bg, tab):
    s = x.shape[0]
    tm = PROJ_TM
    hw = MLA_HEADS * HEAD_PAD
    head_spec = pl.BlockSpec((MLA_HEADS, tm, HEAD_PAD), lambda i: (0, i, 0))
    head_shape = jax.ShapeDtypeStruct((MLA_HEADS, s, HEAD_PAD), BF16)
    vt_spec = pl.BlockSpec((MLA_HEADS, MLA_V, tm), lambda i: (0, 0, i))
    vt_shape = jax.ShapeDtypeStruct((MLA_HEADS, MLA_V, s), BF16)

    def row_spec(n):
        return pl.BlockSpec((tm, n), lambda i: (i, 0))

    def row_shape(n):
        return jax.ShapeDtypeStruct((s, n), BF16)

    n_qs, n_kv, n_qm, n_g = SWA_HEADS * SWA_HD, SWA_KV_HEADS * SWA_HD, MEM_HEADS * MEM_HD, N_BRANCH * D_MODEL
    return pl.pallas_call(
        _proj_kernel,
        grid=(s // tm,),
        in_specs=[
            row_spec(D_MODEL),
            _const_spec((1, D_MODEL)),
            _const_spec((D_MODEL, _C_END)),
            _const_spec((1, MLA_Q_LORA)),
            _const_spec((MLA_Q_LORA, 2 * hw)),
            _const_spec((1, MLA_KV_LORA)),
            _const_spec((MLA_KV_LORA, hw)),
            _const_spec((MLA_HEADS * MLA_V, MLA_KV_LORA)),
            _const_spec((n_kv, D_MODEL)),
            _const_spec((1, n_g)),
            row_spec(2 * HEAD_PAD),
        ],
        out_specs=[head_spec, head_spec, vt_spec,
                   row_spec(n_qs), row_spec(n_kv), row_spec(n_kv),
                   pl.BlockSpec((n_kv, tm), lambda i: (0, i)), row_spec(n_qm), row_spec(n_g)],
        out_shape=[head_shape, head_shape, vt_shape,
                   row_shape(n_qs), row_shape(n_kv), row_shape(n_kv),
                   jax.ShapeDtypeStruct((n_kv, s), BF16), row_shape(n_qm), row_shape(n_g)],
        compiler_params=pltpu.CompilerParams(
            dimension_semantics=("arbitrary",), vmem_limit_bytes=VMEM_LIMIT),
        name="proj",
    )(x, g, w1, qn, wuq, kvn, wk, wvt, wvst, bg, tab)


def _mla_kernel(q_ref, k_ref, vt_ref, o_ref, sa_scr, sb_scr, acc_scr):
    i = pl.program_id(1)
    tq, tk, tc, hp = MLA_TQ, MLA_TK, MLA_TC, MLA_HEADS_PER_STEP
    nc = tq // tc
    assert tq == 2 * tk and tk == 2 * tc
    assert hp * MLA_V == LANES
    key = lax.broadcasted_iota(jnp.int32, (tk, tc), 0)
    lane = lax.broadcasted_iota(jnp.int32, (tk, tc), 1)
    ones_rows = (lax.broadcasted_iota(jnp.int32, (MLA_ACC_ROWS - MLA_V, tk), 0) == 0).astype(BF16)

    units = [(h, c) for c in range(nc) for h in range(hp)]

    def scores(j, buf, u, key_off=None, or_valid=None):
        h, c = u
        start = pl.multiple_of(j * tk, tk)
        s = _dot_nt(k_ref[h, pl.ds(start, tk), :], q_ref[h, c * tc:(c + 1) * tc, :])
        if key_off is not None:
            valid = key + key_off <= lane + c * tc
            if or_valid is not None:
                valid = jnp.logical_or(valid, or_valid)
            s = jnp.where(valid, s, NEG_BIG)
        buf[h, :, c * tc:(c + 1) * tc] = s
        return jnp.max(s, axis=0, keepdims=True)

    def accumulate(j, buf, u, m, cm):
        h, c = u
        start = pl.multiple_of(j * tk, tk)
        m_new = jnp.maximum(m, cm)
        alpha = jnp.exp2(m - m_new)
        p = jnp.exp2(buf[h, :, c * tc:(c + 1) * tc] - m_new).astype(BF16)
        v1 = jnp.concatenate([vt_ref[h, :, pl.ds(start, tk)], ones_rows], axis=0)
        acc_scr[h, :, c * tc:(c + 1) * tc] = (
            alpha * acc_scr[h, :, c * tc:(c + 1) * tc] + _dot(v1, p))
        return m_new

    assert MLA_LOOKAHEAD <= len(units)

    def trip(t, carry, next_on_diagonal):
        m, cm_a = dict(zip(units, carry[0])), dict(zip(units, carry[1]))
        cm_b, cm_next = {}, {}
        todo = [("b", u) for u in units] + [("next", u) for u in units]

        def issue():
            which, u = todo.pop(0)
            if which == "b":
                cm_b[u] = scores(2 * t + 1, sb_scr, u)
            else:
                masked = next_on_diagonal and u[1] * tc < tk
                cm_next[u] = scores(2 * t + 2, sa_scr, u, key_off=0 if masked else None)

        for _ in range(MLA_LOOKAHEAD):
            issue()
        for which, u in [("a", u) for u in units] + [("b", u) for u in units]:
            if which == "a":
                m[u] = accumulate(2 * t, sa_scr, u, m[u], cm_a[u])
            else:
                m[u] = accumulate(2 * t + 1, sb_scr, u, m[u], cm_b[u])
            if todo:
                issue()
        return tuple(m[u] for u in units), tuple(cm_next[u] for u in units)

    acc_scr[...] = jnp.zeros_like(acc_scr)
    cm_a = tuple(scores(0, sa_scr, u, key_off=0, or_valid=i > 0) for u in units)
    carry = (tuple(jnp.full((1, tc), NEG_BIG, F32) for _ in units), cm_a)
    n_plain = jnp.maximum(i - 1, 0)
    carry = lax.fori_loop(0, n_plain, lambda t, c: trip(t, c, False), carry)
    carry = lax.fori_loop(n_plain, i, lambda t, c: trip(t, c, True), carry)
    m, cm_a = dict(zip(units, carry[0])), dict(zip(units, carry[1]))
    late = [u for u in units if (u[1] + 1) * tc > tk]
    cm_b = {u: scores(2 * i + 1, sb_scr, u, key_off=tk) for u in late}
    for u in units:
        m[u] = accumulate(2 * i, sa_scr, u, m[u], cm_a[u])
    for u in late:
        m[u] = accumulate(2 * i + 1, sb_scr, u, m[u], cm_b[u])
    o_t = jnp.concatenate([acc_scr[h, 0:MLA_V, :] / acc_scr[h, MLA_V:MLA_V + 1, :] for h in range(hp)],
                          axis=0)
    for blk in range(tq // LANES):
        o_ref[blk * LANES:(blk + 1) * LANES, :] = o_t[:, blk * LANES:(blk + 1) * LANES].T.astype(BF16)


def _mla_call(q, k, vt):
    s = q.shape[1]
    tq, hp = MLA_TQ, MLA_HEADS_PER_STEP
    return pl.pallas_call(
        _mla_kernel,
        grid=(MLA_HEADS // hp, s // tq),
        in_specs=[
            pl.BlockSpec((hp, tq, HEAD_PAD), lambda h, i: (h, i, 0)),
            pl.BlockSpec((hp, s, HEAD_PAD), lambda h, i: (h, 0, 0)),
            pl.BlockSpec((hp, MLA_V, s), lambda h, i: (h, 0, 0)),
        ],
        out_specs=pl.BlockSpec((tq, hp * MLA_V), lambda h, i: (i, h)),
        out_shape=jax.ShapeDtypeStruct((s, MLA_HEADS * MLA_V), BF16),
        scratch_shapes=[pltpu.VMEM((hp, MLA_TK, tq), F32), pltpu.VMEM((hp, MLA_TK, tq), F32),
                        pltpu.VMEM((hp, MLA_ACC_ROWS, tq), F32)],
        compiler_params=pltpu.CompilerParams(
            dimension_semantics=("arbitrary", "arbitrary"), vmem_limit_bytes=VMEM_LIMIT),
        name="mla_attn",
    )(q, k, vt)


def _swa_kernel(relb_ref, sink_ref, bkt_ref, q_ref, k_ref, kp_ref, ksw_ref, kswp_ref, vt_ref, vtp_ref,
                o_ref, bias_scr):
    i = pl.program_id(0)
    tq, hd, rep = SWA_TQ, SWA_HD, SWA_HEADS // SWA_KV_HEADS
    win = 2 * BLOCK
    assert 2 * hd == LANES and rep == 4 and BLOCK == LANES

    @pl.when(i == 0)
    def _build_bias():
        bkt = bkt_ref[...]
        for hh in range(SWA_HEADS):
            t = jnp.full((win, BLOCK), NEG_BIG, F32)
            for b in range(REL_BUCKETS):
                t = jnp.where(bkt == b, relb_ref[b * SWA_HEADS + hh] * LOG2E, t)
            g, r = divmod(hh, rep)
            bias_scr[2 * g + r % 2, :, (r // 2) * BLOCK:(r // 2 + 1) * BLOCK] = t

    lo_half = lax.broadcasted_iota(jnp.int32, (BLOCK + tq, LANES), 1) < hd
    kfull = jnp.concatenate([kp_ref[...], k_ref[...]], axis=0)
    kswfull = jnp.concatenate([kswp_ref[...], ksw_ref[...]], axis=0)
    zero = jnp.zeros_like(kfull)
    k_lo = (jnp.where(lo_half, kfull, zero), jnp.where(lo_half, kswfull, zero))
    k_hi = (jnp.where(lo_half, zero, kswfull), jnp.where(lo_half, zero, kfull))
    vfull = jnp.concatenate([vtp_ref[...], vt_ref[...]], axis=1)
    ones_rows = (lax.broadcasted_iota(jnp.int32, (BF16_SUBLANES, win), 0) == 0).astype(BF16)
    key_row = lax.broadcasted_iota(jnp.int32, (win, 2 * BLOCK), 0)
    first_valid = jnp.logical_or(key_row >= BLOCK, i > 0)

    sinks = [jnp.concatenate(
        [jnp.full((1, BLOCK), sink_ref[g * rep + half] * LOG2E, F32),
         jnp.full((1, BLOCK), sink_ref[g * rep + 2 + half] * LOG2E, F32)], axis=1)
        for g in range(SWA_KV_HEADS) for half in range(2)]

    tiles = [(b, g, half) for b in range(tq // BLOCK) for g in range(SWA_KV_HEADS) for half in range(2)]

    def tile_scores(b, g, half):
        r0 = b * BLOCK
        qg = jnp.concatenate([q_ref[r0:r0 + BLOCK, (2 * g) * LANES:(2 * g + 1) * LANES],
                              q_ref[r0:r0 + BLOCK, (2 * g + 1) * LANES:(2 * g + 2) * LANES]],
                             axis=0)
        kmat = (k_lo, k_hi)[half][g]
        s = _dot_nt(kmat[r0:r0 + win], qg) + bias_scr[2 * g + half]
        return jnp.where(first_valid, s, NEG_BIG) if b == 0 else s

    def tile_output(b, g, half, s):
        r0 = b * BLOCK
        sink = sinks[2 * g + half]
        v1 = jnp.concatenate([vfull[g * hd:(g + 1) * hd, r0:r0 + win], ones_rows], axis=0)
        m = jnp.maximum(jnp.max(s, axis=0, keepdims=True), sink)
        p = jnp.exp2(s - m).astype(BF16)
        ot = _dot(v1, p)
        den = ot[hd:hd + 1] + jnp.exp2(sink - m)
        return ot[0:hd] / den

    pending = [tile_scores(*tile) for tile in tiles[:SWA_LOOKAHEAD]]
    outs = {}
    for t, (b, g, half) in enumerate(tiles):
        s_cur = pending.pop(0)
        if t + SWA_LOOKAHEAD < len(tiles):
            pending.append(tile_scores(*tiles[t + SWA_LOOKAHEAD]))
        outs[half] = tile_output(b, g, half, s_cur)
        if half == 1:
            r0 = b * BLOCK
            for pr in range(2):
                x = jnp.concatenate([outs[0][:, pr * BLOCK:(pr + 1) * BLOCK],
                                     outs[1][:, pr * BLOCK:(pr + 1) * BLOCK]], axis=0)
                o_ref[r0:r0 + BLOCK, (2 * g + pr) * LANES:(2 * g + pr + 1) * LANES] = x.T.astype(BF16)


def _swa_call(rel_bias, sinks, bkt, qs, ks, ksw, vst):
    s = qs.shape[0]
    tq = SWA_TQ
    nblk = tq // BLOCK
    n_kv = SWA_KV_HEADS * SWA_HD
    own = pl.BlockSpec((tq, n_kv), lambda i: (i, 0))
    prev = pl.BlockSpec((BLOCK, n_kv), lambda i: (jnp.maximum(i * nblk - 1, 0), 0))
    own_t = pl.BlockSpec((n_kv, tq), lambda i: (0, i))
    prev_t = pl.BlockSpec((n_kv, BLOCK), lambda i: (0, jnp.maximum(i * nblk - 1, 0)))
    smem = pl.BlockSpec(memory_space=pltpu.SMEM)
    return pl.pallas_call(
        _swa_kernel,
        grid=(s // tq,),
        in_specs=[
            smem, smem, _const_spec((2 * BLOCK, BLOCK)),
            pl.BlockSpec((tq, SWA_HEADS * SWA_HD), lambda i: (i, 0)),
            own, prev, own, prev, own_t, prev_t,
        ],
        out_specs=pl.BlockSpec((tq, SWA_HEADS * SWA_HD), lambda i: (i, 0)),
        out_shape=jax.ShapeDtypeStruct((s, SWA_HEADS * SWA_HD), BF16),
        scratch_shapes=[pltpu.VMEM((2 * SWA_KV_HEADS, 2 * BLOCK, 2 * BLOCK), F32)],
        compiler_params=pltpu.CompilerParams(
            dimension_semantics=("arbitrary",), vmem_limit_bytes=VMEM_LIMIT),
        name="swa_attn",
    )(rel_bias.reshape(-1), sinks, bkt, qs, ks, ks, ksw, ksw, vst, vst)


def _memkv_kernel(mem_ref, g_ref, w_ref, o_ref):
    mn = _rms(mem_ref[...], g_ref[...]).astype(BF16)
    o_ref[...] = _dot(mn, w_ref[...]).astype(BF16)


def _memkv_call(mem, g, w):
    n = 2 * MEM_HEADS * MEM_HD
    return pl.pallas_call(
        _memkv_kernel,
        out_shape=jax.ShapeDtypeStruct((MEM_LEN, n), BF16),
        compiler_params=pltpu.CompilerParams(vmem_limit_bytes=VMEM_LIMIT),
        name="mem_kv",
    )(mem, g, w)


def _mem_kernel(q_ref, kv_ref, o_ref):
    nk = MEM_HEADS * MEM_HD
    for hh in range(MEM_HEADS):
        lo = hh * MEM_HD
        s = _dot_nt(q_ref[:, lo:lo + MEM_HD], kv_ref[:, lo:lo + MEM_HD])
        m = jnp.max(s, axis=-1, keepdims=True)
        p = jnp.exp(s - m)
        den = jnp.sum(p, axis=-1, keepdims=True)
        o = _dot(p.astype(BF16), kv_ref[:, nk + lo:nk + lo + MEM_HD]) / den
        o_ref[:, lo:lo + MEM_HD] = o.astype(BF16)


def _mem_call(qm, kvm):
    s = qm.shape[0]
    tq = MEM_TQ
    n = MEM_HEADS * MEM_HD
    return pl.pallas_call(
        _mem_kernel,
        grid=(s // tq,),
        in_specs=[pl.BlockSpec((tq, n), lambda i: (i, 0)), _const_spec((MEM_LEN, 2 * n))],
        out_specs=pl.BlockSpec((tq, n), lambda i: (i, 0)),
        out_shape=jax.ShapeDtypeStruct((s, n), BF16),
        compiler_params=pltpu.CompilerParams(
            dimension_semantics=("arbitrary",), vmem_limit_bytes=VMEM_LIMIT),
        name="mem_attn",
    )(qm, kvm)


def _post_kernel(x_ref, oa_ref, ob_ref, oc_ref, gate_ref, wa_ref, wb_ref, wc_ref, wout_ref,
                 gn_ref, wup_ref, wdn_ref, fn_ref, o_ref, *, final):
    d = D_MODEL
    y = (gate_ref[:, 0:d].astype(F32) * _dot(oa_ref[...], wa_ref[...])
         + gate_ref[:, d:2 * d].astype(F32) * _dot(ob_ref[...], wb_ref[...])
         + gate_ref[:, 2 * d:3 * d].astype(F32) * _dot(oc_ref[...], wc_ref[...]))
    x1 = x_ref[...] + _dot(y.astype(BF16), wout_ref[...])
    h = _rms(x1, gn_ref[...]).astype(BF16)
    acc = x1
    for c in range(D_FF // FF_CHUNK):
        lo = c * FF_CHUNK
        u = jnp.maximum(_dot(h, wup_ref[:, lo:lo + FF_CHUNK]), 0.0)
        acc = acc + _dot((u * u).astype(BF16), wdn_ref[lo:lo + FF_CHUNK, :])
    if final:
        acc = _rms(acc, fn_ref[...])
    o_ref[...] = acc


def _post_call(x, oa, ob, oc, gates, wa, wb, wc, wout, gn, wup, wdn, fn, final):
    s = x.shape[0]
    tm = POST_TM
    d = D_MODEL

    def row_spec(n):
        return pl.BlockSpec((tm, n), lambda i: (i, 0))

    return pl.pallas_call(
        functools.partial(_post_kernel, final=final),
        grid=(s // tm,),
        in_specs=[
            row_spec(d), row_spec(oa.shape[1]), row_spec(ob.shape[1]), row_spec(oc.shape[1]),
            row_spec(N_BRANCH * d),
            _const_spec(wa.shape), _const_spec(wb.shape), _const_spec(wc.shape), _const_spec((d, d)),
            _const_spec((1, d)), _const_spec((d, D_FF)), _const_spec((D_FF, d)), _const_spec((1, d)),
        ],
        out_specs=row_spec(d),
        out_shape=jax.ShapeDtypeStruct((s, d), F32),
        compiler_params=pltpu.CompilerParams(
            dimension_semantics=("arbitrary",), vmem_limit_bytes=VMEM_LIMIT),
        name="post",
    )(x, oa, ob, oc, gates, wa, wb, wc, wout, gn, wup, wdn, fn)


def _rot_cols(w):
    half = MLA_ROPE // 2
    return jnp.concatenate([-w[..., half:], w[..., :half]], axis=-1)


def _pad_cols(w, left, total):
    return jnp.pad(w, [(0, 0)] * (w.ndim - 1) + [(left, total - left - w.shape[-1])])


def _layer_weights(w_in, w_uq, w_ukv):
    sizes = (MLA_Q_LORA, MLA_KV_LORA, MLA_ROPE, SWA_HEADS * SWA_HD, SWA_KV_HEADS * SWA_HD,
             SWA_KV_HEADS * SWA_HD, MEM_HEADS * MEM_HD, N_BRANCH * D_MODEL)
    offs = [0]
    for n in sizes:
        offs.append(offs[-1] + n)
    cq, ckv, kpe, qs, ks, vs, qm, gt = (w_in[:, offs[t]:offs[t + 1]] for t in range(len(sizes)))
    ks_swapped = jnp.concatenate([ks[:, SWA_HD:], ks[:, :SWA_HD]], axis=1)
    w1 = jnp.concatenate(
        [cq, ckv, _pad_cols(kpe, MLA_NOPE, HEAD_PAD), _pad_cols(_rot_cols(kpe), MLA_NOPE, HEAD_PAD),
         ks, ks_swapped, qs, qm, gt], axis=1).astype(BF16)
    wvst = vs.T.astype(BF16)

    uq = w_uq.reshape(MLA_Q_LORA, MLA_HEADS, MLA_NOPE + MLA_ROPE)
    uq_plain = _pad_cols(uq, 0, HEAD_PAD)
    uq_rot = _pad_cols(_rot_cols(uq[..., MLA_NOPE:]), MLA_NOPE, HEAD_PAD)
    wuq = jnp.concatenate([uq_plain.reshape(MLA_Q_LORA, -1), uq_rot.reshape(MLA_Q_LORA, -1)],
                          axis=1).astype(BF16)

    ukv = w_ukv.reshape(MLA_KV_LORA, MLA_HEADS, MLA_NOPE + MLA_V)
    wk = _pad_cols(ukv[..., :MLA_NOPE], 0, HEAD_PAD).reshape(MLA_KV_LORA, -1).astype(BF16)
    wvt = ukv[..., MLA_NOPE:].reshape(MLA_KV_LORA, -1).T.astype(BF16)
    return w1, wuq, wk, wvt, wvst


def _rope_table(seq):
    pos = jnp.arange(seq, dtype=F32)
    inv = 1.0 / (ROPE_THETA ** (jnp.arange(0, MLA_ROPE, 2, dtype=F32) / MLA_ROPE))
    ang = pos[:, None] * inv[None, :]
    cos = jnp.concatenate([jnp.cos(ang)] * 2, axis=-1)
    sin = jnp.concatenate([jnp.sin(ang)] * 2, axis=-1)
    return jnp.concatenate([_pad_cols(cos, MLA_NOPE, HEAD_PAD), _pad_cols(sin, MLA_NOPE, HEAD_PAD)],
                           axis=-1)


def _t5_bucket(dist):
    n = jnp.maximum(dist, 0)
    max_exact = REL_BUCKETS // 2
    nf = jnp.maximum(n, 1).astype(F32)
    large = max_exact + (jnp.log(nf / max_exact) / math.log(REL_MAX_DIST / max_exact)
                         * (REL_BUCKETS - max_exact)).astype(jnp.int32)
    large = jnp.minimum(large, REL_BUCKETS - 1)
    return jnp.where(n < max_exact, n, large)


def _swa_bucket_table():
    kj = jnp.arange(2 * BLOCK)[:, None]
    qi = jnp.arange(BLOCK)[None, :]
    dist = qi + BLOCK - kj
    band = (dist >= 0) & (dist < WINDOW)
    return jnp.where(band, _t5_bucket(dist), -1).astype(jnp.int32)


def kernel(x, mem, rel_bias, attn_norm, mem_norm, w_in, b_gate, mla_q_norm, w_uq, mla_kv_norm,
           w_ukv, attn_sinks, w_mem_kv, w_o_mla, w_o_swa, w_o_mem, w_out, mlp_norm, w_up, w_down,
           final_norm):
    batch, seq, d = x.shape
    assert batch == 1 and d == D_MODEL and mem.shape == (1, MEM_LEN, D_MODEL)
    depth = w_in.shape[0]
    xs = x[0]
    mem2 = mem[0]
    tab = _rope_table(seq)
    bkt = _swa_bucket_table()
    fn = final_norm.reshape(1, d)

    for l in range(depth):
        w1, wuq, wk, wvt, wvst = _layer_weights(w_in[l], w_uq[l], w_ukv[l])
        q, k, vt, qs, ks, ksw, vst, qm, gates = _proj_call(
            xs, attn_norm[l].reshape(1, d), w1, mla_q_norm[l].reshape(1, -1), wuq,
            mla_kv_norm[l].reshape(1, -1), wk, wvt, wvst, b_gate[l].reshape(1, -1), tab)
        o_mla = _mla_call(q, k, vt)
        o_swa = _swa_call(rel_bias.astype(F32), attn_sinks[l], bkt, qs, ks, ksw, vst)
        kvm = _memkv_call(mem2, mem_norm[l].reshape(1, d), w_mem_kv[l].astype(BF16))
        o_mem = _mem_call(qm, kvm)
        xs = _post_call(
            xs, o_mla, o_swa, o_mem, gates,
            w_o_mla[l].astype(BF16), w_o_swa[l].astype(BF16), w_o_mem[l].astype(BF16),
            w_out[l].astype(BF16), mlp_norm[l].reshape(1, d), w_up[l].astype(BF16),
            w_down[l].astype(BF16), fn, final=(l == depth - 1))
    return xs[None]
```

```python
import functools
import math

import jax
import jax.numpy as jnp
from jax import lax
from jax.experimental import pallas as pl
from jax.experimental.pallas import tpu as pltpu

F32 = jnp.float32
BF16 = jnp.bfloat16

D_MODEL = 1024
MLA_HEADS = 8
MLA_Q_LORA = 256
MLA_KV_LORA = 128
MLA_NOPE = 64
MLA_ROPE = 32
MLA_V = 64
ROPE_THETA = 10000.0
SWA_HEADS = 8
SWA_KV_HEADS = 2
SWA_HD = 64
WINDOW = 128
BLOCK = 128
REL_BUCKETS = 32
REL_MAX_DIST = 128
MEM_LEN = 256
MEM_HEADS = 4
MEM_HD = 128
N_BRANCH = 3
D_FF = 4 * D_MODEL
EPS = 1e-6

HEAD_PAD = 128
BF16_SUBLANES = 16
MLA_ACC_ROWS = MLA_V + BF16_SUBLANES
NEG_BIG = -1e30
LOG2E = math.log2(math.e)
LANES = 128

PROJ_TM = 512
MLA_TQ = 1024
MLA_TK = 512
MLA_TC = 256
MLA_HEADS_PER_STEP = 2
MLA_PAIRS_PER_TRIP = 2
MLA_LOOKAHEAD = 3
SWA_TQ = 512
SWA_LOOKAHEAD = 5
MEM_TQ = 512
POST_TM = 512
FF_CHUNK = 1024
VMEM_LIMIT = 56 * 1024 * 1024

_H_CQ = 0
_H_CKV = _H_CQ + MLA_Q_LORA
_H_KPE = _H_CKV + MLA_KV_LORA
_H_KROT = _H_KPE + HEAD_PAD
_H_KSW = _H_KROT + HEAD_PAD
_H_END = _H_KSW + SWA_KV_HEADS * SWA_HD
_IN_REST = MLA_Q_LORA + MLA_KV_LORA + MLA_ROPE
_R_QS = 0
_R_KS = _R_QS + SWA_HEADS * SWA_HD
_R_VS = _R_KS + SWA_KV_HEADS * SWA_HD
_R_QM = _R_VS + SWA_KV_HEADS * SWA_HD
_R_GATE = _R_QM + MEM_HEADS * MEM_HD
_R_END = _R_GATE + N_BRANCH * D_MODEL


def _rms(x, g):
    return x * lax.rsqrt(jnp.mean(x * x, axis=-1, keepdims=True) + EPS) * g


def _const_spec(shape):
    zeros = (0,) * len(shape)
    return pl.BlockSpec(shape, lambda *_: zeros, pipeline_mode=pl.Buffered(1))


def _layer_spec(shape, layer):
    index = (layer,) + (0,) * len(shape)
    return pl.BlockSpec((None,) + tuple(shape), lambda *_: index, pipeline_mode=pl.Buffered(1))


def _dot(a, b):
    return jnp.dot(a, b, preferred_element_type=F32)


def _dot_nt(a, b):
    return lax.dot_general(a, b, (((1,), (1,)), ((), ())), preferred_element_type=F32)


def _proj_kernel(x_ref, g_ref, wh_ref, wr_ref, qn_ref, wuq_ref, kvn_ref, wk_ref, wv_ref, bg_ref,
                 tab_ref, q_ref, k_ref, vt_ref, qs_ref, ks_ref, ksw_ref, vst_ref, qm_ref, gate_ref):
    h = _rms(x_ref[...], g_ref[...]).astype(BF16)

    k_cos, k_sin = tab_ref[:, 0:HEAD_PAD], tab_ref[:, HEAD_PAD:2 * HEAD_PAD]
    q_scale = (MLA_NOPE + MLA_ROPE) ** -0.5 * LOG2E
    nope = (lax.broadcasted_iota(jnp.int32, k_cos.shape, 1) < MLA_NOPE).astype(F32)
    q_cos, q_sin = (k_cos + nope) * q_scale, k_sin * q_scale
    hw = MLA_HEADS * HEAD_PAD

    mla_in = _dot(h, wh_ref[:, _H_CQ:_H_KROT])
    aux_in = _dot(h, wh_ref[:, _H_KROT:_H_END])
    swa_in = _dot(h, wr_ref[:, _R_QS:_R_GATE])

    cq = _rms(mla_in[:, _H_CQ:_H_CKV], qn_ref[...]).astype(BF16)
    qq = _dot(cq, wuq_ref[...])
    for hh in range(MLA_HEADS):
        lo = hh * HEAD_PAD
        q_ref[hh] = (qq[:, lo:lo + HEAD_PAD] * q_cos
                     + qq[:, hw + lo:hw + lo + HEAD_PAD] * q_sin).astype(BF16)

    ckv = _rms(mla_in[:, _H_CKV:_H_KPE], kvn_ref[...]).astype(BF16)
    kn = _dot(ckv, wk_ref[...])
    kpe = mla_in[:, _H_KPE:_H_KROT] * k_cos + aux_in[:, 0:HEAD_PAD] * k_sin
    for hh in range(MLA_HEADS):
        lo = hh * HEAD_PAD
        k_ref[hh] = (kn[:, lo:lo + HEAD_PAD] + kpe).astype(BF16)
    vt = _dot(ckv, wv_ref[...]).T
    for hh in range(MLA_HEADS):
        vt_ref[hh] = vt[hh * MLA_V:(hh + 1) * MLA_V, :].astype(BF16)

    qs_ref[...] = (swa_in[:, _R_QS:_R_KS] * (SWA_HD ** -0.5 * LOG2E)).astype(BF16)
    ks_ref[...] = swa_in[:, _R_KS:_R_VS].astype(BF16)
    ksw_ref[...] = aux_in[:, HEAD_PAD:].astype(BF16)
    vst_ref[...] = swa_in[:, _R_VS:_R_QM].T.astype(BF16)
    qm_ref[...] = (swa_in[:, _R_QM:_R_GATE] * (MEM_HD ** -0.5 * LOG2E)).astype(BF16)
    for c in range(N_BRANCH):
        lo = _R_GATE + c * D_MODEL
        z = _dot(h, wr_ref[:, lo:lo + D_MODEL]) + bg_ref[:, c * D_MODEL:(c + 1) * D_MODEL]
        gate_ref[:, c * D_MODEL:(c + 1) * D_MODEL] = (1.0 / (1.0 + jnp.exp(-z))).astype(BF16)


def _proj_call(layer, x, g, wh, wr, qn, wuq, kvn, wk, wv, bg, tab):
    s = x.shape[0]
    tm = PROJ_TM
    hw = MLA_HEADS * HEAD_PAD
    head_spec = pl.BlockSpec((MLA_HEADS, tm, HEAD_PAD), lambda i: (0, i, 0))
    head_shape = jax.ShapeDtypeStruct((MLA_HEADS, s, HEAD_PAD), BF16)
    vt_spec = pl.BlockSpec((MLA_HEADS, MLA_V, tm), lambda i: (0, 0, i))
    vt_shape = jax.ShapeDtypeStruct((MLA_HEADS, MLA_V, s), BF16)

    def row_spec(n):
        return pl.BlockSpec((tm, n), lambda i: (i, 0))

    def row_shape(n):
        return jax.ShapeDtypeStruct((s, n), BF16)

    n_qs, n_kv, n_qm, n_g = SWA_HEADS * SWA_HD, SWA_KV_HEADS * SWA_HD, MEM_HEADS * MEM_HD, N_BRANCH * D_MODEL
    return pl.pallas_call(
        _proj_kernel,
        grid=(s // tm,),
        in_specs=[
            row_spec(D_MODEL),
            _layer_spec((1, D_MODEL), layer),
            _layer_spec((D_MODEL, _H_END), layer),
            _layer_spec((D_MODEL, _R_END), layer),
            _layer_spec((1, MLA_Q_LORA), layer),
            _layer_spec((MLA_Q_LORA, 2 * hw), layer),
            _layer_spec((1, MLA_KV_LORA), layer),
            _layer_spec((MLA_KV_LORA, hw), layer),
            _layer_spec((MLA_KV_LORA, MLA_HEADS * MLA_V), layer),
            _layer_spec((1, n_g), layer),
            row_spec(2 * HEAD_PAD),
        ],
        out_specs=[head_spec, head_spec, vt_spec,
                   row_spec(n_qs), row_spec(n_kv), row_spec(n_kv),
                   pl.BlockSpec((n_kv, tm), lambda i: (0, i)), row_spec(n_qm), row_spec(n_g)],
        out_shape=[head_shape, head_shape, vt_shape,
                   row_shape(n_qs), row_shape(n_kv), row_shape(n_kv),
                   jax.ShapeDtypeStruct((n_kv, s), BF16), row_shape(n_qm), row_shape(n_g)],
        compiler_params=pltpu.CompilerParams(
            dimension_semantics=("arbitrary",), vmem_limit_bytes=VMEM_LIMIT),
        name="proj",
    )(x, g, wh, wr, qn, wuq, kvn, wk, wv, bg, tab)


def _mla_kernel(q_ref, k_ref, vt_ref, o_ref, sa_scr, sb_scr, acc_scr):
    i = pl.program_id(1)
    tq, tk, tc, hp = MLA_TQ, MLA_TK, MLA_TC, MLA_HEADS_PER_STEP
    nc = tq // tc
    assert tq == 2 * tk and tk == 2 * tc
    assert hp * MLA_V == LANES
    key = lax.broadcasted_iota(jnp.int32, (tk, tc), 0)
    lane = lax.broadcasted_iota(jnp.int32, (tk, tc), 1)
    ones_rows = (lax.broadcasted_iota(jnp.int32, (MLA_ACC_ROWS - MLA_V, tk), 0) == 0).astype(BF16)

    units = [(h, c) for c in range(nc) for h in range(hp)]
    slot = {u: n for n, u in enumerate(units)}

    def scores(j, buf, u, key_off=None, or_valid=None):
        h, c = u
        start = pl.multiple_of(j * tk, tk)
        s = _dot_nt(k_ref[h, pl.ds(start, tk), :], q_ref[h, c * tc:(c + 1) * tc, :])
        if key_off is not None:
            valid = key + key_off <= lane + c * tc
            if or_valid is not None:
                valid = jnp.logical_or(valid, or_valid)
            s = jnp.where(valid, s, NEG_BIG)
        buf[slot[u]] = s
        return jnp.max(s, axis=0, keepdims=True)

    def accumulate(j, buf, u, m, cm):
        h, c = u
        start = pl.multiple_of(j * tk, tk)
        m_new = jnp.maximum(m, cm)
        alpha = jnp.exp2(m - m_new)
        p = jnp.exp2(buf[slot[u]] - m_new).astype(BF16)
        v1 = jnp.concatenate([vt_ref[h, :, pl.ds(start, tk)], ones_rows], axis=0)
        acc_scr[slot[u]] = alpha * acc_scr[slot[u]] + _dot(v1, p)
        return m_new

    assert MLA_LOOKAHEAD <= len(units)

    def trip(j0, carry, npairs, last_on_diagonal):
        m, cm_first = dict(zip(units, carry[0])), dict(zip(units, carry[1]))
        nchunks = 2 * npairs
        cm = {(0, u): cm_first[u] for u in units}
        todo = [(r, u) for r in range(1, nchunks + 1) for u in units]

        def issue():
            r, u = todo.pop(0)
            masked = last_on_diagonal and r == nchunks and u[1] * tc < tk
            cm[(r, u)] = scores(j0 + r, sb_scr if r % 2 else sa_scr, u, key_off=0 if masked else None)

        for _ in range(MLA_LOOKAHEAD):
            issue()
        for r in range(nchunks):
            for u in units:
                m[u] = accumulate(j0 + r, sb_scr if r % 2 else sa_scr, u, m[u], cm[(r, u)])
                if todo:
                    issue()
        return tuple(m[u] for u in units), tuple(cm[(nchunks, u)] for u in units)

    acc_scr[...] = jnp.zeros_like(acc_scr)
    cm_a = tuple(scores(0, sa_scr, u, key_off=0, or_valid=i > 0) for u in units)
    carry = (tuple(jnp.full((1, tc), NEG_BIG, F32) for _ in units), cm_a)
    n_plain = jnp.maximum(i - 1, 0)
    n_long = n_plain // MLA_PAIRS_PER_TRIP
    carry = lax.fori_loop(
        0, n_long, lambda t, c: trip(2 * MLA_PAIRS_PER_TRIP * t, c, MLA_PAIRS_PER_TRIP, False), carry)
    carry = lax.fori_loop(
        n_long * MLA_PAIRS_PER_TRIP, n_plain, lambda t, c: trip(2 * t, c, 1, False), carry)
    carry = lax.fori_loop(n_plain, i, lambda t, c: trip(2 * t, c, 1, True), carry)
    m, cm_a = dict(zip(units, carry[0])), dict(zip(units, carry[1]))
    late = [u for u in units if (u[1] + 1) * tc > tk]
    cm_b = {u: scores(2 * i + 1, sb_scr, u, key_off=tk) for u in late}
    for u in units:
        m[u] = accumulate(2 * i, sa_scr, u, m[u], cm_a[u])
    for u in late:
        m[u] = accumulate(2 * i + 1, sb_scr, u, m[u], cm_b[u])
    o_t = jnp.concatenate(
        [jnp.concatenate([acc_scr[slot[(h, c)], 0:MLA_V, :] / acc_scr[slot[(h, c)], MLA_V:MLA_V + 1, :]
                          for c in range(nc)], axis=1) for h in range(hp)], axis=0)
    for blk in range(tq // LANES):
        o_ref[blk * LANES:(blk + 1) * LANES, :] = o_t[:, blk * LANES:(blk + 1) * LANES].T.astype(BF16)


def _mla_call(q, k, vt):
    s = q.shape[1]
    tq, hp = MLA_TQ, MLA_HEADS_PER_STEP
    n_units = hp * (tq // MLA_TC)
    return pl.pallas_call(
        _mla_kernel,
        grid=(MLA_HEADS // hp, s // tq),
        in_specs=[
            pl.BlockSpec((hp, tq, HEAD_PAD), lambda h, i: (h, i, 0)),
            pl.BlockSpec((hp, s, HEAD_PAD), lambda h, i: (h, 0, 0)),
            pl.BlockSpec((hp, MLA_V, s), lambda h, i: (h, 0, 0)),
        ],
        out_specs=pl.BlockSpec((tq, hp * MLA_V), lambda h, i: (i, h)),
        out_shape=jax.ShapeDtypeStruct((s, MLA_HEADS * MLA_V), BF16),
        scratch_shapes=[pltpu.VMEM((n_units, MLA_TK, MLA_TC), F32), pltpu.VMEM((n_units, MLA_TK, MLA_TC), F32),
                        pltpu.VMEM((n_units, MLA_ACC_ROWS, MLA_TC), F32)],
        compiler_params=pltpu.CompilerParams(
            dimension_semantics=("arbitrary", "arbitrary"), vmem_limit_bytes=VMEM_LIMIT),
        name="mla_attn",
    )(q, k, vt)


def _swa_kernel(relb_ref, sink_ref, bkt_ref, q_ref, k_ref, kp_ref, ksw_ref, kswp_ref, vt_ref, vtp_ref,
                o_ref, bias_scr):
    i = pl.program_id(0)
    tq, hd, rep = SWA_TQ, SWA_HD, SWA_HEADS // SWA_KV_HEADS
    win = 2 * BLOCK
    assert 2 * hd == LANES and rep == 4 and BLOCK == LANES

    @pl.when(i == 0)
    def _build_bias():
        bkt = bkt_ref[...]
        for hh in range(SWA_HEADS):
            t = jnp.full((win, BLOCK), NEG_BIG, F32)
            for b in range(REL_BUCKETS):
                t = jnp.where(bkt == b, relb_ref[b * SWA_HEADS + hh] * LOG2E, t)
            g, r = divmod(hh, rep)
            bias_scr[2 * g + r % 2, :, (r // 2) * BLOCK:(r // 2 + 1) * BLOCK] = t

    lo_half = lax.broadcasted_iota(jnp.int32, (BLOCK + tq, LANES), 1) < hd
    kfull = jnp.concatenate([kp_ref[...], k_ref[...]], axis=0)
    kswfull = jnp.concatenate([kswp_ref[...], ksw_ref[...]], axis=0)
    zero = jnp.zeros_like(kfull)
    k_lo = (jnp.where(lo_half, kfull, zero), jnp.where(lo_half, kswfull, zero))
    k_hi = (jnp.where(lo_half, zero, kswfull), jnp.where(lo_half, zero, kfull))
    vfull = jnp.concatenate([vtp_ref[...], vt_ref[...]], axis=1)
    ones_rows = (lax.broadcasted_iota(jnp.int32, (BF16_SUBLANES, win), 0) == 0).astype(BF16)
    key_row = lax.broadcasted_iota(jnp.int32, (win, 2 * BLOCK), 0)
    first_valid = jnp.logical_or(key_row >= BLOCK, i > 0)

    sinks = [jnp.concatenate(
        [jnp.full((1, BLOCK), sink_ref[g * rep + half] * LOG2E, F32),
         jnp.full((1, BLOCK), sink_ref[g * rep + 2 + half] * LOG2E, F32)], axis=1)
        for g in range(SWA_KV_HEADS) for half in range(2)]

    tiles = [(b, g, half) for b in range(tq // BLOCK) for g in range(SWA_KV_HEADS) for half in range(2)]

    def tile_scores(b, g, half):
        r0 = b * BLOCK
        qg = jnp.concatenate([q_ref[r0:r0 + BLOCK, (2 * g) * LANES:(2 * g + 1) * LANES],
                              q_ref[r0:r0 + BLOCK, (2 * g + 1) * LANES:(2 * g + 2) * LANES]],
                             axis=0)
        kmat = (k_lo, k_hi)[half][g]
        s = _dot_nt(kmat[r0:r0 + win], qg) + bias_scr[2 * g + half]
        return jnp.where(first_valid, s, NEG_BIG) if b == 0 else s

    def tile_output(b, g, half, s):
        r0 = b * BLOCK
        sink = sinks[2 * g + half]
        v1 = jnp.concatenate([vfull[g * hd:(g + 1) * hd, r0:r0 + win], ones_rows], axis=0)
        m = jnp.maximum(jnp.max(s, axis=0, keepdims=True), sink)
        p = jnp.exp2(s - m).astype(BF16)
        ot = _dot(v1, p)
        den = ot[hd:hd + 1] + jnp.exp2(sink - m)
        return ot[0:hd] / den

    pending = [tile_scores(*tile) for tile in tiles[:SWA_LOOKAHEAD]]
    outs = {}
    for t, (b, g, half) in enumerate(tiles):
        s_cur = pending.pop(0)
        if t + SWA_LOOKAHEAD < len(tiles):
            pending.append(tile_scores(*tiles[t + SWA_LOOKAHEAD]))
        outs[half] = tile_output(b, g, half, s_cur)
        if half == 1:
            r0 = b * BLOCK
            for pr in range(2):
                x = jnp.concatenate([outs[0][:, pr * BLOCK:(pr + 1) * BLOCK],
                                     outs[1][:, pr * BLOCK:(pr + 1) * BLOCK]], axis=0)
                o_ref[r0:r0 + BLOCK, (2 * g + pr) * LANES:(2 * g + pr + 1) * LANES] = x.T.astype(BF16)


def _swa_call(rel_bias, sinks, bkt, qs, ks, ksw, vst):
    s = qs.shape[0]
    tq = SWA_TQ
    nblk = tq // BLOCK
    n_kv = SWA_KV_HEADS * SWA_HD
    own = pl.BlockSpec((tq, n_kv), lambda i: (i, 0))
    prev = pl.BlockSpec((BLOCK, n_kv), lambda i: (jnp.maximum(i * nblk - 1, 0), 0))
    own_t = pl.BlockSpec((n_kv, tq), lambda i: (0, i))
    prev_t = pl.BlockSpec((n_kv, BLOCK), lambda i: (0, jnp.maximum(i * nblk - 1, 0)))
    smem = pl.BlockSpec(memory_space=pltpu.SMEM)
    return pl.pallas_call(
        _swa_kernel,
        grid=(s // tq,),
        in_specs=[
            smem, smem, _const_spec((2 * BLOCK, BLOCK)),
            pl.BlockSpec((tq, SWA_HEADS * SWA_HD), lambda i: (i, 0)),
            own, prev, own, prev, own_t, prev_t,
        ],
        out_specs=pl.BlockSpec((tq, SWA_HEADS * SWA_HD), lambda i: (i, 0)),
        out_shape=jax.ShapeDtypeStruct((s, SWA_HEADS * SWA_HD), BF16),
        scratch_shapes=[pltpu.VMEM((2 * SWA_KV_HEADS, 2 * BLOCK, 2 * BLOCK), F32)],
        compiler_params=pltpu.CompilerParams(
            dimension_semantics=("arbitrary",), vmem_limit_bytes=VMEM_LIMIT),
        name="swa_attn",
    )(rel_bias.reshape(-1), sinks, bkt, qs, ks, ks, ksw, ksw, vst, vst)


def _memkv_kernel(mem_ref, g_ref, w_ref, k_ref, vt_ref):
    n = MEM_HEADS * MEM_HD
    mn = _rms(mem_ref[...], g_ref[...]).astype(BF16)
    k_ref[...] = _dot(mn, w_ref[:, 0:n]).astype(BF16)
    vt_ref[...] = _dot(mn, w_ref[:, n:2 * n]).T.astype(BF16)


def _memkv_call(layer, mem, g, w):
    n = MEM_HEADS * MEM_HD
    return pl.pallas_call(
        _memkv_kernel,
        grid=(1,),
        in_specs=[_const_spec((MEM_LEN, D_MODEL)), _layer_spec((1, D_MODEL), layer),
                  _layer_spec((D_MODEL, 2 * n), layer)],
        out_specs=[_const_spec((MEM_LEN, n)), _const_spec((n, MEM_LEN))],
        out_shape=[jax.ShapeDtypeStruct((MEM_LEN, n), BF16), jax.ShapeDtypeStruct((n, MEM_LEN), BF16)],
        compiler_params=pltpu.CompilerParams(
            dimension_semantics=("arbitrary",), vmem_limit_bytes=VMEM_LIMIT),
        name="mem_kv",
    )(mem, g, w)


def _mem_kernel(q_ref, k_ref, vt_ref, o_ref):
    tq, dm = MEM_TQ, MEM_HD
    ones_rows = (lax.broadcasted_iota(jnp.int32, (BF16_SUBLANES, MEM_LEN), 0) == 0).astype(BF16)
    scores = [_dot_nt(k_ref[:, hh * dm:(hh + 1) * dm], q_ref[:, hh * dm:(hh + 1) * dm])
              for hh in range(MEM_HEADS)]
    for hh, s in enumerate(scores):
        p = jnp.exp2(s - jnp.max(s, axis=0, keepdims=True)).astype(BF16)
        v1 = jnp.concatenate([vt_ref[hh * dm:(hh + 1) * dm, :], ones_rows], axis=0)
        ot = _dot(v1, p)
        o = ot[0:dm] / ot[dm:dm + 1]
        for blk in range(tq // LANES):
            o_ref[blk * LANES:(blk + 1) * LANES, hh * dm:(hh + 1) * dm] = (
                o[:, blk * LANES:(blk + 1) * LANES].T.astype(BF16))


def _mem_call(qm, km, vmt):
    s = qm.shape[0]
    tq = MEM_TQ
    n = MEM_HEADS * MEM_HD
    assert MEM_HD == LANES
    return pl.pallas_call(
        _mem_kernel,
        grid=(s // tq,),
        in_specs=[pl.BlockSpec((tq, n), lambda i: (i, 0)), _const_spec((MEM_LEN, n)),
                  _const_spec((n, MEM_LEN))],
        out_specs=pl.BlockSpec((tq, n), lambda i: (i, 0)),
        out_shape=jax.ShapeDtypeStruct((s, n), BF16),
        compiler_params=pltpu.CompilerParams(
            dimension_semantics=("arbitrary",), vmem_limit_bytes=VMEM_LIMIT),
        name="mem_attn",
    )(qm, km, vmt)


def _post_kernel(x_ref, oa_ref, ob_ref, oc_ref, gate_ref, wa_ref, wb_ref, wc_ref, wout_ref,
                 gn_ref, wup_ref, wdn_ref, fn_ref, o_ref, *, final):
    d = D_MODEL
    y = (gate_ref[:, 0:d].astype(F32) * _dot(oa_ref[...], wa_ref[...])
         + gate_ref[:, d:2 * d].astype(F32) * _dot(ob_ref[...], wb_ref[...])
         + gate_ref[:, 2 * d:3 * d].astype(F32) * _dot(oc_ref[...], wc_ref[...]))
    x1 = x_ref[...] + _dot(y.astype(BF16), wout_ref[...])
    h = _rms(x1, gn_ref[...]).astype(BF16)
    acc = x1
    for c in range(D_FF // FF_CHUNK):
        lo = c * FF_CHUNK
        u = jnp.maximum(_dot(h, wup_ref[:, lo:lo + FF_CHUNK]), 0.0)
        acc = acc + _dot((u * u).astype(BF16), wdn_ref[lo:lo + FF_CHUNK, :])
    if final:
        acc = _rms(acc, fn_ref[...])
    o_ref[...] = acc


def _post_call(layer, x, oa, ob, oc, gates, wa, wb, wc, wout, gn, wup, wdn, fn, final):
    s = x.shape[0]
    tm = POST_TM
    d = D_MODEL

    def row_spec(n):
        return pl.BlockSpec((tm, n), lambda i: (i, 0))

    return pl.pallas_call(
        functools.partial(_post_kernel, final=final),
        grid=(s // tm,),
        in_specs=[
            row_spec(d), row_spec(oa.shape[1]), row_spec(ob.shape[1]), row_spec(oc.shape[1]),
            row_spec(N_BRANCH * d),
            _layer_spec(wa.shape[1:], layer), _layer_spec(wb.shape[1:], layer),
            _layer_spec(wc.shape[1:], layer), _layer_spec((d, d), layer),
            _layer_spec((1, d), layer), _layer_spec((d, D_FF), layer), _layer_spec((D_FF, d), layer),
            _const_spec((1, d)),
        ],
        out_specs=row_spec(d),
        out_shape=jax.ShapeDtypeStruct((s, d), F32),
        compiler_params=pltpu.CompilerParams(
            dimension_semantics=("arbitrary",), vmem_limit_bytes=VMEM_LIMIT),
        name="post",
    )(x, oa, ob, oc, gates, wa, wb, wc, wout, gn, wup, wdn, fn)


def _rot_cols(w):
    half = MLA_ROPE // 2
    return jnp.concatenate([-w[..., half:], w[..., :half]], axis=-1)


def _pad_cols(w, left, total):
    return jnp.pad(w, [(0, 0)] * (w.ndim - 1) + [(left, total - left - w.shape[-1])])


def _stacked_weights(w_in, w_uq, w_ukv):
    depth = w_in.shape[0]
    n_kv = SWA_KV_HEADS * SWA_HD
    kpe = w_in[:, :, _H_KPE:_IN_REST]
    ks0 = _IN_REST + _R_KS
    ks_swapped = jnp.concatenate([w_in[:, :, ks0 + SWA_HD:ks0 + n_kv], w_in[:, :, ks0:ks0 + SWA_HD]],
                                 axis=-1)
    w_head = jnp.concatenate(
        [w_in[:, :, :_H_KPE], _pad_cols(kpe, MLA_NOPE, HEAD_PAD),
         _pad_cols(_rot_cols(kpe), MLA_NOPE, HEAD_PAD), ks_swapped], axis=-1).astype(BF16)
    w_rest = w_in[:, :, _IN_REST:].astype(BF16)

    uq = w_uq.reshape(depth, MLA_Q_LORA, MLA_HEADS, MLA_NOPE + MLA_ROPE)
    uq_plain = _pad_cols(uq, 0, HEAD_PAD)
    uq_rot = _pad_cols(_rot_cols(uq[..., MLA_NOPE:]), MLA_NOPE, HEAD_PAD)
    wuq = jnp.concatenate([uq_plain.reshape(depth, MLA_Q_LORA, -1), uq_rot.reshape(depth, MLA_Q_LORA, -1)],
                          axis=-1).astype(BF16)

    ukv = w_ukv.reshape(depth, MLA_KV_LORA, MLA_HEADS, MLA_NOPE + MLA_V)
    wk = _pad_cols(ukv[..., :MLA_NOPE], 0, HEAD_PAD).reshape(depth, MLA_KV_LORA, -1).astype(BF16)
    wv = ukv[..., MLA_NOPE:].reshape(depth, MLA_KV_LORA, -1).astype(BF16)
    return w_head, w_rest, wuq, wk, wv


def _rope_table(seq):
    pos = jnp.arange(seq, dtype=F32)
    inv = 1.0 / (ROPE_THETA ** (jnp.arange(0, MLA_ROPE, 2, dtype=F32) / MLA_ROPE))
    ang = pos[:, None] * inv[None, :]
    cos = jnp.concatenate([jnp.cos(ang)] * 2, axis=-1)
    sin = jnp.concatenate([jnp.sin(ang)] * 2, axis=-1)
    return jnp.concatenate([_pad_cols(cos, MLA_NOPE, HEAD_PAD), _pad_cols(sin, MLA_NOPE, HEAD_PAD)],
                           axis=-1)


def _t5_bucket(dist):
    n = jnp.maximum(dist, 0)
    max_exact = REL_BUCKETS // 2
    nf = jnp.maximum(n, 1).astype(F32)
    large = max_exact + (jnp.log(nf / max_exact) / math.log(REL_MAX_DIST / max_exact)
                         * (REL_BUCKETS - max_exact)).astype(jnp.int32)
    large = jnp.minimum(large, REL_BUCKETS - 1)
    return jnp.where(n < max_exact, n, large)


def _swa_bucket_table():
    kj = jnp.arange(2 * BLOCK)[:, None]
    qi = jnp.arange(BLOCK)[None, :]
    dist = qi + BLOCK - kj
    band = (dist >= 0) & (dist < WINDOW)
    return jnp.where(band, _t5_bucket(dist), -1).astype(jnp.int32)


def kernel(x, mem, rel_bias, attn_norm, mem_norm, w_in, b_gate, mla_q_norm, w_uq, mla_kv_norm,
           w_ukv, attn_sinks, w_mem_kv, w_o_mla, w_o_swa, w_o_mem, w_out, mlp_norm, w_up, w_down,
           final_norm):
    batch, seq, d = x.shape
    assert batch == 1 and d == D_MODEL and mem.shape == (1, MEM_LEN, D_MODEL)
    depth = w_in.shape[0]
    xs = x[0]
    mem2 = mem[0]
    tab = _rope_table(seq)
    bkt = _swa_bucket_table()
    rel_bias = rel_bias.astype(F32)

    def rows(p):
        return p.reshape(depth, 1, -1)

    w_head, w_rest, wuq, wk, wv = _stacked_weights(w_in, w_uq, w_ukv)
    w_mem_kv, w_o_mla, w_o_swa, w_o_mem, w_out, w_up, w_down = (
        w.astype(BF16) for w in (w_mem_kv, w_o_mla, w_o_swa, w_o_mem, w_out, w_up, w_down))
    attn_norm, mem_norm, mla_q_norm, mla_kv_norm, mlp_norm, b_gate = (
        rows(p) for p in (attn_norm, mem_norm, mla_q_norm, mla_kv_norm, mlp_norm, b_gate))
    fn = final_norm.reshape(1, d)

    for l in range(depth):
        q, k, vt, qs, ks, ksw, vst, qm, gates = _proj_call(
            l, xs, attn_norm, w_head, w_rest, mla_q_norm, wuq, mla_kv_norm, wk, wv, b_gate, tab)
        o_mla = _mla_call(q, k, vt)
        o_swa = _swa_call(rel_bias, attn_sinks[l], bkt, qs, ks, ksw, vst)
        km, vmt = _memkv_call(l, mem2, mem_norm, w_mem_kv)
        o_mem = _mem_call(qm, km, vmt)
        xs = _post_call(l, xs, o_mla, o_swa, o_mem, gates, w_o_mla, w_o_swa, w_o_mem, w_out,
                        mlp_norm, w_up, w_down, fn, final=(l == depth - 1))
    return xs[None]
```

```python
import functools
import math

import jax
import jax.numpy as jnp
from jax import lax
from jax.experimental import pallas as pl
from jax.experimental.pallas import tpu as pltpu

F32 = jnp.float32
BF16 = jnp.bfloat16

D_MODEL = 1024
MLA_HEADS = 8
MLA_Q_LORA = 256
MLA_KV_LORA = 128
MLA_NOPE = 64
MLA_ROPE = 32
MLA_V = 64
ROPE_THETA = 10000.0
SWA_HEADS = 8
SWA_KV_HEADS = 2
SWA_HD = 64
WINDOW = 128
BLOCK = 128
REL_BUCKETS = 32
REL_MAX_DIST = 128
MEM_LEN = 256
MEM_HEADS = 4
MEM_HD = 128
N_BRANCH = 3
D_FF = 4 * D_MODEL
EPS = 1e-6

HEAD_PAD = 128
BF16_SUBLANES = 16
MLA_ACC_ROWS = MLA_V + BF16_SUBLANES
NEG_BIG = -1e30
LOG2E = math.log2(math.e)
LANES = 128

PROJ_TM = 512
MLA_TQ = 1024
MLA_TK = 512
MLA_TC = 256
MLA_HEADS_PER_STEP = 2
MLA_PAIRS_PER_TRIP = 2
MLA_LOOKAHEAD = 3
SWA_TQ = 512
SWA_LOOKAHEAD = 5
MEM_TQ = 512
POST_TM = 512
FF_CHUNK = 1024
VMEM_LIMIT = 56 * 1024 * 1024

_H_CQ = 0
_H_CKV = _H_CQ + MLA_Q_LORA
_H_KPE = _H_CKV + MLA_KV_LORA
_W_CQ = _H_CQ
_W_QS = _H_KPE + MLA_ROPE
_W_KS = _W_QS + SWA_HEADS * SWA_HD
_W_GATE = _W_KS + 2 * SWA_KV_HEADS * SWA_HD + MEM_HEADS * MEM_HD
_W_END = _W_GATE + N_BRANCH * D_MODEL
_MLA_IN_COLS = 512
_R_QS = 0
_R_KS = _R_QS + SWA_HEADS * SWA_HD
_R_VS = _R_KS + SWA_KV_HEADS * SWA_HD
_R_QM = _R_VS + SWA_KV_HEADS * SWA_HD
_R_GATE = _R_QM + MEM_HEADS * MEM_HD
assert _W_QS % BF16_SUBLANES == 0 and _W_GATE == _W_QS + _R_GATE and _H_KPE + LANES <= _MLA_IN_COLS


def _rms(x, g):
    return x * lax.rsqrt(jnp.mean(x * x, axis=-1, keepdims=True) + EPS) * g


def _const_spec(shape):
    zeros = (0,) * len(shape)
    return pl.BlockSpec(shape, lambda *_: zeros, pipeline_mode=pl.Buffered(1))


def _layer_spec(shape, layer):
    index = (layer,) + (0,) * len(shape)
    return pl.BlockSpec((None,) + tuple(shape), lambda *_: index, pipeline_mode=pl.Buffered(1))


def _dot(a, b):
    return jnp.dot(a, b, preferred_element_type=F32)


def _dot_nt(a, b):
    return lax.dot_general(a, b, (((1,), (1,)), ((), ())), preferred_element_type=F32)


def _proj_kernel(x_ref, g_ref, wt_ref, wa_ref, qn_ref, wuq_ref, kvn_ref, wk_ref, wv_ref, bg_ref,
                 tab_ref, q_ref, k_ref, vt_ref, qs_ref, ks_ref, ksw_ref, vst_ref, qm_ref, gate_ref):
    h = _rms(x_ref[...], g_ref[...]).astype(BF16)

    k_cos, k_sin = tab_ref[:, 0:HEAD_PAD], tab_ref[:, HEAD_PAD:2 * HEAD_PAD]
    q_scale = (MLA_NOPE + MLA_ROPE) ** -0.5 * LOG2E
    nope = (lax.broadcasted_iota(jnp.int32, k_cos.shape, 1) < MLA_NOPE).astype(F32)
    q_cos, q_sin = (k_cos + nope) * q_scale, k_sin * q_scale
    hw = MLA_HEADS * HEAD_PAD

    mla_in = _dot_nt(h, wt_ref[_W_CQ:_W_CQ + _MLA_IN_COLS, :])
    aux_in = _dot_nt(h, wa_ref[...])
    swa_in = _dot_nt(h, wt_ref[_W_QS:_W_GATE, :])

    cq = _rms(mla_in[:, _H_CQ:_H_CKV], qn_ref[...]).astype(BF16)
    qq = _dot(cq, wuq_ref[...])
    for hh in range(MLA_HEADS):
        lo = hh * HEAD_PAD
        q_ref[hh] = (qq[:, lo:lo + HEAD_PAD] * q_cos
                     + qq[:, hw + lo:hw + lo + HEAD_PAD] * q_sin).astype(BF16)

    ckv = _rms(mla_in[:, _H_CKV:_H_KPE], kvn_ref[...]).astype(BF16)
    kn = _dot(ckv, wk_ref[...])
    kpe = (pltpu.roll(mla_in[:, _H_KPE:_H_KPE + LANES], MLA_NOPE, axis=1) * k_cos
           + aux_in[:, 0:HEAD_PAD] * k_sin)
    for hh in range(MLA_HEADS):
        lo = hh * HEAD_PAD
        k_ref[hh] = (kn[:, lo:lo + HEAD_PAD] + kpe).astype(BF16)
    vt = _dot(ckv, wv_ref[...]).T
    for hh in range(MLA_HEADS):
        vt_ref[hh] = vt[hh * MLA_V:(hh + 1) * MLA_V, :].astype(BF16)

    qs_ref[...] = (swa_in[:, _R_QS:_R_KS] * (SWA_HD ** -0.5 * LOG2E)).astype(BF16)
    ks_ref[...] = swa_in[:, _R_KS:_R_VS].astype(BF16)
    ksw_ref[...] = aux_in[:, HEAD_PAD:].astype(BF16)
    vst_ref[...] = swa_in[:, _R_VS:_R_QM].T.astype(BF16)
    qm_ref[...] = (swa_in[:, _R_QM:_R_GATE] * (MEM_HD ** -0.5 * LOG2E)).astype(BF16)
    for c in range(N_BRANCH):
        lo = _W_GATE + c * D_MODEL
        half_z = _dot_nt(h, wt_ref[lo:lo + D_MODEL, :]) + bg_ref[:, c * D_MODEL:(c + 1) * D_MODEL]
        gate_ref[:, c * D_MODEL:(c + 1) * D_MODEL] = (0.5 * jnp.tanh(half_z) + 0.5).astype(BF16)


def _proj_call(layer, x, g, wt, wa, qn, wuq, kvn, wk, wv, bg, tab):
    s = x.shape[0]
    tm = PROJ_TM
    hw = MLA_HEADS * HEAD_PAD
    head_spec = pl.BlockSpec((MLA_HEADS, tm, HEAD_PAD), lambda i: (0, i, 0))
    head_shape = jax.ShapeDtypeStruct((MLA_HEADS, s, HEAD_PAD), BF16)
    vt_spec = pl.BlockSpec((MLA_HEADS, MLA_V, tm), lambda i: (0, 0, i))
    vt_shape = jax.ShapeDtypeStruct((MLA_HEADS, MLA_V, s), BF16)

    def row_spec(n):
        return pl.BlockSpec((tm, n), lambda i: (i, 0))

    def row_shape(n):
        return jax.ShapeDtypeStruct((s, n), BF16)

    n_qs, n_kv, n_qm, n_g = SWA_HEADS * SWA_HD, SWA_KV_HEADS * SWA_HD, MEM_HEADS * MEM_HD, N_BRANCH * D_MODEL
    return pl.pallas_call(
        _proj_kernel,
        grid=(s // tm,),
        in_specs=[
            row_spec(D_MODEL),
            _layer_spec((1, D_MODEL), layer),
            _layer_spec((_W_END, D_MODEL), layer),
            _layer_spec((HEAD_PAD + n_kv, D_MODEL), layer),
            _layer_spec((1, MLA_Q_LORA), layer),
            _layer_spec((MLA_Q_LORA, 2 * hw), layer),
            _layer_spec((1, MLA_KV_LORA), layer),
            _layer_spec((MLA_KV_LORA, hw), layer),
            _layer_spec((MLA_KV_LORA, MLA_HEADS * MLA_V), layer),
            _layer_spec((1, n_g), layer),
            row_spec(2 * HEAD_PAD),
        ],
        out_specs=[head_spec, head_spec, vt_spec,
                   row_spec(n_qs), row_spec(n_kv), row_spec(n_kv),
                   pl.BlockSpec((n_kv, tm), lambda i: (0, i)), row_spec(n_qm), row_spec(n_g)],
        out_shape=[head_shape, head_shape, vt_shape,
                   row_shape(n_qs), row_shape(n_kv), row_shape(n_kv),
                   jax.ShapeDtypeStruct((n_kv, s), BF16), row_shape(n_qm), row_shape(n_g)],
        compiler_params=pltpu.CompilerParams(
            dimension_semantics=("arbitrary",), vmem_limit_bytes=VMEM_LIMIT),
        name="proj",
    )(x, g, wt, wa, qn, wuq, kvn, wk, wv, bg, tab)


def _mla_kernel(q_ref, k_ref, vt_ref, o_ref, sa_scr, sb_scr, acc_scr):
    i = pl.program_id(1)
    tq, tk, tc, hp = MLA_TQ, MLA_TK, MLA_TC, MLA_HEADS_PER_STEP
    nc = tq // tc
    assert tq == 2 * tk and tk == 2 * tc
    assert hp * MLA_V == LANES
    key = lax.broadcasted_iota(jnp.int32, (tk, tc), 0)
    lane = lax.broadcasted_iota(jnp.int32, (tk, tc), 1)
    ones_rows = (lax.broadcasted_iota(jnp.int32, (MLA_ACC_ROWS - MLA_V, tk), 0) == 0).astype(BF16)

    units = [(h, c) for c in range(nc) for h in range(hp)]
    slot = {u: n for n, u in enumerate(units)}

    def scores(j, buf, u, key_off=None, or_valid=None):
        h, c = u
        start = pl.multiple_of(j * tk, tk)
        s = _dot_nt(k_ref[h, pl.ds(start, tk), :], q_ref[h, c * tc:(c + 1) * tc, :])
        if key_off is not None:
            valid = key + key_off <= lane + c * tc
            if or_valid is not None:
                valid = jnp.logical_or(valid, or_valid)
            s = jnp.where(valid, s, NEG_BIG)
        buf[slot[u]] = s
        return jnp.max(s, axis=0, keepdims=True)

    def accumulate(j, buf, u, m, cm):
        h, c = u
        start = pl.multiple_of(j * tk, tk)
        m_new = jnp.maximum(m, cm)
        alpha = jnp.exp2(m - m_new)
        p = jnp.exp2(buf[slot[u]] - m_new).astype(BF16)
        v1 = jnp.concatenate([vt_ref[h, :, pl.ds(start, tk)], ones_rows], axis=0)
        acc_scr[slot[u]] = alpha * acc_scr[slot[u]] + _dot(v1, p)
        return m_new

    assert MLA_LOOKAHEAD <= len(units)

    def trip(j0, carry, npairs, last_on_diagonal):
        m, cm_first = dict(zip(units, carry[0])), dict(zip(units, carry[1]))
        nchunks = 2 * npairs
        cm = {(0, u): cm_first[u] for u in units}
        todo = [(r, u) for r in range(1, nchunks + 1) for u in units]

        def issue():
            r, u = todo.pop(0)
            masked = last_on_diagonal and r == nchunks and u[1] * tc < tk
            cm[(r, u)] = scores(j0 + r, sb_scr if r % 2 else sa_scr, u, key_off=0 if masked else None)

        for _ in range(MLA_LOOKAHEAD):
            issue()
        for r in range(nchunks):
            for u in units:
                m[u] = accumulate(j0 + r, sb_scr if r % 2 else sa_scr, u, m[u], cm[(r, u)])
                if todo:
                    issue()
        return tuple(m[u] for u in units), tuple(cm[(nchunks, u)] for u in units)

    acc_scr[...] = jnp.zeros_like(acc_scr)
    cm_a = tuple(scores(0, sa_scr, u, key_off=0, or_valid=i > 0) for u in units)
    carry = (tuple(jnp.full((1, tc), NEG_BIG, F32) for _ in units), cm_a)
    n_plain = jnp.maximum(i - 1, 0)
    n_long = n_plain // MLA_PAIRS_PER_TRIP
    carry = lax.fori_loop(
        0, n_long, lambda t, c: trip(2 * MLA_PAIRS_PER_TRIP * t, c, MLA_PAIRS_PER_TRIP, False), carry)
    carry = lax.fori_loop(
        n_long * MLA_PAIRS_PER_TRIP, n_plain, lambda t, c: trip(2 * t, c, 1, False), carry)
    carry = lax.fori_loop(n_plain, i, lambda t, c: trip(2 * t, c, 1, True), carry)
    m, cm_a = dict(zip(units, carry[0])), dict(zip(units, carry[1]))
    late = [u for u in units if (u[1] + 1) * tc > tk]
    cm_b = {u: scores(2 * i + 1, sb_scr, u, key_off=tk) for u in late}
    for u in units:
        m[u] = accumulate(2 * i, sa_scr, u, m[u], cm_a[u])
    for u in late:
        m[u] = accumulate(2 * i + 1, sb_scr, u, m[u], cm_b[u])
    o_t = jnp.concatenate(
        [jnp.concatenate([acc_scr[slot[(h, c)], 0:MLA_V, :] / acc_scr[slot[(h, c)], MLA_V:MLA_V + 1, :]
                          for c in range(nc)], axis=1) for h in range(hp)], axis=0)
    for blk in range(tq // LANES):
        o_ref[blk * LANES:(blk + 1) * LANES, :] = o_t[:, blk * LANES:(blk + 1) * LANES].T.astype(BF16)


def _mla_call(q, k, vt):
    s = q.shape[1]
    tq, hp = MLA_TQ, MLA_HEADS_PER_STEP
    n_units = hp * (tq // MLA_TC)
    return pl.pallas_call(
        _mla_kernel,
        grid=(MLA_HEADS // hp, s // tq),
        in_specs=[
            pl.BlockSpec((hp, tq, HEAD_PAD), lambda h, i: (h, i, 0)),
            pl.BlockSpec((hp, s, HEAD_PAD), lambda h, i: (h, 0, 0)),
            pl.BlockSpec((hp, MLA_V, s), lambda h, i: (h, 0, 0)),
        ],
        out_specs=pl.BlockSpec((tq, hp * MLA_V), lambda h, i: (i, h)),
        out_shape=jax.ShapeDtypeStruct((s, MLA_HEADS * MLA_V), BF16),
        scratch_shapes=[pltpu.VMEM((n_units, MLA_TK, MLA_TC), F32), pltpu.VMEM((n_units, MLA_TK, MLA_TC), F32),
                        pltpu.VMEM((n_units, MLA_ACC_ROWS, MLA_TC), F32)],
        compiler_params=pltpu.CompilerParams(
            dimension_semantics=("arbitrary", "arbitrary"), vmem_limit_bytes=VMEM_LIMIT),
        name="mla_attn",
    )(q, k, vt)


def _swa_kernel(relb_ref, sink_ref, bkt_ref, q_ref, k_ref, kp_ref, ksw_ref, kswp_ref, vt_ref, vtp_ref,
                o_ref, bias_scr):
    i = pl.program_id(0)
    tq, hd, rep = SWA_TQ, SWA_HD, SWA_HEADS // SWA_KV_HEADS
    win = 2 * BLOCK
    assert 2 * hd == LANES and rep == 4 and BLOCK == LANES

    @pl.when(i == 0)
    def _build_bias():
        bkt = bkt_ref[...]
        for hh in range(SWA_HEADS):
            t = jnp.full((win, BLOCK), NEG_BIG, F32)
            for b in range(REL_BUCKETS):
                t = jnp.where(bkt == b, relb_ref[b * SWA_HEADS + hh] * LOG2E, t)
            g, r = divmod(hh, rep)
            bias_scr[2 * g + r % 2, :, (r // 2) * BLOCK:(r // 2 + 1) * BLOCK] = t

    lo_half = lax.broadcasted_iota(jnp.int32, (BLOCK + tq, LANES), 1) < hd
    kfull = jnp.concatenate([kp_ref[...], k_ref[...]], axis=0)
    kswfull = jnp.concatenate([kswp_ref[...], ksw_ref[...]], axis=0)
    zero = jnp.zeros_like(kfull)
    k_lo = (jnp.where(lo_half, kfull, zero), jnp.where(lo_half, kswfull, zero))
    k_hi = (jnp.where(lo_half, zero, kswfull), jnp.where(lo_half, zero, kfull))
    vfull = jnp.concatenate([vtp_ref[...], vt_ref[...]], axis=1)
    ones_rows = (lax.broadcasted_iota(jnp.int32, (BF16_SUBLANES, win), 0) == 0).astype(BF16)
    key_row = lax.broadcasted_iota(jnp.int32, (win, 2 * BLOCK), 0)
    first_valid = jnp.logical_or(key_row >= BLOCK, i > 0)

    sinks = [jnp.concatenate(
        [jnp.full((1, BLOCK), sink_ref[g * rep + half] * LOG2E, F32),
         jnp.full((1, BLOCK), sink_ref[g * rep + 2 + half] * LOG2E, F32)], axis=1)
        for g in range(SWA_KV_HEADS) for half in range(2)]

    tiles = [(b, g, half) for b in range(tq // BLOCK) for g in range(SWA_KV_HEADS) for half in range(2)]

    def tile_scores(b, g, half):
        r0 = b * BLOCK
        qg = jnp.concatenate([q_ref[r0:r0 + BLOCK, (2 * g) * LANES:(2 * g + 1) * LANES],
                              q_ref[r0:r0 + BLOCK, (2 * g + 1) * LANES:(2 * g + 2) * LANES]],
                             axis=0)
        kmat = (k_lo, k_hi)[half][g]
        s = _dot_nt(kmat[r0:r0 + win], qg) + bias_scr[2 * g + half]
        return jnp.where(first_valid, s, NEG_BIG) if b == 0 else s

    def tile_output(b, g, half, s):
        r0 = b * BLOCK
        sink = sinks[2 * g + half]
        v1 = jnp.concatenate([vfull[g * hd:(g + 1) * hd, r0:r0 + win], ones_rows], axis=0)
        m = jnp.maximum(jnp.max(s, axis=0, keepdims=True), sink)
        p = jnp.exp2(s - m).astype(BF16)
        ot = _dot(v1, p)
        den = ot[hd:hd + 1] + jnp.exp2(sink - m)
        return ot[0:hd] / den

    pending = [tile_scores(*tile) for tile in tiles[:SWA_LOOKAHEAD]]
    outs = {}
    for t, (b, g, half) in enumerate(tiles):
        s_cur = pending.pop(0)
        if t + SWA_LOOKAHEAD < len(tiles):
            pending.append(tile_scores(*tiles[t + SWA_LOOKAHEAD]))
        outs[half] = tile_output(b, g, half, s_cur)
        if half == 1:
            r0 = b * BLOCK
            for pr in range(2):
                x = jnp.concatenate([outs[0][:, pr * BLOCK:(pr + 1) * BLOCK],
                                     outs[1][:, pr * BLOCK:(pr + 1) * BLOCK]], axis=0)
                o_ref[r0:r0 + BLOCK, (2 * g + pr) * LANES:(2 * g + pr + 1) * LANES] = x.T.astype(BF16)


def _memkv_kernel(mem_ref, g_ref, w_ref, k_ref, vt_ref):
    n = MEM_HEADS * MEM_HD
    mn = _rms(mem_ref[...], g_ref[...]).astype(BF16)
    k_ref[...] = _dot(mn, w_ref[:, 0:n]).astype(BF16)
    vt_ref[...] = _dot(mn, w_ref[:, n:2 * n]).T.astype(BF16)


def _memkv_call(layer, mem, g, w):
    n = MEM_HEADS * MEM_HD
    return pl.pallas_call(
        _memkv_kernel,
        grid=(1,),
        in_specs=[_const_spec((MEM_LEN, D_MODEL)), _layer_spec((1, D_MODEL), layer),
                  _layer_spec((D_MODEL, 2 * n), layer)],
        out_specs=[_const_spec((MEM_LEN, n)), _const_spec((n, MEM_LEN))],
        out_shape=[jax.ShapeDtypeStruct((MEM_LEN, n), BF16), jax.ShapeDtypeStruct((n, MEM_LEN), BF16)],
        compiler_params=pltpu.CompilerParams(
            dimension_semantics=("arbitrary",), vmem_limit_bytes=VMEM_LIMIT),
        name="mem_kv",
    )(mem, g, w)


def _mem_kernel(q_ref, k_ref, vt_ref, o_ref):
    tq, dm = MEM_TQ, MEM_HD
    ones_rows = (lax.broadcasted_iota(jnp.int32, (BF16_SUBLANES, MEM_LEN), 0) == 0).astype(BF16)
    scores = [_dot_nt(k_ref[:, hh * dm:(hh + 1) * dm], q_ref[:, hh * dm:(hh + 1) * dm])
              for hh in range(MEM_HEADS)]
    for hh, s in enumerate(scores):
        p = jnp.exp2(s - jnp.max(s, axis=0, keepdims=True)).astype(BF16)
        v1 = jnp.concatenate([vt_ref[hh * dm:(hh + 1) * dm, :], ones_rows], axis=0)
        ot = _dot(v1, p)
        o = ot[0:dm] / ot[dm:dm + 1]
        for blk in range(tq // LANES):
            o_ref[blk * LANES:(blk + 1) * LANES, hh * dm:(hh + 1) * dm] = (
                o[:, blk * LANES:(blk + 1) * LANES].T.astype(BF16))


def _local_kernel(relb_ref, sink_ref, bkt_ref, q_ref, k_ref, kp_ref, ksw_ref, kswp_ref, vt_ref, vtp_ref,
                  qm_ref, km_ref, vmt_ref, o_ref, om_ref, bias_scr):
    _swa_kernel(relb_ref, sink_ref, bkt_ref, q_ref, k_ref, kp_ref, ksw_ref, kswp_ref, vt_ref, vtp_ref,
                o_ref, bias_scr)
    _mem_kernel(qm_ref, km_ref, vmt_ref, om_ref)


def _local_call(rel_bias, sinks, bkt, qs, ks, ksw, vst, qm, km, vmt):
    s = qs.shape[0]
    tq = SWA_TQ
    assert MEM_TQ == tq and MEM_HD == LANES
    nblk = tq // BLOCK
    n_kv = SWA_KV_HEADS * SWA_HD
    n_qs = SWA_HEADS * SWA_HD
    n_qm = MEM_HEADS * MEM_HD
    own = pl.BlockSpec((tq, n_kv), lambda i: (i, 0))
    prev = pl.BlockSpec((BLOCK, n_kv), lambda i: (jnp.maximum(i * nblk - 1, 0), 0))
    own_t = pl.BlockSpec((n_kv, tq), lambda i: (0, i))
    prev_t = pl.BlockSpec((n_kv, BLOCK), lambda i: (0, jnp.maximum(i * nblk - 1, 0)))
    smem = pl.BlockSpec(memory_space=pltpu.SMEM)
    return pl.pallas_call(
        _local_kernel,
        grid=(s // tq,),
        in_specs=[
            smem, smem, _const_spec((2 * BLOCK, BLOCK)),
            pl.BlockSpec((tq, n_qs), lambda i: (i, 0)),
            own, prev, own, prev, own_t, prev_t,
            pl.BlockSpec((tq, n_qm), lambda i: (i, 0)), _const_spec((MEM_LEN, n_qm)),
            _const_spec((n_qm, MEM_LEN)),
        ],
        out_specs=[pl.BlockSpec((tq, n_qs), lambda i: (i, 0)), pl.BlockSpec((tq, n_qm), lambda i: (i, 0))],
        out_shape=[jax.ShapeDtypeStruct((s, n_qs), BF16), jax.ShapeDtypeStruct((s, n_qm), BF16)],
        scratch_shapes=[pltpu.VMEM((2 * SWA_KV_HEADS, 2 * BLOCK, 2 * BLOCK), F32)],
        compiler_params=pltpu.CompilerParams(
            dimension_semantics=("arbitrary",), vmem_limit_bytes=VMEM_LIMIT),
        name="local_attn",
    )(rel_bias.reshape(-1), sinks, bkt, qs, ks, ks, ksw, ksw, vst, vst, qm, km, vmt)


def _post_kernel(x_ref, oa_ref, ob_ref, oc_ref, gate_ref, wa_ref, wb_ref, wc_ref, wout_ref,
                 gn_ref, wup_ref, wdn_ref, fn_ref, o_ref, *, final):
    d = D_MODEL
    y = (gate_ref[:, 0:d].astype(F32) * _dot(oa_ref[...], wa_ref[...])
         + gate_ref[:, d:2 * d].astype(F32) * _dot(ob_ref[...], wb_ref[...])
         + gate_ref[:, 2 * d:3 * d].astype(F32) * _dot(oc_ref[...], wc_ref[...]))
    x1 = x_ref[...] + _dot(y.astype(BF16), wout_ref[...])
    h = _rms(x1, gn_ref[...]).astype(BF16)
    acc = x1
    for c in range(D_FF // FF_CHUNK):
        lo = c * FF_CHUNK
        u = jnp.maximum(_dot(h, wup_ref[:, lo:lo + FF_CHUNK]), 0.0)
        acc = acc + _dot((u * u).astype(BF16), wdn_ref[lo:lo + FF_CHUNK, :])
    if final:
        acc = _rms(acc, fn_ref[...])
    o_ref[...] = acc


def _post_call(layer, x, oa, ob, oc, gates, wa, wb, wc, wout, gn, wup, wdn, fn, final):
    s = x.shape[0]
    tm = POST_TM
    d = D_MODEL

    def row_spec(n):
        return pl.BlockSpec((tm, n), lambda i: (i, 0))

    return pl.pallas_call(
        functools.partial(_post_kernel, final=final),
        grid=(s // tm,),
        in_specs=[
            row_spec(d), row_spec(oa.shape[1]), row_spec(ob.shape[1]), row_spec(oc.shape[1]),
            row_spec(N_BRANCH * d),
            _layer_spec(wa.shape[1:], layer), _layer_spec(wb.shape[1:], layer),
            _layer_spec(wc.shape[1:], layer), _layer_spec((d, d), layer),
            _layer_spec((1, d), layer), _layer_spec((d, D_FF), layer), _layer_spec((D_FF, d), layer),
            _const_spec((1, d)),
        ],
        out_specs=row_spec(d),
        out_shape=jax.ShapeDtypeStruct((s, d), F32),
        compiler_params=pltpu.CompilerParams(
            dimension_semantics=("arbitrary",), vmem_limit_bytes=VMEM_LIMIT),
        name="post",
    )(x, oa, ob, oc, gates, wa, wb, wc, wout, gn, wup, wdn, fn)


def _rot_cols(w):
    half = MLA_ROPE // 2
    return jnp.concatenate([-w[..., half:], w[..., :half]], axis=-1)


def _pad_cols(w, left, total):
    return jnp.pad(w, [(0, 0)] * (w.ndim - 1) + [(left, total - left - w.shape[-1])])


def _stacked_weights(w_in, w_uq, w_ukv):
    depth = w_in.shape[0]
    n_kv = SWA_KV_HEADS * SWA_HD
    row_scale = jnp.where(jnp.arange(_W_END) >= _W_GATE, 0.5, 1.0).astype(F32)
    w_t = (jnp.swapaxes(w_in, 1, 2) * row_scale[None, :, None]).astype(BF16)
    kpe = w_in[:, :, _H_KPE:_W_QS]
    ks_swapped = jnp.concatenate([w_in[:, :, _W_KS + SWA_HD:_W_KS + n_kv],
                                  w_in[:, :, _W_KS:_W_KS + SWA_HD]], axis=-1)
    w_aux_t = jnp.swapaxes(
        jnp.concatenate([_pad_cols(_rot_cols(kpe), MLA_NOPE, HEAD_PAD), ks_swapped], axis=-1),
        1, 2).astype(BF16)

    uq = w_uq.reshape(depth, MLA_Q_LORA, MLA_HEADS, MLA_NOPE + MLA_ROPE)
    uq_plain = _pad_cols(uq, 0, HEAD_PAD)
    uq_rot = _pad_cols(_rot_cols(uq[..., MLA_NOPE:]), MLA_NOPE, HEAD_PAD)
    wuq = jnp.concatenate([uq_plain.reshape(depth, MLA_Q_LORA, -1), uq_rot.reshape(depth, MLA_Q_LORA, -1)],
                          axis=-1).astype(BF16)

    ukv = w_ukv.reshape(depth, MLA_KV_LORA, MLA_HEADS, MLA_NOPE + MLA_V)
    wk = _pad_cols(ukv[..., :MLA_NOPE], 0, HEAD_PAD).reshape(depth, MLA_KV_LORA, -1).astype(BF16)
    wv = ukv[..., MLA_NOPE:].reshape(depth, MLA_KV_LORA, -1).astype(BF16)
    return w_t, w_aux_t, wuq, wk, wv


def _rope_table(seq):
    pos = jnp.arange(seq, dtype=F32)
    inv = 1.0 / (ROPE_THETA ** (jnp.arange(0, MLA_ROPE, 2, dtype=F32) / MLA_ROPE))
    ang = pos[:, None] * inv[None, :]
    cos = jnp.concatenate([jnp.cos(ang)] * 2, axis=-1)
    sin = jnp.concatenate([jnp.sin(ang)] * 2, axis=-1)
    return jnp.concatenate([_pad_cols(cos, MLA_NOPE, HEAD_PAD), _pad_cols(sin, MLA_NOPE, HEAD_PAD)],
                           axis=-1)


def _t5_bucket(dist):
    n = jnp.maximum(dist, 0)
    max_exact = REL_BUCKETS // 2
    nf = jnp.maximum(n, 1).astype(F32)
    large = max_exact + (jnp.log(nf / max_exact) / math.log(REL_MAX_DIST / max_exact)
                         * (REL_BUCKETS - max_exact)).astype(jnp.int32)
    large = jnp.minimum(large, REL_BUCKETS - 1)
    return jnp.where(n < max_exact, n, large)


def _swa_bucket_table():
    kj = jnp.arange(2 * BLOCK)[:, None]
    qi = jnp.arange(BLOCK)[None, :]
    dist = qi + BLOCK - kj
    band = (dist >= 0) & (dist < WINDOW)
    return jnp.where(band, _t5_bucket(dist), -1).astype(jnp.int32)


def kernel(x, mem, rel_bias, attn_norm, mem_norm, w_in, b_gate, mla_q_norm, w_uq, mla_kv_norm,
           w_ukv, attn_sinks, w_mem_kv, w_o_mla, w_o_swa, w_o_mem, w_out, mlp_norm, w_up, w_down,
           final_norm):
    batch, seq, d = x.shape
    assert batch == 1 and d == D_MODEL and mem.shape == (1, MEM_LEN, D_MODEL)
    depth = w_in.shape[0]
    xs = x[0]
    mem2 = mem[0]
    tab = _rope_table(seq)
    bkt = _swa_bucket_table()
    rel_bias = rel_bias.astype(F32)

    def rows(p):
        return p.reshape(depth, 1, -1)

    w_t, w_aux_t, wuq, wk, wv = _stacked_weights(w_in, w_uq, w_ukv)
    w_mem_kv, w_o_mla, w_o_swa, w_o_mem, w_out, w_up, w_down = (
        w.astype(BF16) for w in (w_mem_kv, w_o_mla, w_o_swa, w_o_mem, w_out, w_up, w_down))
    attn_norm, mem_norm, mla_q_norm, mla_kv_norm, mlp_norm, b_gate = (
        rows(p) for p in (attn_norm, mem_norm, mla_q_norm, mla_kv_norm, mlp_norm, b_gate))
    half_b_gate = 0.5 * b_gate
    fn = final_norm.reshape(1, d)

    for l in range(depth):
        q, k, vt, qs, ks, ksw, vst, qm, gates = _proj_call(
            l, xs, attn_norm, w_t, w_aux_t, mla_q_norm, wuq, mla_kv_norm, wk, wv, half_b_gate, tab)
        o_mla = _mla_call(q, k, vt)
        km, vmt = _memkv_call(l, mem2, mem_norm, w_mem_kv)
        o_swa, o_mem = _local_call(rel_bias, attn_sinks[l], bkt, qs, ks, ksw, vst, qm, km, vmt)
        xs = _post_call(l, xs, o_mla, o_swa, o_mem, gates, w_o_mla, w_o_swa, w_o_mem, w_out,
                        mlp_norm, w_up, w_down, fn, final=(l == depth - 1))
    return xs[None]
```

```python
import functools
import math

import jax
import jax.numpy as jnp
from jax import lax
from jax.experimental import pallas as pl
from jax.experimental.pallas import tpu as pltpu

F32 = jnp.float32
BF16 = jnp.bfloat16

D_MODEL = 1024
MLA_HEADS = 8
MLA_Q_LORA = 256
MLA_KV_LORA = 128
MLA_NOPE = 64
MLA_ROPE = 32
MLA_V = 64
ROPE_THETA = 10000.0
SWA_HEADS = 8
SWA_KV_HEADS = 2
SWA_HD = 64
WINDOW = 128
BLOCK = 128
REL_BUCKETS = 32
REL_MAX_DIST = 128
MEM_LEN = 256
MEM_HEADS = 4
MEM_HD = 128
N_BRANCH = 3
D_FF = 4 * D_MODEL
EPS = 1e-6

HEAD_PAD = 128
BF16_SUBLANES = 16
MLA_ACC_ROWS = MLA_V + BF16_SUBLANES
NEG_BIG = -1e30
LOG2E = math.log2(math.e)
LANES = 128

PROJ_TM = 512
MLA_TQ = 1024
MLA_TK = 512
MLA_TC = 256
MLA_HEADS_PER_STEP = 2
MLA_PAIRS_PER_TRIP = 2
MLA_LOOKAHEAD = 3
SWA_TQ = 512
SWA_LOOKAHEAD = 5
MEM_TQ = 512
POST_TM = 512
FF_CHUNK = 1024
VMEM_LIMIT = 56 * 1024 * 1024
N_LATER_WEIGHTS = 6

_H_CQ = 0
_H_CKV = _H_CQ + MLA_Q_LORA
_H_KPE = _H_CKV + MLA_KV_LORA
_W_CQ = _H_CQ
_W_QS = _H_KPE + MLA_ROPE
_W_KS = _W_QS + SWA_HEADS * SWA_HD
_W_GATE = _W_KS + 2 * SWA_KV_HEADS * SWA_HD + MEM_HEADS * MEM_HD
_W_END = _W_GATE + N_BRANCH * D_MODEL
_MLA_IN_COLS = 512
_R_QS = 0
_R_KS = _R_QS + SWA_HEADS * SWA_HD
_R_VS = _R_KS + SWA_KV_HEADS * SWA_HD
_R_QM = _R_VS + SWA_KV_HEADS * SWA_HD
_R_GATE = _R_QM + MEM_HEADS * MEM_HD
assert _W_QS % BF16_SUBLANES == 0 and _W_GATE == _W_QS + _R_GATE and _H_KPE + LANES <= _MLA_IN_COLS


def _rms(x, g):
    return x * lax.rsqrt(jnp.mean(x * x, axis=-1, keepdims=True) + EPS) * g


def _const_spec(shape):
    zeros = (0,) * len(shape)
    return pl.BlockSpec(shape, lambda *_: zeros, pipeline_mode=pl.Buffered(1))


def _layer_spec(shape, layer):
    index = (layer,) + (0,) * len(shape)
    return pl.BlockSpec((None,) + tuple(shape), lambda *_: index, pipeline_mode=pl.Buffered(1))


def _dot(a, b):
    return jnp.dot(a, b, preferred_element_type=F32)


def _dot_nt(a, b):
    return lax.dot_general(a, b, (((1,), (1,)), ((), ())), preferred_element_type=F32)


def _proj_kernel(x_ref, g_ref, wt_ref, wa_ref, qn_ref, wuq_ref, kvn_ref, wk_ref, wv_ref, bg_ref,
                 tab_ref, *refs):
    later_f32 = refs[:N_LATER_WEIGHTS]
    q_ref, k_ref, vt_ref, qs_ref, ks_ref, ksw_ref, vst_ref, qm_ref, gate_ref = refs[N_LATER_WEIGHTS:-N_LATER_WEIGHTS]
    later_bf16 = refs[-N_LATER_WEIGHTS:]
    for src, dst in zip(later_f32, later_bf16):
        dst[...] = src[...].astype(BF16)

    h = _rms(x_ref[...], g_ref[...]).astype(BF16)

    k_cos, k_sin = tab_ref[:, 0:HEAD_PAD], tab_ref[:, HEAD_PAD:2 * HEAD_PAD]
    q_scale = (MLA_NOPE + MLA_ROPE) ** -0.5 * LOG2E
    nope = (lax.broadcasted_iota(jnp.int32, k_cos.shape, 1) < MLA_NOPE).astype(F32)
    q_cos, q_sin = (k_cos + nope) * q_scale, k_sin * q_scale
    hw = MLA_HEADS * HEAD_PAD

    mla_in = _dot_nt(h, wt_ref[_W_CQ:_W_CQ + _MLA_IN_COLS, :])
    aux_in = _dot_nt(h, wa_ref[...])
    swa_in = _dot_nt(h, wt_ref[_W_QS:_W_GATE, :])

    cq = _rms(mla_in[:, _H_CQ:_H_CKV], qn_ref[...]).astype(BF16)
    qq = _dot(cq, wuq_ref[...])
    for hh in range(MLA_HEADS):
        lo = hh * HEAD_PAD
        q_ref[hh] = (qq[:, lo:lo + HEAD_PAD] * q_cos
                     + qq[:, hw + lo:hw + lo + HEAD_PAD] * q_sin).astype(BF16)

    ckv = _rms(mla_in[:, _H_CKV:_H_KPE], kvn_ref[...]).astype(BF16)
    kn = _dot(ckv, wk_ref[...])
    kpe = (pltpu.roll(mla_in[:, _H_KPE:_H_KPE + LANES], MLA_NOPE, axis=1) * k_cos
           + aux_in[:, 0:HEAD_PAD] * k_sin)
    for hh in range(MLA_HEADS):
        lo = hh * HEAD_PAD
        k_ref[hh] = (kn[:, lo:lo + HEAD_PAD] + kpe).astype(BF16)
    vt = _dot(ckv, wv_ref[...]).T
    for hh in range(MLA_HEADS):
        vt_ref[hh] = vt[hh * MLA_V:(hh + 1) * MLA_V, :].astype(BF16)

    qs_ref[...] = (swa_in[:, _R_QS:_R_KS] * (SWA_HD ** -0.5 * LOG2E)).astype(BF16)
    ks_ref[...] = swa_in[:, _R_KS:_R_VS].astype(BF16)
    ksw_ref[...] = aux_in[:, HEAD_PAD:].astype(BF16)
    vst_ref[...] = swa_in[:, _R_VS:_R_QM].T.astype(BF16)
    qm_ref[...] = (swa_in[:, _R_QM:_R_GATE] * (MEM_HD ** -0.5 * LOG2E)).astype(BF16)
    for c in range(N_BRANCH):
        lo = _W_GATE + c * D_MODEL
        half_z = _dot_nt(h, wt_ref[lo:lo + D_MODEL, :]) + bg_ref[:, c * D_MODEL:(c + 1) * D_MODEL]
        gate_ref[:, c * D_MODEL:(c + 1) * D_MODEL] = (0.5 * jnp.tanh(half_z) + 0.5).astype(BF16)


def _proj_call(layer, x, g, wt, wa, qn, wuq, kvn, wk, wv, bg, tab, later_weights):
    s = x.shape[0]
    tm = PROJ_TM
    steps = s // tm
    hw = MLA_HEADS * HEAD_PAD
    assert len(later_weights) == N_LATER_WEIGHTS
    assert all(w.shape[1] % (steps * BF16_SUBLANES) == 0 for w in later_weights)

    def slab_in(w):
        return pl.BlockSpec((None, w.shape[1] // steps, w.shape[2]), lambda i: (layer, i, 0))

    def slab_out(w):
        return pl.BlockSpec((w.shape[1] // steps, w.shape[2]), lambda i: (i, 0))

    head_spec = pl.BlockSpec((MLA_HEADS, tm, HEAD_PAD), lambda i: (0, i, 0))
    head_shape = jax.ShapeDtypeStruct((MLA_HEADS, s, HEAD_PAD), BF16)
    vt_spec = pl.BlockSpec((MLA_HEADS, MLA_V, tm), lambda i: (0, 0, i))
    vt_shape = jax.ShapeDtypeStruct((MLA_HEADS, MLA_V, s), BF16)

    def row_spec(n):
        return pl.BlockSpec((tm, n), lambda i: (i, 0))

    def row_shape(n):
        return jax.ShapeDtypeStruct((s, n), BF16)

    n_qs, n_kv, n_qm, n_g = SWA_HEADS * SWA_HD, SWA_KV_HEADS * SWA_HD, MEM_HEADS * MEM_HD, N_BRANCH * D_MODEL
    return pl.pallas_call(
        _proj_kernel,
        grid=(s // tm,),
        in_specs=[
            row_spec(D_MODEL),
            _layer_spec((1, D_MODEL), layer),
            _layer_spec((_W_END, D_MODEL), layer),
            _layer_spec((HEAD_PAD + n_kv, D_MODEL), layer),
            _layer_spec((1, MLA_Q_LORA), layer),
            _layer_spec((MLA_Q_LORA, 2 * hw), layer),
            _layer_spec((1, MLA_KV_LORA), layer),
            _layer_spec((MLA_KV_LORA, hw), layer),
            _layer_spec((MLA_KV_LORA, MLA_HEADS * MLA_V), layer),
            _layer_spec((1, n_g), layer),
            row_spec(2 * HEAD_PAD),
        ] + [slab_in(w) for w in later_weights],
        out_specs=[head_spec, head_spec, vt_spec,
                   row_spec(n_qs), row_spec(n_kv), row_spec(n_kv),
                   pl.BlockSpec((n_kv, tm), lambda i: (0, i)), row_spec(n_qm), row_spec(n_g)]
        + [slab_out(w) for w in later_weights],
        out_shape=[head_shape, head_shape, vt_shape,
                   row_shape(n_qs), row_shape(n_kv), row_shape(n_kv),
                   jax.ShapeDtypeStruct((n_kv, s), BF16), row_shape(n_qm), row_shape(n_g)]
        + [jax.ShapeDtypeStruct(w.shape[1:], BF16) for w in later_weights],
        compiler_params=pltpu.CompilerParams(
            dimension_semantics=("arbitrary",), vmem_limit_bytes=VMEM_LIMIT),
        name="proj",
    )(x, g, wt, wa, qn, wuq, kvn, wk, wv, bg, tab, *later_weights)


def _mla_kernel(q_ref, k_ref, vt_ref, o_ref, sa_scr, sb_scr, acc_scr):
    i = pl.program_id(1)
    tq, tk, tc, hp = MLA_TQ, MLA_TK, MLA_TC, MLA_HEADS_PER_STEP
    nc = tq // tc
    assert tq == 2 * tk and tk == 2 * tc
    assert hp * MLA_V == LANES
    key = lax.broadcasted_iota(jnp.int32, (tk, tc), 0)
    lane = lax.broadcasted_iota(jnp.int32, (tk, tc), 1)
    ones_rows = (lax.broadcasted_iota(jnp.int32, (MLA_ACC_ROWS - MLA_V, tk), 0) == 0).astype(BF16)

    units = [(h, c) for c in range(nc) for h in range(hp)]
    slot = {u: n for n, u in enumerate(units)}

    def scores(j, buf, u, key_off=None, or_valid=None):
        h, c = u
        start = pl.multiple_of(j * tk, tk)
        s = _dot_nt(k_ref[h, pl.ds(start, tk), :], q_ref[h, c * tc:(c + 1) * tc, :])
        if key_off is not None:
            valid = key + key_off <= lane + c * tc
            if or_valid is not None:
                valid = jnp.logical_or(valid, or_valid)
            s = jnp.where(valid, s, NEG_BIG)
        buf[slot[u]] = s
        return jnp.max(s, axis=0, keepdims=True)

    def accumulate(j, buf, u, m, cm):
        h, c = u
        start = pl.multiple_of(j * tk, tk)
        m_new = jnp.maximum(m, cm)
        alpha = jnp.exp2(m - m_new)
        p = jnp.exp2(buf[slot[u]] - m_new).astype(BF16)
        v1 = jnp.concatenate([vt_ref[h, :, pl.ds(start, tk)], ones_rows], axis=0)
        acc_scr[slot[u]] = alpha * acc_scr[slot[u]] + _dot(v1, p)
        return m_new

    def run(j0, carry, chunks, produce_next):
        m, cm_first = dict(zip(units, carry[0])), dict(zip(units, carry[1]))
        cm = {(0, u): cm_first[u] for u in units}
        todo = [(r, u) for r in range(1, len(chunks)) for u in chunks[r][0]]
        n_acc = len(chunks) - (1 if produce_next else 0)
        order = []

        def issue():
            r, u = todo.pop(0)
            kind = chunks[r][1]
            key_off = {None: None, "a": 0 if u[1] * tc < tk else None, "b": tk}[kind]
            cm[(r, u)] = scores(j0 + r, sb_scr if r % 2 else sa_scr, u, key_off=key_off)
            order.append(("score", r, u))

        for _ in range(min(MLA_LOOKAHEAD, len(todo))):
            issue()
        for r in range(n_acc):
            for u in chunks[r][0]:
                m[u] = accumulate(j0 + r, sb_scr if r % 2 else sa_scr, u, m[u], cm[(r, u)])
                order.append(("acc", r, u))
                if todo:
                    issue()
        for pos, (what, r, u) in enumerate(order):
            if what == "score" and ("acc", r - 2, u) in order:
                assert order.index(("acc", r - 2, u)) < pos
        next_cm = tuple(cm[(len(chunks) - 1, u)] for u in units) if produce_next else None
        return tuple(m[u] for u in units), next_cm

    plain = (units, None)
    late = [u for u in units if (u[1] + 1) * tc > tk]

    def finish(carry, with_pair_before):
        chunks = ([plain, plain] if with_pair_before else []) + [(units, "a"), (late, "b")]
        run(2 * i - (2 if with_pair_before else 0), carry, chunks, False)
        o_t = jnp.concatenate(
            [jnp.concatenate([acc_scr[slot[(h, c)], 0:MLA_V, :] / acc_scr[slot[(h, c)], MLA_V:MLA_V + 1, :]
                              for c in range(nc)], axis=1) for h in range(hp)], axis=0)
        for blk in range(tq // LANES):
            o_ref[blk * LANES:(blk + 1) * LANES, :] = o_t[:, blk * LANES:(blk + 1) * LANES].T.astype(BF16)

    acc_scr[...] = jnp.zeros_like(acc_scr)
    cm_a = tuple(scores(0, sa_scr, u, key_off=0, or_valid=i > 0) for u in units)
    carry = (tuple(jnp.full((1, tc), NEG_BIG, F32) for _ in units), cm_a)
    n_plain = jnp.maximum(i - 1, 0)
    n_long = n_plain // MLA_PAIRS_PER_TRIP
    carry = lax.fori_loop(
        0, n_long,
        lambda t, c: run(2 * MLA_PAIRS_PER_TRIP * t, c, [plain] * (2 * MLA_PAIRS_PER_TRIP + 1), True), carry)
    carry = lax.fori_loop(
        n_long * MLA_PAIRS_PER_TRIP, n_plain, lambda t, c: run(2 * t, c, [plain] * 3, True), carry)

    @pl.when(i > 0)
    def _finish_after_pairs():
        finish(carry, True)

    @pl.when(i == 0)
    def _finish_first_tile():
        finish(carry, False)


def _mla_call(q, k, vt):
    s = q.shape[1]
    tq, hp = MLA_TQ, MLA_HEADS_PER_STEP
    n_units = hp * (tq // MLA_TC)
    return pl.pallas_call(
        _mla_kernel,
        grid=(MLA_HEADS // hp, s // tq),
        in_specs=[
            pl.BlockSpec((hp, tq, HEAD_PAD), lambda h, i: (h, i, 0)),
            pl.BlockSpec((hp, s, HEAD_PAD), lambda h, i: (h, 0, 0)),
            pl.BlockSpec((hp, MLA_V, s), lambda h, i: (h, 0, 0)),
        ],
        out_specs=pl.BlockSpec((tq, hp * MLA_V), lambda h, i: (i, h)),
        out_shape=jax.ShapeDtypeStruct((s, MLA_HEADS * MLA_V), BF16),
        scratch_shapes=[pltpu.VMEM((n_units, MLA_TK, MLA_TC), F32), pltpu.VMEM((n_units, MLA_TK, MLA_TC), F32),
                        pltpu.VMEM((n_units, MLA_ACC_ROWS, MLA_TC), F32)],
        compiler_params=pltpu.CompilerParams(
            dimension_semantics=("arbitrary", "arbitrary"), vmem_limit_bytes=VMEM_LIMIT),
        name="mla_attn",
    )(q, k, vt)


def _swa_kernel(relb_ref, sink_ref, bkt_ref, q_ref, k_ref, kp_ref, ksw_ref, kswp_ref, vt_ref, vtp_ref,
                o_ref, bias_scr):
    i = pl.program_id(0)
    tq, hd, rep = SWA_TQ, SWA_HD, SWA_HEADS // SWA_KV_HEADS
    win = 2 * BLOCK
    assert 2 * hd == LANES and rep == 4 and BLOCK == LANES

    @pl.when(i == 0)
    def _build_bias():
        bkt = bkt_ref[...]
        for hh in range(SWA_HEADS):
            t = jnp.full((win, BLOCK), NEG_BIG, F32)
            for b in range(REL_BUCKETS):
                t = jnp.where(bkt == b, relb_ref[b * SWA_HEADS + hh] * LOG2E, t)
            g, r = divmod(hh, rep)
            bias_scr[2 * g + r % 2, :, (r // 2) * BLOCK:(r // 2 + 1) * BLOCK] = t

    lo_half = lax.broadcasted_iota(jnp.int32, (BLOCK + tq, LANES), 1) < hd
    kfull = jnp.concatenate([kp_ref[...], k_ref[...]], axis=0)
    kswfull = jnp.concatenate([kswp_ref[...], ksw_ref[...]], axis=0)
    zero = jnp.zeros_like(kfull)
    k_lo = (jnp.where(lo_half, kfull, zero), jnp.where(lo_half, kswfull, zero))
    k_hi = (jnp.where(lo_half, zero, kswfull), jnp.where(lo_half, zero, kfull))
    vfull = jnp.concatenate([vtp_ref[...], vt_ref[...]], axis=1)
    ones_rows = (lax.broadcasted_iota(jnp.int32, (BF16_SUBLANES, win), 0) == 0).astype(BF16)
    key_row = lax.broadcasted_iota(jnp.int32, (win, 2 * BLOCK), 0)
    first_valid = jnp.logical_or(key_row >= BLOCK, i > 0)

    sinks = [jnp.concatenate(
        [jnp.full((1, BLOCK), sink_ref[g * rep + half] * LOG2E, F32),
         jnp.full((1, BLOCK), sink_ref[g * rep + 2 + half] * LOG2E, F32)], axis=1)
        for g in range(SWA_KV_HEADS) for half in range(2)]

    tiles = [(b, g, half) for b in range(tq // BLOCK) for g in range(SWA_KV_HEADS) for half in range(2)]

    def tile_scores(b, g, half):
        r0 = b * BLOCK
        qg = jnp.concatenate([q_ref[r0:r0 + BLOCK, (2 * g) * LANES:(2 * g + 1) * LANES],
                              q_ref[r0:r0 + BLOCK, (2 * g + 1) * LANES:(2 * g + 2) * LANES]],
                             axis=0)
        kmat = (k_lo, k_hi)[half][g]
        s = _dot_nt(kmat[r0:r0 + win], qg) + bias_scr[2 * g + half]
        return jnp.where(first_valid, s, NEG_BIG) if b == 0 else s

    def tile_output(b, g, half, s):
        r0 = b * BLOCK
        sink = sinks[2 * g + half]
        v1 = jnp.concatenate([vfull[g * hd:(g + 1) * hd, r0:r0 + win], ones_rows], axis=0)
        m = jnp.maximum(jnp.max(s, axis=0, keepdims=True), sink)
        p = jnp.exp2(s - m).astype(BF16)
        ot = _dot(v1, p)
        den = ot[hd:hd + 1] + jnp.exp2(sink - m)
        return ot[0:hd] / den

    pending = [tile_scores(*tile) for tile in tiles[:SWA_LOOKAHEAD]]
    outs = {}
    for t, (b, g, half) in enumerate(tiles):
        s_cur = pending.pop(0)
        if t + SWA_LOOKAHEAD < len(tiles):
            pending.append(tile_scores(*tiles[t + SWA_LOOKAHEAD]))
        outs[half] = tile_output(b, g, half, s_cur)
        if half == 1:
            r0 = b * BLOCK
            for pr in range(2):
                x = jnp.concatenate([outs[0][:, pr * BLOCK:(pr + 1) * BLOCK],
                                     outs[1][:, pr * BLOCK:(pr + 1) * BLOCK]], axis=0)
                o_ref[r0:r0 + BLOCK, (2 * g + pr) * LANES:(2 * g + pr + 1) * LANES] = x.T.astype(BF16)


def _memkv_kernel(mem_ref, g_ref, w_ref, k_ref, vt_ref):
    n = MEM_HEADS * MEM_HD
    mn = _rms(mem_ref[...], g_ref[...]).astype(BF16)
    k_ref[...] = _dot(mn, w_ref[:, 0:n]).astype(BF16)
    vt_ref[...] = _dot(mn, w_ref[:, n:2 * n]).T.astype(BF16)


def _memkv_call(layer, mem, g, w):
    n = MEM_HEADS * MEM_HD
    return pl.pallas_call(
        _memkv_kernel,
        grid=(1,),
        in_specs=[_const_spec((MEM_LEN, D_MODEL)), _layer_spec((1, D_MODEL), layer),
                  _layer_spec((D_MODEL, 2 * n), layer)],
        out_specs=[_const_spec((MEM_LEN, n)), _const_spec((n, MEM_LEN))],
        out_shape=[jax.ShapeDtypeStruct((MEM_LEN, n), BF16), jax.ShapeDtypeStruct((n, MEM_LEN), BF16)],
        compiler_params=pltpu.CompilerParams(
            dimension_semantics=("arbitrary",), vmem_limit_bytes=VMEM_LIMIT),
        name="mem_kv",
    )(mem, g, w)


def _mem_kernel(q_ref, k_ref, vt_ref, o_ref):
    tq, dm = MEM_TQ, MEM_HD
    ones_rows = (lax.broadcasted_iota(jnp.int32, (BF16_SUBLANES, MEM_LEN), 0) == 0).astype(BF16)
    scores = [_dot_nt(k_ref[:, hh * dm:(hh + 1) * dm], q_ref[:, hh * dm:(hh + 1) * dm])
              for hh in range(MEM_HEADS)]
    for hh, s in enumerate(scores):
        p = jnp.exp2(s - jnp.max(s, axis=0, keepdims=True)).astype(BF16)
        v1 = jnp.concatenate([vt_ref[hh * dm:(hh + 1) * dm, :], ones_rows], axis=0)
        ot = _dot(v1, p)
        o = ot[0:dm] / ot[dm:dm + 1]
        for blk in range(tq // LANES):
            o_ref[blk * LANES:(blk + 1) * LANES, hh * dm:(hh + 1) * dm] = (
                o[:, blk * LANES:(blk + 1) * LANES].T.astype(BF16))


def _local_kernel(relb_ref, sink_ref, bkt_ref, q_ref, k_ref, kp_ref, ksw_ref, kswp_ref, vt_ref, vtp_ref,
                  qm_ref, km_ref, vmt_ref, o_ref, om_ref, bias_scr):
    _swa_kernel(relb_ref, sink_ref, bkt_ref, q_ref, k_ref, kp_ref, ksw_ref, kswp_ref, vt_ref, vtp_ref,
                o_ref, bias_scr)
    _mem_kernel(qm_ref, km_ref, vmt_ref, om_ref)


def _local_call(rel_bias, sinks, bkt, qs, ks, ksw, vst, qm, km, vmt):
    s = qs.shape[0]
    tq = SWA_TQ
    assert MEM_TQ == tq and MEM_HD == LANES
    nblk = tq // BLOCK
    n_kv = SWA_KV_HEADS * SWA_HD
    n_qs = SWA_HEADS * SWA_HD
    n_qm = MEM_HEADS * MEM_HD
    own = pl.BlockSpec((tq, n_kv), lambda i: (i, 0))
    prev = pl.BlockSpec((BLOCK, n_kv), lambda i: (jnp.maximum(i * nblk - 1, 0), 0))
    own_t = pl.BlockSpec((n_kv, tq), lambda i: (0, i))
    prev_t = pl.BlockSpec((n_kv, BLOCK), lambda i: (0, jnp.maximum(i * nblk - 1, 0)))
    smem = pl.BlockSpec(memory_space=pltpu.SMEM)
    return pl.pallas_call(
        _local_kernel,
        grid=(s // tq,),
        in_specs=[
            smem, smem, _const_spec((2 * BLOCK, BLOCK)),
            pl.BlockSpec((tq, n_qs), lambda i: (i, 0)),
            own, prev, own, prev, own_t, prev_t,
            pl.BlockSpec((tq, n_qm), lambda i: (i, 0)), _const_spec((MEM_LEN, n_qm)),
            _const_spec((n_qm, MEM_LEN)),
        ],
        out_specs=[pl.BlockSpec((tq, n_qs), lambda i: (i, 0)), pl.BlockSpec((tq, n_qm), lambda i: (i, 0))],
        out_shape=[jax.ShapeDtypeStruct((s, n_qs), BF16), jax.ShapeDtypeStruct((s, n_qm), BF16)],
        scratch_shapes=[pltpu.VMEM((2 * SWA_KV_HEADS, 2 * BLOCK, 2 * BLOCK), F32)],
        compiler_params=pltpu.CompilerParams(
            dimension_semantics=("arbitrary",), vmem_limit_bytes=VMEM_LIMIT),
        name="local_attn",
    )(rel_bias.reshape(-1), sinks, bkt, qs, ks, ks, ksw, ksw, vst, vst, qm, km, vmt)


def _post_kernel(x_ref, oa_ref, ob_ref, oc_ref, gate_ref, wa_ref, wb_ref, wc_ref, wout_ref,
                 gn_ref, wup_ref, wdn_ref, fn_ref, o_ref, *, final):
    d = D_MODEL
    y = (gate_ref[:, 0:d].astype(F32) * _dot(oa_ref[...], wa_ref[...])
         + gate_ref[:, d:2 * d].astype(F32) * _dot(ob_ref[...], wb_ref[...])
         + gate_ref[:, 2 * d:3 * d].astype(F32) * _dot(oc_ref[...], wc_ref[...]))
    x1 = x_ref[...] + _dot(y.astype(BF16), wout_ref[...])
    h = _rms(x1, gn_ref[...]).astype(BF16)
    acc = x1
    for c in range(D_FF // FF_CHUNK):
        lo = c * FF_CHUNK
        u = jnp.maximum(_dot(h, wup_ref[:, lo:lo + FF_CHUNK]), 0.0)
        acc = acc + _dot((u * u).astype(BF16), wdn_ref[lo:lo + FF_CHUNK, :])
    if final:
        acc = _rms(acc, fn_ref[...])
    o_ref[...] = acc


def _post_call(layer, x, oa, ob, oc, gates, wa, wb, wc, wout, gn, wup, wdn, fn, final):
    s = x.shape[0]
    tm = POST_TM
    d = D_MODEL

    def row_spec(n):
        return pl.BlockSpec((tm, n), lambda i: (i, 0))

    return pl.pallas_call(
        functools.partial(_post_kernel, final=final),
        grid=(s // tm,),
        in_specs=[
            row_spec(d), row_spec(oa.shape[1]), row_spec(ob.shape[1]), row_spec(oc.shape[1]),
            row_spec(N_BRANCH * d),
            _const_spec(wa.shape), _const_spec(wb.shape), _const_spec(wc.shape), _const_spec((d, d)),
            _layer_spec((1, d), layer), _const_spec((d, D_FF)), _const_spec((D_FF, d)),
            _const_spec((1, d)),
        ],
        out_specs=row_spec(d),
        out_shape=jax.ShapeDtypeStruct((s, d), F32),
        compiler_params=pltpu.CompilerParams(
            dimension_semantics=("arbitrary",), vmem_limit_bytes=VMEM_LIMIT),
        name="post",
    )(x, oa, ob, oc, gates, wa, wb, wc, wout, gn, wup, wdn, fn)


def _rot_cols(w):
    half = MLA_ROPE // 2
    return jnp.concatenate([-w[..., half:], w[..., :half]], axis=-1)


def _pad_cols(w, left, total):
    return jnp.pad(w, [(0, 0)] * (w.ndim - 1) + [(left, total - left - w.shape[-1])])


def _stacked_weights(w_in, w_uq, w_ukv):
    depth = w_in.shape[0]
    n_kv = SWA_KV_HEADS * SWA_HD
    row_scale = jnp.where(jnp.arange(_W_END) >= _W_GATE, 0.5, 1.0).astype(F32)
    w_t = (jnp.swapaxes(w_in, 1, 2) * row_scale[None, :, None]).astype(BF16)
    kpe = w_in[:, :, _H_KPE:_W_QS]
    ks_swapped = jnp.concatenate([w_in[:, :, _W_KS + SWA_HD:_W_KS + n_kv],
                                  w_in[:, :, _W_KS:_W_KS + SWA_HD]], axis=-1)
    w_aux_t = jnp.swapaxes(
        jnp.concatenate([_pad_cols(_rot_cols(kpe), MLA_NOPE, HEAD_PAD), ks_swapped], axis=-1),
        1, 2).astype(BF16)

    uq = w_uq.reshape(depth, MLA_Q_LORA, MLA_HEADS, MLA_NOPE + MLA_ROPE)
    uq_plain = _pad_cols(uq, 0, HEAD_PAD)
    uq_rot = _pad_cols(_rot_cols(uq[..., MLA_NOPE:]), MLA_NOPE, HEAD_PAD)
    wuq = jnp.concatenate([uq_plain.reshape(depth, MLA_Q_LORA, -1), uq_rot.reshape(depth, MLA_Q_LORA, -1)],
                          axis=-1).astype(BF16)

    ukv = w_ukv.reshape(depth, MLA_KV_LORA, MLA_HEADS, MLA_NOPE + MLA_V)
    wk = _pad_cols(ukv[..., :MLA_NOPE], 0, HEAD_PAD).reshape(depth, MLA_KV_LORA, -1).astype(BF16)
    wv = ukv[..., MLA_NOPE:].reshape(depth, MLA_KV_LORA, -1).astype(BF16)
    return w_t, w_aux_t, wuq, wk, wv


def _rope_table(seq):
    pos = jnp.arange(seq, dtype=F32)
    inv = 1.0 / (ROPE_THETA ** (jnp.arange(0, MLA_ROPE, 2, dtype=F32) / MLA_ROPE))
    ang = pos[:, None] * inv[None, :]
    cos = jnp.concatenate([jnp.cos(ang)] * 2, axis=-1)
    sin = jnp.concatenate([jnp.sin(ang)] * 2, axis=-1)
    return jnp.concatenate([_pad_cols(cos, MLA_NOPE, HEAD_PAD), _pad_cols(sin, MLA_NOPE, HEAD_PAD)],
                           axis=-1)


def _t5_bucket(dist):
    n = jnp.maximum(dist, 0)
    max_exact = REL_BUCKETS // 2
    nf = jnp.maximum(n, 1).astype(F32)
    large = max_exact + (jnp.log(nf / max_exact) / math.log(REL_MAX_DIST / max_exact)
                         * (REL_BUCKETS - max_exact)).astype(jnp.int32)
    large = jnp.minimum(large, REL_BUCKETS - 1)
    return jnp.where(n < max_exact, n, large)


def _swa_bucket_table():
    kj = jnp.arange(2 * BLOCK)[:, None]
    qi = jnp.arange(BLOCK)[None, :]
    dist = qi + BLOCK - kj
    band = (dist >= 0) & (dist < WINDOW)
    return jnp.where(band, _t5_bucket(dist), -1).astype(jnp.int32)


def kernel(x, mem, rel_bias, attn_norm, mem_norm, w_in, b_gate, mla_q_norm, w_uq, mla_kv_norm,
           w_ukv, attn_sinks, w_mem_kv, w_o_mla, w_o_swa, w_o_mem, w_out, mlp_norm, w_up, w_down,
           final_norm):
    batch, seq, d = x.shape
    assert batch == 1 and d == D_MODEL and mem.shape == (1, MEM_LEN, D_MODEL)
    depth = w_in.shape[0]
    xs = x[0]
    mem2 = mem[0]
    tab = _rope_table(seq)
    bkt = _swa_bucket_table()
    rel_bias = rel_bias.astype(F32)

    def rows(p):
        return p.reshape(depth, 1, -1)

    w_t, w_aux_t, wuq, wk, wv = _stacked_weights(w_in, w_uq, w_ukv)
    w_mem_kv = w_mem_kv.astype(BF16)
    later_weights = (w_o_mla, w_o_swa, w_o_mem, w_out, w_up, w_down)
    attn_norm, mem_norm, mla_q_norm, mla_kv_norm, mlp_norm, b_gate = (
        rows(p) for p in (attn_norm, mem_norm, mla_q_norm, mla_kv_norm, mlp_norm, b_gate))
    half_b_gate = 0.5 * b_gate
    fn = final_norm.reshape(1, d)

    for l in range(depth):
        q, k, vt, qs, ks, ksw, vst, qm, gates, wa, wb, wc, wout, wup, wdn = _proj_call(
            l, xs, attn_norm, w_t, w_aux_t, mla_q_norm, wuq, mla_kv_norm, wk, wv, half_b_gate, tab,
            later_weights)
        o_mla = _mla_call(q, k, vt)
        km, vmt = _memkv_call(l, mem2, mem_norm, w_mem_kv)
        o_swa, o_mem = _local_call(rel_bias, attn_sinks[l], bkt, qs, ks, ksw, vst, qm, km, vmt)
        xs = _post_call(l, xs, o_mla, o_swa, o_mem, gates, wa, wb, wc, wout, mlp_norm, wup, wdn, fn,
                        final=(l == depth - 1))
    return xs[None]
```

```python
import functools
import math

import jax
import jax.numpy as jnp
from jax import lax
from jax.experimental import pallas as pl
from jax.experimental.pallas import tpu as pltpu

F32 = jnp.float32
BF16 = jnp.bfloat16

D_MODEL = 1024
MLA_HEADS = 8
MLA_Q_LORA = 256
MLA_KV_LORA = 128
MLA_NOPE = 64
MLA_ROPE = 32
MLA_V = 64
ROPE_THETA = 10000.0
SWA_HEADS = 8
SWA_KV_HEADS = 2
SWA_HD = 64
WINDOW = 128
BLOCK = 128
REL_BUCKETS = 32
REL_MAX_DIST = 128
MEM_LEN = 256
MEM_HEADS = 4
MEM_HD = 128
N_BRANCH = 3
D_FF = 4 * D_MODEL
EPS = 1e-6

HEAD_PAD = 128
BF16_SUBLANES = 16
MLA_ACC_ROWS = MLA_V + BF16_SUBLANES
NEG_BIG = -1e30
LOG2E = math.log2(math.e)
LANES = 128

PROJ_TM = 512
MLA_TQ = 1024
MLA_TK = 512
MLA_TC = 256
MLA_HEADS_PER_STEP = 2
MLA_PAIRS_PER_TRIP = 2
MLA_LOOKAHEAD = 3
SWA_TQ = 512
LOCAL_LOOKAHEAD = 5
MEM_TQ = 512
POST_TM = 512
FF_CHUNK = 1024
VMEM_LIMIT = 56 * 1024 * 1024
N_LATER_WEIGHTS = 6

_H_CQ = 0
_H_CKV = _H_CQ + MLA_Q_LORA
_H_KPE = _H_CKV + MLA_KV_LORA
_W_CQ = _H_CQ
_W_QS = _H_KPE + MLA_ROPE
_W_KS = _W_QS + SWA_HEADS * SWA_HD
_W_GATE = _W_KS + 2 * SWA_KV_HEADS * SWA_HD + MEM_HEADS * MEM_HD
_W_END = _W_GATE + N_BRANCH * D_MODEL
_MLA_IN_COLS = 512
_R_QS = 0
_R_KS = _R_QS + SWA_HEADS * SWA_HD
_R_VS = _R_KS + SWA_KV_HEADS * SWA_HD
_R_QM = _R_VS + SWA_KV_HEADS * SWA_HD
_R_GATE = _R_QM + MEM_HEADS * MEM_HD
assert _W_QS % BF16_SUBLANES == 0 and _W_GATE == _W_QS + _R_GATE and _H_KPE + LANES <= _MLA_IN_COLS


def _rms(x, g):
    return x * lax.rsqrt(jnp.mean(x * x, axis=-1, keepdims=True) + EPS) * g


def _const_spec(shape):
    zeros = (0,) * len(shape)
    return pl.BlockSpec(shape, lambda *_: zeros, pipeline_mode=pl.Buffered(1))


def _layer_spec(shape, layer):
    index = (layer,) + (0,) * len(shape)
    return pl.BlockSpec((None,) + tuple(shape), lambda *_: index, pipeline_mode=pl.Buffered(1))


def _dot(a, b):
    return jnp.dot(a, b, preferred_element_type=F32)


def _dot_nt(a, b):
    return lax.dot_general(a, b, (((1,), (1,)), ((), ())), preferred_element_type=F32)


def _proj_kernel(x_ref, g_ref, wt_ref, wa_ref, qn_ref, wuq_ref, kvn_ref, wk_ref, wv_ref, bg_ref,
                 tab_ref, *refs):
    later_f32 = refs[:N_LATER_WEIGHTS]
    q_ref, k_ref, vt_ref, qs_ref, ks_ref, ksw_ref, vst_ref, qm_ref, gate_ref = refs[N_LATER_WEIGHTS:-N_LATER_WEIGHTS]
    later_bf16 = refs[-N_LATER_WEIGHTS:]
    for src, dst in zip(later_f32, later_bf16):
        dst[...] = src[...].astype(BF16)

    h = _rms(x_ref[...], g_ref[...]).astype(BF16)

    tab = tab_ref[...]
    lane = lax.broadcasted_iota(jnp.int32, tab.shape, 1)
    on_rope = jnp.logical_and(lane >= MLA_NOPE, lane < MLA_NOPE + MLA_ROPE)
    k_cos = jnp.where(on_rope, tab, 0.0)
    k_sin = jnp.where(on_rope, pltpu.roll(tab, HEAD_PAD - MLA_ROPE, axis=1), 0.0)
    q_scale = (MLA_NOPE + MLA_ROPE) ** -0.5 * LOG2E
    nope = (lane < MLA_NOPE).astype(F32)
    q_cos, q_sin = (k_cos + nope) * q_scale, k_sin * q_scale
    hw = MLA_HEADS * HEAD_PAD

    mla_in = _dot_nt(h, wt_ref[_W_CQ:_W_CQ + _MLA_IN_COLS, :])
    aux_in = _dot_nt(h, wa_ref[...])
    swa_in = _dot_nt(h, wt_ref[_W_QS:_W_GATE, :])

    cq = _rms(mla_in[:, _H_CQ:_H_CKV], qn_ref[...]).astype(BF16)
    qq = _dot(cq, wuq_ref[...])
    for hh in range(MLA_HEADS):
        lo = hh * HEAD_PAD
        q_ref[hh] = (qq[:, lo:lo + HEAD_PAD] * q_cos
                     + qq[:, hw + lo:hw + lo + HEAD_PAD] * q_sin).astype(BF16)

    ckv = _rms(mla_in[:, _H_CKV:_H_KPE], kvn_ref[...]).astype(BF16)
    kn = _dot(ckv, wk_ref[...])
    kpe = (pltpu.roll(mla_in[:, _H_KPE:_H_KPE + LANES], MLA_NOPE, axis=1) * k_cos
           + aux_in[:, 0:HEAD_PAD] * k_sin)
    for hh in range(MLA_HEADS):
        lo = hh * HEAD_PAD
        k_ref[hh] = (kn[:, lo:lo + HEAD_PAD] + kpe).astype(BF16)
    vt = _dot(ckv, wv_ref[...]).T
    for hh in range(MLA_HEADS):
        vt_ref[hh] = vt[hh * MLA_V:(hh + 1) * MLA_V, :].astype(BF16)

    qs_ref[...] = (swa_in[:, _R_QS:_R_KS] * (SWA_HD ** -0.5 * LOG2E)).astype(BF16)
    ks_ref[...] = swa_in[:, _R_KS:_R_VS].astype(BF16)
    ksw_ref[...] = aux_in[:, HEAD_PAD:].astype(BF16)
    vst_ref[...] = swa_in[:, _R_VS:_R_QM].T.astype(BF16)
    qm_ref[...] = (swa_in[:, _R_QM:_R_GATE] * (MEM_HD ** -0.5 * LOG2E)).astype(BF16)
    for c in range(N_BRANCH):
        lo = _W_GATE + c * D_MODEL
        half_z = _dot_nt(h, wt_ref[lo:lo + D_MODEL, :]) + bg_ref[:, c * D_MODEL:(c + 1) * D_MODEL]
        gate_ref[:, c * D_MODEL:(c + 1) * D_MODEL] = (0.5 * jnp.tanh(half_z) + 0.5).astype(BF16)


def _proj_call(layer, x, g, wt, wa, qn, wuq, kvn, wk, wv, bg, tab, later_weights):
    s = x.shape[0]
    tm = PROJ_TM
    steps = s // tm
    hw = MLA_HEADS * HEAD_PAD
    assert len(later_weights) == N_LATER_WEIGHTS
    assert all(w.shape[1] % (steps * BF16_SUBLANES) == 0 for w in later_weights)

    def slab_in(w):
        return pl.BlockSpec((None, w.shape[1] // steps, w.shape[2]), lambda i: (layer, i, 0))

    def slab_out(w):
        return pl.BlockSpec((w.shape[1] // steps, w.shape[2]), lambda i: (i, 0))

    head_spec = pl.BlockSpec((MLA_HEADS, tm, HEAD_PAD), lambda i: (0, i, 0))
    head_shape = jax.ShapeDtypeStruct((MLA_HEADS, s, HEAD_PAD), BF16)
    vt_spec = pl.BlockSpec((MLA_HEADS, MLA_V, tm), lambda i: (0, 0, i))
    vt_shape = jax.ShapeDtypeStruct((MLA_HEADS, MLA_V, s), BF16)

    def row_spec(n):
        return pl.BlockSpec((tm, n), lambda i: (i, 0))

    def row_shape(n):
        return jax.ShapeDtypeStruct((s, n), BF16)

    n_qs, n_kv, n_qm, n_g = SWA_HEADS * SWA_HD, SWA_KV_HEADS * SWA_HD, MEM_HEADS * MEM_HD, N_BRANCH * D_MODEL
    return pl.pallas_call(
        _proj_kernel,
        grid=(s // tm,),
        in_specs=[
            row_spec(D_MODEL),
            _layer_spec((1, D_MODEL), layer),
            _layer_spec((_W_END, D_MODEL), layer),
            _layer_spec((HEAD_PAD + n_kv, D_MODEL), layer),
            _layer_spec((1, MLA_Q_LORA), layer),
            _layer_spec((MLA_Q_LORA, 2 * hw), layer),
            _layer_spec((1, MLA_KV_LORA), layer),
            _layer_spec((MLA_KV_LORA, hw), layer),
            _layer_spec((MLA_KV_LORA, MLA_HEADS * MLA_V), layer),
            _layer_spec((1, n_g), layer),
            row_spec(HEAD_PAD),
        ] + [slab_in(w) for w in later_weights],
        out_specs=[head_spec, head_spec, vt_spec,
                   row_spec(n_qs), row_spec(n_kv), row_spec(n_kv),
                   pl.BlockSpec((n_kv, tm), lambda i: (0, i)), row_spec(n_qm), row_spec(n_g)]
        + [slab_out(w) for w in later_weights],
        out_shape=[head_shape, head_shape, vt_shape,
                   row_shape(n_qs), row_shape(n_kv), row_shape(n_kv),
                   jax.ShapeDtypeStruct((n_kv, s), BF16), row_shape(n_qm), row_shape(n_g)]
        + [jax.ShapeDtypeStruct(w.shape[1:], BF16) for w in later_weights],
        compiler_params=pltpu.CompilerParams(
            dimension_semantics=("arbitrary",), vmem_limit_bytes=VMEM_LIMIT),
        name="proj",
    )(x, g, wt, wa, qn, wuq, kvn, wk, wv, bg, tab, *later_weights)


def _mla_kernel(q_ref, k_ref, vt_ref, o_ref, sa_scr, sb_scr, acc_scr):
    i = pl.program_id(1)
    tq, tk, tc, hp = MLA_TQ, MLA_TK, MLA_TC, MLA_HEADS_PER_STEP
    nc = tq // tc
    assert tq == 2 * tk and tk == 2 * tc
    assert hp * MLA_V == LANES
    key = lax.broadcasted_iota(jnp.int32, (tk, tc), 0)
    lane = lax.broadcasted_iota(jnp.int32, (tk, tc), 1)
    ones_rows = (lax.broadcasted_iota(jnp.int32, (MLA_ACC_ROWS - MLA_V, tk), 0) == 0).astype(BF16)

    units = [(h, c) for h in range(hp) for c in range(nc)]
    slot = {u: n for n, u in enumerate(units)}

    def scores(j, buf, u, key_off=None, or_valid=None):
        h, c = u
        start = pl.multiple_of(j * tk, tk)
        s = _dot_nt(k_ref[h, pl.ds(start, tk), :], q_ref[h, c * tc:(c + 1) * tc, :])
        if key_off is not None:
            valid = key + key_off <= lane + c * tc
            if or_valid is not None:
                valid = jnp.logical_or(valid, or_valid)
            s = jnp.where(valid, s, NEG_BIG)
        buf[slot[u]] = s
        return jnp.max(s, axis=0, keepdims=True)

    def accumulate(j, buf, u, m, cm):
        h, c = u
        start = pl.multiple_of(j * tk, tk)
        m_new = jnp.maximum(m, cm)
        alpha = jnp.exp2(m - m_new)
        p = jnp.exp2(buf[slot[u]] - m_new).astype(BF16)
        v1 = jnp.concatenate([vt_ref[h, :, pl.ds(start, tk)], ones_rows], axis=0)
        acc_scr[slot[u]] = alpha * acc_scr[slot[u]] + _dot(v1, p)
        return m_new

    def run(j0, carry, chunks, produce_next):
        m, cm_first = dict(zip(units, carry[0])), dict(zip(units, carry[1]))
        cm = {(0, u): cm_first[u] for u in units}
        todo = [(r, u) for r in range(1, len(chunks)) for u in chunks[r][0]]
        n_acc = len(chunks) - (1 if produce_next else 0)
        order = []

        def issue():
            r, u = todo.pop(0)
            kind = chunks[r][1]
            key_off = {None: None, "a": 0 if u[1] * tc < tk else None, "b": tk}[kind]
            cm[(r, u)] = scores(j0 + r, sb_scr if r % 2 else sa_scr, u, key_off=key_off)
            order.append(("score", r, u))

        for _ in range(min(MLA_LOOKAHEAD, len(todo))):
            issue()
        for r in range(n_acc):
            for u in chunks[r][0]:
                m[u] = accumulate(j0 + r, sb_scr if r % 2 else sa_scr, u, m[u], cm[(r, u)])
                order.append(("acc", r, u))
                if todo:
                    issue()
        for pos, (what, r, u) in enumerate(order):
            if what == "score" and ("acc", r - 2, u) in order:
                assert order.index(("acc", r - 2, u)) < pos
        next_cm = tuple(cm[(len(chunks) - 1, u)] for u in units) if produce_next else None
        return tuple(m[u] for u in units), next_cm

    plain = (units, None)
    late = [u for u in units if (u[1] + 1) * tc > tk]

    def finish(carry, with_pair_before):
        chunks = ([plain, plain] if with_pair_before else []) + [(units, "a"), (late, "b")]
        run(2 * i - (2 if with_pair_before else 0), carry, chunks, False)
        o_t = jnp.concatenate(
            [jnp.concatenate([acc_scr[slot[(h, c)], 0:MLA_V, :] / acc_scr[slot[(h, c)], MLA_V:MLA_V + 1, :]
                              for c in range(nc)], axis=1) for h in range(hp)], axis=0)
        for blk in range(tq // LANES):
            o_ref[blk * LANES:(blk + 1) * LANES, :] = o_t[:, blk * LANES:(blk + 1) * LANES].T.astype(BF16)

    acc_scr[...] = jnp.zeros_like(acc_scr)
    cm_a = tuple(scores(0, sa_scr, u, key_off=0, or_valid=i > 0) for u in units)
    carry = (tuple(jnp.full((1, tc), NEG_BIG, F32) for _ in units), cm_a)
    n_plain = jnp.maximum(i - 1, 0)
    n_long = n_plain // MLA_PAIRS_PER_TRIP
    carry = lax.fori_loop(
        0, n_long,
        lambda t, c: run(2 * MLA_PAIRS_PER_TRIP * t, c, [plain] * (2 * MLA_PAIRS_PER_TRIP + 1), True), carry)
    carry = lax.fori_loop(
        n_long * MLA_PAIRS_PER_TRIP, n_plain, lambda t, c: run(2 * t, c, [plain] * 3, True), carry)

    @pl.when(i > 0)
    def _finish_after_pairs():
        finish(carry, True)

    @pl.when(i == 0)
    def _finish_first_tile():
        finish(carry, False)


def _mla_call(q, k, vt):
    s = q.shape[1]
    tq, hp = MLA_TQ, MLA_HEADS_PER_STEP
    n_units = hp * (tq // MLA_TC)
    return pl.pallas_call(
        _mla_kernel,
        grid=(MLA_HEADS // hp, s // tq),
        in_specs=[
            pl.BlockSpec((hp, tq, HEAD_PAD), lambda h, i: (h, i, 0)),
            pl.BlockSpec((hp, s, HEAD_PAD), lambda h, i: (h, 0, 0)),
            pl.BlockSpec((hp, MLA_V, s), lambda h, i: (h, 0, 0)),
        ],
        out_specs=pl.BlockSpec((tq, hp * MLA_V), lambda h, i: (i, h)),
        out_shape=jax.ShapeDtypeStruct((s, MLA_HEADS * MLA_V), BF16),
        scratch_shapes=[pltpu.VMEM((n_units, MLA_TK, MLA_TC), F32), pltpu.VMEM((n_units, MLA_TK, MLA_TC), F32),
                        pltpu.VMEM((n_units, MLA_ACC_ROWS, MLA_TC), F32)],
        compiler_params=pltpu.CompilerParams(
            dimension_semantics=("arbitrary", "arbitrary"), vmem_limit_bytes=VMEM_LIMIT),
        name="mla_attn",
    )(q, k, vt)


def _swa_jobs(relb_ref, sink_ref, bkt_ref, q_ref, k_ref, kp_ref, ksw_ref, kswp_ref, vt_ref, vtp_ref,
              o_ref, bias_scr):
    i = pl.program_id(0)
    tq, hd, rep = SWA_TQ, SWA_HD, SWA_HEADS // SWA_KV_HEADS
    win = 2 * BLOCK
    assert 2 * hd == LANES and rep == 4 and BLOCK == LANES

    @pl.when(i == 0)
    def _build_bias():
        bkt = bkt_ref[...]
        for hh in range(SWA_HEADS):
            t = jnp.full((win, BLOCK), NEG_BIG, F32)
            for b in range(REL_BUCKETS):
                t = jnp.where(bkt == b, relb_ref[b * SWA_HEADS + hh] * LOG2E, t)
            g, r = divmod(hh, rep)
            bias_scr[2 * g + r % 2, :, (r // 2) * BLOCK:(r // 2 + 1) * BLOCK] = t

    lo_half = lax.broadcasted_iota(jnp.int32, (BLOCK + tq, LANES), 1) < hd
    kfull = jnp.concatenate([kp_ref[...], k_ref[...]], axis=0)
    kswfull = jnp.concatenate([kswp_ref[...], ksw_ref[...]], axis=0)
    zero = jnp.zeros_like(kfull)
    k_lo = (jnp.where(lo_half, kfull, zero), jnp.where(lo_half, kswfull, zero))
    k_hi = (jnp.where(lo_half, zero, kswfull), jnp.where(lo_half, zero, kfull))
    vfull = jnp.concatenate([vtp_ref[...], vt_ref[...]], axis=1)
    ones_rows = (lax.broadcasted_iota(jnp.int32, (BF16_SUBLANES, win), 0) == 0).astype(BF16)
    key_row = lax.broadcasted_iota(jnp.int32, (win, 2 * BLOCK), 0)
    first_valid = jnp.logical_or(key_row >= BLOCK, i > 0)

    sinks = [jnp.concatenate(
        [jnp.full((1, BLOCK), sink_ref[g * rep + half] * LOG2E, F32),
         jnp.full((1, BLOCK), sink_ref[g * rep + 2 + half] * LOG2E, F32)], axis=1)
        for g in range(SWA_KV_HEADS) for half in range(2)]

    tiles = [(b, g, half) for b in range(tq // BLOCK) for g in range(SWA_KV_HEADS) for half in range(2)]

    def tile_scores(b, g, half):
        r0 = b * BLOCK
        qg = jnp.concatenate([q_ref[r0:r0 + BLOCK, (2 * g) * LANES:(2 * g + 1) * LANES],
                              q_ref[r0:r0 + BLOCK, (2 * g + 1) * LANES:(2 * g + 2) * LANES]],
                             axis=0)
        kmat = (k_lo, k_hi)[half][g]
        s = _dot_nt(kmat[r0:r0 + win], qg) + bias_scr[2 * g + half]
        return jnp.where(first_valid, s, NEG_BIG) if b == 0 else s

    def tile_output(b, g, half, s):
        r0 = b * BLOCK
        sink = sinks[2 * g + half]
        v1 = jnp.concatenate([vfull[g * hd:(g + 1) * hd, r0:r0 + win], ones_rows], axis=0)
        m = jnp.maximum(jnp.max(s, axis=0, keepdims=True), sink)
        p = jnp.exp2(s - m).astype(BF16)
        ot = _dot(v1, p)
        den = ot[hd:hd + 1] + jnp.exp2(sink - m)
        return ot[0:hd] / den

    outs = {}

    def tile_finish(b, g, half, s):
        outs[half] = tile_output(b, g, half, s)
        if half == 1:
            r0 = b * BLOCK
            for pr in range(2):
                x = jnp.concatenate([outs[0][:, pr * BLOCK:(pr + 1) * BLOCK],
                                     outs[1][:, pr * BLOCK:(pr + 1) * BLOCK]], axis=0)
                o_ref[r0:r0 + BLOCK, (2 * g + pr) * LANES:(2 * g + pr + 1) * LANES] = x.T.astype(BF16)

    return [(functools.partial(tile_scores, *tile), functools.partial(tile_finish, *tile)) for tile in tiles]


def _run_jobs(jobs, lookahead):
    pending = [score() for score, _ in jobs[:lookahead]]
    for t, (_, finish) in enumerate(jobs):
        s = pending.pop(0)
        if t + lookahead < len(jobs):
            pending.append(jobs[t + lookahead][0]())
        finish(s)


def _memkv_kernel(mem_ref, g_ref, w_ref, k_ref, vt_ref):
    n = MEM_HEADS * MEM_HD
    mn = _rms(mem_ref[...], g_ref[...]).astype(BF16)
    k_ref[...] = _dot(mn, w_ref[:, 0:n]).astype(BF16)
    vt_ref[...] = _dot(mn, w_ref[:, n:2 * n]).T.astype(BF16)


def _memkv_call(layer, mem, g, w):
    n = MEM_HEADS * MEM_HD
    return pl.pallas_call(
        _memkv_kernel,
        grid=(1,),
        in_specs=[_const_spec((MEM_LEN, D_MODEL)), _layer_spec((1, D_MODEL), layer),
                  _layer_spec((D_MODEL, 2 * n), layer)],
        out_specs=[_const_spec((MEM_LEN, n)), _const_spec((n, MEM_LEN))],
        out_shape=[jax.ShapeDtypeStruct((MEM_LEN, n), BF16), jax.ShapeDtypeStruct((n, MEM_LEN), BF16)],
        compiler_params=pltpu.CompilerParams(
            dimension_semantics=("arbitrary",), vmem_limit_bytes=VMEM_LIMIT),
        name="mem_kv",
    )(mem, g, w)


def _mem_jobs(q_ref, k_ref, vt_ref, o_ref):
    tq, dm = MEM_TQ, MEM_HD
    ones_rows = (lax.broadcasted_iota(jnp.int32, (BF16_SUBLANES, MEM_LEN), 0) == 0).astype(BF16)

    def head_scores(hh):
        return _dot_nt(k_ref[:, hh * dm:(hh + 1) * dm], q_ref[:, hh * dm:(hh + 1) * dm])

    def head_finish(hh, s):
        p = jnp.exp2(s - jnp.max(s, axis=0, keepdims=True)).astype(BF16)
        v1 = jnp.concatenate([vt_ref[hh * dm:(hh + 1) * dm, :], ones_rows], axis=0)
        ot = _dot(v1, p)
        o = ot[0:dm] / ot[dm:dm + 1]
        for blk in range(tq // LANES):
            o_ref[blk * LANES:(blk + 1) * LANES, hh * dm:(hh + 1) * dm] = (
                o[:, blk * LANES:(blk + 1) * LANES].T.astype(BF16))

    return [(functools.partial(head_scores, hh), functools.partial(head_finish, hh))
            for hh in range(MEM_HEADS)]


def _local_kernel(relb_ref, sink_ref, bkt_ref, q_ref, k_ref, kp_ref, ksw_ref, kswp_ref, vt_ref, vtp_ref,
                  qm_ref, km_ref, vmt_ref, o_ref, om_ref, bias_scr):
    swa = _swa_jobs(relb_ref, sink_ref, bkt_ref, q_ref, k_ref, kp_ref, ksw_ref, kswp_ref, vt_ref, vtp_ref,
                    o_ref, bias_scr)
    mem = _mem_jobs(qm_ref, km_ref, vmt_ref, om_ref)
    every = len(swa) // len(mem)
    jobs = []
    for n, job in enumerate(swa):
        jobs.append(job)
        if n % every == every - 1:
            jobs.append(mem[n // every])
    _run_jobs(jobs, LOCAL_LOOKAHEAD)


def _local_call(rel_bias, sinks, bkt, qs, ks, ksw, vst, qm, km, vmt):
    s = qs.shape[0]
    tq = SWA_TQ
    assert MEM_TQ == tq and MEM_HD == LANES
    nblk = tq // BLOCK
    n_kv = SWA_KV_HEADS * SWA_HD
    n_qs = SWA_HEADS * SWA_HD
    n_qm = MEM_HEADS * MEM_HD
    own = pl.BlockSpec((tq, n_kv), lambda i: (i, 0))
    prev = pl.BlockSpec((BLOCK, n_kv), lambda i: (jnp.maximum(i * nblk - 1, 0), 0))
    own_t = pl.BlockSpec((n_kv, tq), lambda i: (0, i))
    prev_t = pl.BlockSpec((n_kv, BLOCK), lambda i: (0, jnp.maximum(i * nblk - 1, 0)))
    smem = pl.BlockSpec(memory_space=pltpu.SMEM)
    return pl.pallas_call(
        _local_kernel,
        grid=(s // tq,),
        in_specs=[
            smem, smem, _const_spec((2 * BLOCK, BLOCK)),
            pl.BlockSpec((tq, n_qs), lambda i: (i, 0)),
            own, prev, own, prev, own_t, prev_t,
            pl.BlockSpec((tq, n_qm), lambda i: (i, 0)), _const_spec((MEM_LEN, n_qm)),
            _const_spec((n_qm, MEM_LEN)),
        ],
        out_specs=[pl.BlockSpec((tq, n_qs), lambda i: (i, 0)), pl.BlockSpec((tq, n_qm), lambda i: (i, 0))],
        out_shape=[jax.ShapeDtypeStruct((s, n_qs), BF16), jax.ShapeDtypeStruct((s, n_qm), BF16)],
        scratch_shapes=[pltpu.VMEM((2 * SWA_KV_HEADS, 2 * BLOCK, 2 * BLOCK), F32)],
        compiler_params=pltpu.CompilerParams(
            dimension_semantics=("arbitrary",), vmem_limit_bytes=VMEM_LIMIT),
        name="local_attn",
    )(rel_bias.reshape(-1), sinks, bkt, qs, ks, ks, ksw, ksw, vst, vst, qm, km, vmt)


def _post_kernel(x_ref, oa_ref, ob_ref, oc_ref, gate_ref, wa_ref, wb_ref, wc_ref, wout_ref,
                 gn_ref, wup_ref, wdn_ref, fn_ref, o_ref, *, final):
    d = D_MODEL
    y = (gate_ref[:, 0:d].astype(F32) * _dot(oa_ref[...], wa_ref[...])
         + gate_ref[:, d:2 * d].astype(F32) * _dot(ob_ref[...], wb_ref[...])
         + gate_ref[:, 2 * d:3 * d].astype(F32) * _dot(oc_ref[...], wc_ref[...]))
    x1 = x_ref[...] + _dot(y.astype(BF16), wout_ref[...])
    h = _rms(x1, gn_ref[...]).astype(BF16)
    acc = x1
    for c in range(D_FF // FF_CHUNK):
        lo = c * FF_CHUNK
        u = jnp.maximum(_dot(h, wup_ref[:, lo:lo + FF_CHUNK]), 0.0)
        acc = acc + _dot((u * u).astype(BF16), wdn_ref[lo:lo + FF_CHUNK, :])
    if final:
        acc = _rms(acc, fn_ref[...])
    o_ref[...] = acc


def _post_call(layer, x, oa, ob, oc, gates, wa, wb, wc, wout, gn, wup, wdn, fn, final):
    s = x.shape[0]
    tm = POST_TM
    d = D_MODEL

    def row_spec(n):
        return pl.BlockSpec((tm, n), lambda i: (i, 0))

    return pl.pallas_call(
        functools.partial(_post_kernel, final=final),
        grid=(s // tm,),
        in_specs=[
            row_spec(d), row_spec(oa.shape[1]), row_spec(ob.shape[1]), row_spec(oc.shape[1]),
            row_spec(N_BRANCH * d),
            _const_spec(wa.shape), _const_spec(wb.shape), _const_spec(wc.shape), _const_spec((d, d)),
            _layer_spec((1, d), layer), _const_spec((d, D_FF)), _const_spec((D_FF, d)),
            _const_spec((1, d)),
        ],
        out_specs=row_spec(d),
        out_shape=jax.ShapeDtypeStruct((s, d), F32),
        compiler_params=pltpu.CompilerParams(
            dimension_semantics=("arbitrary",), vmem_limit_bytes=VMEM_LIMIT),
        name="post",
    )(x, oa, ob, oc, gates, wa, wb, wc, wout, gn, wup, wdn, fn)


def _rot_cols(w):
    half = MLA_ROPE // 2
    return jnp.concatenate([-w[..., half:], w[..., :half]], axis=-1)


def _pad_cols(w, left, total):
    return jnp.pad(w, [(0, 0)] * (w.ndim - 1) + [(left, total - left - w.shape[-1])])


def _stacked_weights(w_in, w_uq, w_ukv):
    depth = w_in.shape[0]
    n_kv = SWA_KV_HEADS * SWA_HD
    row_scale = jnp.where(jnp.arange(_W_END) >= _W_GATE, 0.5, 1.0).astype(F32)
    w_t = (jnp.swapaxes(w_in, 1, 2) * row_scale[None, :, None]).astype(BF16)
    kpe = w_in[:, :, _H_KPE:_W_QS]
    ks_swapped = jnp.concatenate([w_in[:, :, _W_KS + SWA_HD:_W_KS + n_kv],
                                  w_in[:, :, _W_KS:_W_KS + SWA_HD]], axis=-1)
    w_aux_t = jnp.swapaxes(
        jnp.concatenate([_pad_cols(_rot_cols(kpe), MLA_NOPE, HEAD_PAD), ks_swapped], axis=-1),
        1, 2).astype(BF16)

    uq = w_uq.reshape(depth, MLA_Q_LORA, MLA_HEADS, MLA_NOPE + MLA_ROPE)
    uq_plain = _pad_cols(uq, 0, HEAD_PAD)
    uq_rot = _pad_cols(_rot_cols(uq[..., MLA_NOPE:]), MLA_NOPE, HEAD_PAD)
    wuq = jnp.concatenate([uq_plain.reshape(depth, MLA_Q_LORA, -1), uq_rot.reshape(depth, MLA_Q_LORA, -1)],
                          axis=-1).astype(BF16)

    ukv = w_ukv.reshape(depth, MLA_KV_LORA, MLA_HEADS, MLA_NOPE + MLA_V)
    wk = _pad_cols(ukv[..., :MLA_NOPE], 0, HEAD_PAD).reshape(depth, MLA_KV_LORA, -1).astype(BF16)
    wv = ukv[..., MLA_NOPE:].reshape(depth, MLA_KV_LORA, -1).astype(BF16)
    return w_t, w_aux_t, wuq, wk, wv


def _rope_table(seq):
    pos = jnp.arange(seq, dtype=F32)
    inv = 1.0 / (ROPE_THETA ** (jnp.arange(0, MLA_ROPE, 2, dtype=F32) / MLA_ROPE))
    ang = pos[:, None] * inv[None, :]
    cos = jnp.concatenate([jnp.cos(ang)] * 2, axis=-1)
    sin = jnp.concatenate([jnp.sin(ang)] * 2, axis=-1)
    assert MLA_NOPE + 2 * MLA_ROPE == HEAD_PAD
    return jnp.concatenate([jnp.zeros((seq, MLA_NOPE), F32), cos, sin], axis=-1)


def _t5_bucket(dist):
    n = jnp.maximum(dist, 0)
    max_exact = REL_BUCKETS // 2
    nf = jnp.maximum(n, 1).astype(F32)
    large = max_exact + (jnp.log(nf / max_exact) / math.log(REL_MAX_DIST / max_exact)
                         * (REL_BUCKETS - max_exact)).astype(jnp.int32)
    large = jnp.minimum(large, REL_BUCKETS - 1)
    return jnp.where(n < max_exact, n, large)


def _swa_bucket_table():
    kj = jnp.arange(2 * BLOCK)[:, None]
    qi = jnp.arange(BLOCK)[None, :]
    dist = qi + BLOCK - kj
    band = (dist >= 0) & (dist < WINDOW)
    return jnp.where(band, _t5_bucket(dist), -1).astype(jnp.int32)


def kernel(x, mem, rel_bias, attn_norm, mem_norm, w_in, b_gate, mla_q_norm, w_uq, mla_kv_norm,
           w_ukv, attn_sinks, w_mem_kv, w_o_mla, w_o_swa, w_o_mem, w_out, mlp_norm, w_up, w_down,
           final_norm):
    batch, seq, d = x.shape
    assert batch == 1 and d == D_MODEL and mem.shape == (1, MEM_LEN, D_MODEL)
    depth = w_in.shape[0]
    xs = x[0]
    mem2 = mem[0]
    tab = _rope_table(seq)
    bkt = _swa_bucket_table()
    rel_bias = rel_bias.astype(F32)

    def rows(p):
        return p.reshape(depth, 1, -1)

    w_t, w_aux_t, wuq, wk, wv = _stacked_weights(w_in, w_uq, w_ukv)
    w_mem_kv = w_mem_kv.astype(BF16)
    later_weights = (w_o_mla, w_o_swa, w_o_mem, w_out, w_up, w_down)
    attn_norm, mem_norm, mla_q_norm, mla_kv_norm, mlp_norm, b_gate = (
        rows(p) for p in (attn_norm, mem_norm, mla_q_norm, mla_kv_norm, mlp_norm, b_gate))
    half_b_gate = 0.5 * b_gate
    fn = final_norm.reshape(1, d)

    for l in range(depth):
        q, k, vt, qs, ks, ksw, vst, qm, gates, wa, wb, wc, wout, wup, wdn = _proj_call(
            l, xs, attn_norm, w_t, w_aux_t, mla_q_norm, wuq, mla_kv_norm, wk, wv, half_b_gate, tab,
            later_weights)
        o_mla = _mla_call(q, k, vt)
        km, vmt = _memkv_call(l, mem2, mem_norm, w_mem_kv)
        o_swa, o_mem = _local_call(rel_bias, attn_sinks[l], bkt, qs, ks, ksw, vst, qm, km, vmt)
        xs = _post_call(l, xs, o_mla, o_swa, o_mem, gates, wa, wb, wc, wout, mlp_norm, wup, wdn, fn,
                        final=(l == depth - 1))
    return xs[None]
```

```python
import functools
import math

import jax
import jax.numpy as jnp
from jax import lax
from jax.experimental import pallas as pl
from jax.experimental.pallas import tpu as pltpu

F32 = jnp.float32
BF16 = jnp.bfloat16

D_MODEL = 1024
MLA_HEADS = 8
MLA_Q_LORA = 256
MLA_KV_LORA = 128
MLA_NOPE = 64
MLA_ROPE = 32
MLA_V = 64
ROPE_THETA = 10000.0
SWA_HEADS = 8
SWA_KV_HEADS = 2
SWA_HD = 64
WINDOW = 128
BLOCK = 128
REL_BUCKETS = 32
REL_MAX_DIST = 128
MEM_LEN = 256
MEM_HEADS = 4
MEM_HD = 128
N_BRANCH = 3
D_FF = 4 * D_MODEL
EPS = 1e-6

HEAD_PAD = 128
BF16_SUBLANES = 16
MLA_ACC_ROWS = MLA_V + BF16_SUBLANES
NEG_BIG = -1e30
LOG2E = math.log2(math.e)
LANES = 128

PROJ_TM = 512
MLA_TQ = 1024
MLA_TK = 512
MLA_TC = 256
MLA_HEADS_PER_STEP = 2
MLA_PAIRS_PER_TRIP = 2
MLA_LOOKAHEAD = 3
SWA_TQ = 512
LOCAL_LOOKAHEAD = 5
MEM_TQ = 512
POST_TM = 512
FF_CHUNK = 1024
VMEM_LIMIT = 56 * 1024 * 1024
N_LATER_WEIGHTS = 7

_H_CQ = 0
_H_CKV = _H_CQ + MLA_Q_LORA
_H_KPE = _H_CKV + MLA_KV_LORA
_W_CQ = _H_CQ
_W_QS = _H_KPE + MLA_ROPE
_W_KS = _W_QS + SWA_HEADS * SWA_HD
_W_GATE = _W_KS + 2 * SWA_KV_HEADS * SWA_HD + MEM_HEADS * MEM_HD
_W_END = _W_GATE + N_BRANCH * D_MODEL
_MLA_IN_COLS = 512
_R_QS = 0
_R_KS = _R_QS + SWA_HEADS * SWA_HD
_R_VS = _R_KS + SWA_KV_HEADS * SWA_HD
_R_QM = _R_VS + SWA_KV_HEADS * SWA_HD
_R_GATE = _R_QM + MEM_HEADS * MEM_HD
assert _W_QS % BF16_SUBLANES == 0 and _W_GATE == _W_QS + _R_GATE and _H_KPE + LANES <= _MLA_IN_COLS


def _rms(x, g):
    return x * lax.rsqrt(jnp.mean(x * x, axis=-1, keepdims=True) + EPS) * g


def _const_spec(shape):
    zeros = (0,) * len(shape)
    return pl.BlockSpec(shape, lambda *_: zeros, pipeline_mode=pl.Buffered(1))


def _layer_spec(shape, layer):
    index = (layer,) + (0,) * len(shape)
    return pl.BlockSpec((None,) + tuple(shape), lambda *_: index, pipeline_mode=pl.Buffered(1))


def _dot(a, b):
    return jnp.dot(a, b, preferred_element_type=F32)


def _dot_nt(a, b):
    return lax.dot_general(a, b, (((1,), (1,)), ((), ())), preferred_element_type=F32)


def _proj_kernel(x_ref, g_ref, wt_ref, wa_ref, qn_ref, wuq_ref, kvn_ref, wk_ref, wv_ref, bg_ref,
                 tab_ref, *refs):
    later_f32 = refs[:N_LATER_WEIGHTS]
    q_ref, k_ref, vt_ref, qs_ref, ks_ref, ksw_ref, vst_ref, qm_ref, gate_ref = refs[N_LATER_WEIGHTS:-N_LATER_WEIGHTS]
    later_bf16 = refs[-N_LATER_WEIGHTS:]
    for src, dst in zip(later_f32, later_bf16):
        dst[...] = src[...].astype(BF16)

    h = _rms(x_ref[...], g_ref[...]).astype(BF16)

    k_cos, k_sin = tab_ref[:, 0:HEAD_PAD], tab_ref[:, HEAD_PAD:2 * HEAD_PAD]
    q_scale = (MLA_NOPE + MLA_ROPE) ** -0.5 * LOG2E
    nope = (lax.broadcasted_iota(jnp.int32, k_cos.shape, 1) < MLA_NOPE).astype(F32)
    q_cos, q_sin = (k_cos + nope) * q_scale, k_sin * q_scale
    hw = MLA_HEADS * HEAD_PAD

    mla_in = _dot_nt(h, wt_ref[_W_CQ:_W_CQ + _MLA_IN_COLS, :])
    aux_in = _dot_nt(h, wa_ref[...])
    swa_in = _dot_nt(h, wt_ref[_W_QS:_W_GATE, :])

    cq = _rms(mla_in[:, _H_CQ:_H_CKV], qn_ref[...]).astype(BF16)
    qq = _dot(cq, wuq_ref[...])
    for hh in range(MLA_HEADS):
        lo = hh * HEAD_PAD
        q_ref[hh] = (qq[:, lo:lo + HEAD_PAD] * q_cos
                     + qq[:, hw + lo:hw + lo + HEAD_PAD] * q_sin).astype(BF16)

    ckv = _rms(mla_in[:, _H_CKV:_H_KPE], kvn_ref[...]).astype(BF16)
    kn = _dot(ckv, wk_ref[...])
    kpe = (pltpu.roll(mla_in[:, _H_KPE:_H_KPE + LANES], MLA_NOPE, axis=1) * k_cos
           + aux_in[:, 0:HEAD_PAD] * k_sin)
    for hh in range(MLA_HEADS):
        lo = hh * HEAD_PAD
        k_ref[hh] = (kn[:, lo:lo + HEAD_PAD] + kpe).astype(BF16)
    vt = _dot(ckv, wv_ref[...]).T
    for hh in range(MLA_HEADS):
        vt_ref[hh] = vt[hh * MLA_V:(hh + 1) * MLA_V, :].astype(BF16)

    qs_ref[...] = (swa_in[:, _R_QS:_R_KS] * (SWA_HD ** -0.5 * LOG2E)).astype(BF16)
    ks_ref[...] = swa_in[:, _R_KS:_R_VS].astype(BF16)
    ksw_ref[...] = aux_in[:, HEAD_PAD:].astype(BF16)
    vst_ref[...] = swa_in[:, _R_VS:_R_QM].T.astype(BF16)
    qm_ref[...] = (swa_in[:, _R_QM:_R_GATE] * (MEM_HD ** -0.5 * LOG2E)).astype(BF16)
    for c in range(N_BRANCH):
        lo = _W_GATE + c * D_MODEL
        half_z = _dot_nt(h, wt_ref[lo:lo + D_MODEL, :]) + bg_ref[:, c * D_MODEL:(c + 1) * D_MODEL]
        gate_ref[:, c * D_MODEL:(c + 1) * D_MODEL] = (0.5 * jnp.tanh(half_z) + 0.5).astype(BF16)


def _proj_call(layer, x, g, wt, wa, qn, wuq, kvn, wk, wv, bg, tab, later_weights):
    s = x.shape[0]
    tm = PROJ_TM
    steps = s // tm
    hw = MLA_HEADS * HEAD_PAD
    assert len(later_weights) == N_LATER_WEIGHTS
    assert all(w.shape[1] % (steps * BF16_SUBLANES) == 0 for w in later_weights)

    def slab_in(w):
        return pl.BlockSpec((None, w.shape[1] // steps, w.shape[2]), lambda i: (layer, i, 0))

    def slab_out(w):
        return pl.BlockSpec((w.shape[1] // steps, w.shape[2]), lambda i: (i, 0))

    head_spec = pl.BlockSpec((MLA_HEADS, tm, HEAD_PAD), lambda i: (0, i, 0))
    head_shape = jax.ShapeDtypeStruct((MLA_HEADS, s, HEAD_PAD), BF16)
    vt_spec = pl.BlockSpec((MLA_HEADS, MLA_V, tm), lambda i: (0, 0, i))
    vt_shape = jax.ShapeDtypeStruct((MLA_HEADS, MLA_V, s), BF16)

    def row_spec(n):
        return pl.BlockSpec((tm, n), lambda i: (i, 0))

    def row_shape(n):
        return jax.ShapeDtypeStruct((s, n), BF16)

    n_qs, n_kv, n_qm, n_g = SWA_HEADS * SWA_HD, SWA_KV_HEADS * SWA_HD, MEM_HEADS * MEM_HD, N_BRANCH * D_MODEL
    return pl.pallas_call(
        _proj_kernel,
        grid=(s // tm,),
        in_specs=[
            row_spec(D_MODEL),
            _layer_spec((1, D_MODEL), layer),
            _layer_spec((_W_END, D_MODEL), layer),
            _layer_spec((HEAD_PAD + n_kv, D_MODEL), layer),
            _layer_spec((1, MLA_Q_LORA), layer),
            _layer_spec((MLA_Q_LORA, 2 * hw), layer),
            _layer_spec((1, MLA_KV_LORA), layer),
            _layer_spec((MLA_KV_LORA, hw), layer),
            _layer_spec((MLA_KV_LORA, MLA_HEADS * MLA_V), layer),
            _layer_spec((1, n_g), layer),
            row_spec(2 * HEAD_PAD),
        ] + [slab_in(w) for w in later_weights],
        out_specs=[head_spec, head_spec, vt_spec,
                   row_spec(n_qs), row_spec(n_kv), row_spec(n_kv),
                   pl.BlockSpec((n_kv, tm), lambda i: (0, i)), row_spec(n_qm), row_spec(n_g)]
        + [slab_out(w) for w in later_weights],
        out_shape=[head_shape, head_shape, vt_shape,
                   row_shape(n_qs), row_shape(n_kv), row_shape(n_kv),
                   jax.ShapeDtypeStruct((n_kv, s), BF16), row_shape(n_qm), row_shape(n_g)]
        + [jax.ShapeDtypeStruct(w.shape[1:], BF16) for w in later_weights],
        compiler_params=pltpu.CompilerParams(
            dimension_semantics=("arbitrary",), vmem_limit_bytes=VMEM_LIMIT),
        name="proj",
    )(x, g, wt, wa, qn, wuq, kvn, wk, wv, bg, tab, *later_weights)


def _mla_kernel(q_ref, k_ref, vt_ref, o_ref, sa_scr, sb_scr, acc_scr):
    i = pl.program_id(1)
    tq, tk, tc, hp = MLA_TQ, MLA_TK, MLA_TC, MLA_HEADS_PER_STEP
    nc = tq // tc
    assert tq == 2 * tk and tk == 2 * tc
    assert hp * MLA_V == LANES
    key = lax.broadcasted_iota(jnp.int32, (tk, tc), 0)
    lane = lax.broadcasted_iota(jnp.int32, (tk, tc), 1)
    ones_rows = (lax.broadcasted_iota(jnp.int32, (MLA_ACC_ROWS - MLA_V, tk), 0) == 0).astype(BF16)

    units = [(h, c) for h in range(hp) for c in range(nc)]
    slot = {u: n for n, u in enumerate(units)}

    def scores(j, buf, u, key_off=None, or_valid=None):
        h, c = u
        start = pl.multiple_of(j * tk, tk)
        s = _dot_nt(k_ref[h, pl.ds(start, tk), :], q_ref[h, c * tc:(c + 1) * tc, :])
        if key_off is not None:
            valid = key + key_off <= lane + c * tc
            if or_valid is not None:
                valid = jnp.logical_or(valid, or_valid)
            s = jnp.where(valid, s, NEG_BIG)
        buf[slot[u]] = s
        return jnp.max(s, axis=0, keepdims=True)

    def accumulate(j, buf, u, m, cm):
        h, c = u
        start = pl.multiple_of(j * tk, tk)
        m_new = jnp.maximum(m, cm)
        alpha = jnp.exp2(m - m_new)
        p = jnp.exp2(buf[slot[u]] - m_new).astype(BF16)
        v1 = jnp.concatenate([vt_ref[h, :, pl.ds(start, tk)], ones_rows], axis=0)
        acc_scr[slot[u]] = alpha * acc_scr[slot[u]] + _dot(v1, p)
        return m_new

    def run(j0, carry, chunks, produce_next):
        m, cm_first = dict(zip(units, carry[0])), dict(zip(units, carry[1]))
        cm = {(0, u): cm_first[u] for u in units}
        todo = [(r, u) for r in range(1, len(chunks)) for u in chunks[r][0]]
        n_acc = len(chunks) - (1 if produce_next else 0)
        order = []

        def issue():
            r, u = todo.pop(0)
            kind = chunks[r][1]
            key_off = {None: None, "a": 0 if u[1] * tc < tk else None, "b": tk}[kind]
            cm[(r, u)] = scores(j0 + r, sb_scr if r % 2 else sa_scr, u, key_off=key_off)
            order.append(("score", r, u))

        for _ in range(min(MLA_LOOKAHEAD, len(todo))):
            issue()
        for r in range(n_acc):
            for u in chunks[r][0]:
                m[u] = accumulate(j0 + r, sb_scr if r % 2 else sa_scr, u, m[u], cm[(r, u)])
                order.append(("acc", r, u))
                if todo:
                    issue()
        for pos, (what, r, u) in enumerate(order):
            if what == "score" and ("acc", r - 2, u) in order:
                assert order.index(("acc", r - 2, u)) < pos
        next_cm = tuple(cm[(len(chunks) - 1, u)] for u in units) if produce_next else None
        return tuple(m[u] for u in units), next_cm

    plain = (units, None)
    late = [u for u in units if (u[1] + 1) * tc > tk]

    def finish(carry, with_pair_before):
        chunks = ([plain, plain] if with_pair_before else []) + [(units, "a"), (late, "b")]
        run(2 * i - (2 if with_pair_before else 0), carry, chunks, False)
        o_t = jnp.concatenate(
            [jnp.concatenate([acc_scr[slot[(h, c)], 0:MLA_V, :] / acc_scr[slot[(h, c)], MLA_V:MLA_V + 1, :]
                              for c in range(nc)], axis=1) for h in range(hp)], axis=0)
        for blk in range(tq // LANES):
            o_ref[blk * LANES:(blk + 1) * LANES, :] = o_t[:, blk * LANES:(blk + 1) * LANES].T.astype(BF16)

    acc_scr[...] = jnp.zeros_like(acc_scr)
    cm_a = tuple(scores(0, sa_scr, u, key_off=0, or_valid=i > 0) for u in units)
    carry = (tuple(jnp.full((1, tc), NEG_BIG, F32) for _ in units), cm_a)
    n_plain = jnp.maximum(i - 1, 0)
    n_long = n_plain // MLA_PAIRS_PER_TRIP
    carry = lax.fori_loop(
        0, n_long,
        lambda t, c: run(2 * MLA_PAIRS_PER_TRIP * t, c, [plain] * (2 * MLA_PAIRS_PER_TRIP + 1), True), carry)
    carry = lax.fori_loop(
        n_long * MLA_PAIRS_PER_TRIP, n_plain, lambda t, c: run(2 * t, c, [plain] * 3, True), carry)

    @pl.when(i > 0)
    def _finish_after_pairs():
        finish(carry, True)

    @pl.when(i == 0)
    def _finish_first_tile():
        finish(carry, False)


def _mla_call(q, k, vt):
    s = q.shape[1]
    tq, hp = MLA_TQ, MLA_HEADS_PER_STEP
    n_units = hp * (tq // MLA_TC)
    return pl.pallas_call(
        _mla_kernel,
        grid=(MLA_HEADS // hp, s // tq),
        in_specs=[
            pl.BlockSpec((hp, tq, HEAD_PAD), lambda h, i: (h, i, 0)),
            pl.BlockSpec((hp, s, HEAD_PAD), lambda h, i: (h, 0, 0)),
            pl.BlockSpec((hp, MLA_V, s), lambda h, i: (h, 0, 0)),
        ],
        out_specs=pl.BlockSpec((tq, hp * MLA_V), lambda h, i: (i, h)),
        out_shape=jax.ShapeDtypeStruct((s, MLA_HEADS * MLA_V), BF16),
        scratch_shapes=[pltpu.VMEM((n_units, MLA_TK, MLA_TC), F32), pltpu.VMEM((n_units, MLA_TK, MLA_TC), F32),
                        pltpu.VMEM((n_units, MLA_ACC_ROWS, MLA_TC), F32)],
        compiler_params=pltpu.CompilerParams(
            dimension_semantics=("arbitrary", "arbitrary"), vmem_limit_bytes=VMEM_LIMIT),
        name="mla_attn",
    )(q, k, vt)


def _swa_jobs(relb_ref, sink_ref, bkt_ref, q_ref, k_ref, kp_ref, ksw_ref, kswp_ref, vt_ref, vtp_ref,
              o_ref, bias_scr):
    i = pl.program_id(0)
    tq, hd, rep = SWA_TQ, SWA_HD, SWA_HEADS // SWA_KV_HEADS
    win = 2 * BLOCK
    assert 2 * hd == LANES and rep == 4 and BLOCK == LANES

    @pl.when(i == 0)
    def _build_bias():
        bkt = bkt_ref[...]
        for hh in range(SWA_HEADS):
            t = jnp.full((win, BLOCK), NEG_BIG, F32)
            for b in range(REL_BUCKETS):
                t = jnp.where(bkt == b, relb_ref[b * SWA_HEADS + hh] * LOG2E, t)
            g, r = divmod(hh, rep)
            bias_scr[2 * g + r % 2, :, (r // 2) * BLOCK:(r // 2 + 1) * BLOCK] = t

    lo_half = lax.broadcasted_iota(jnp.int32, (BLOCK + tq, LANES), 1) < hd
    kfull = jnp.concatenate([kp_ref[...], k_ref[...]], axis=0)
    kswfull = jnp.concatenate([kswp_ref[...], ksw_ref[...]], axis=0)
    zero = jnp.zeros_like(kfull)
    k_lo = (jnp.where(lo_half, kfull, zero), jnp.where(lo_half, kswfull, zero))
    k_hi = (jnp.where(lo_half, zero, kswfull), jnp.where(lo_half, zero, kfull))
    vfull = jnp.concatenate([vtp_ref[...], vt_ref[...]], axis=1)
    ones_rows = (lax.broadcasted_iota(jnp.int32, (BF16_SUBLANES, win), 0) == 0).astype(BF16)
    key_row = lax.broadcasted_iota(jnp.int32, (win, 2 * BLOCK), 0)
    first_valid = jnp.logical_or(key_row >= BLOCK, i > 0)

    sinks = [jnp.concatenate(
        [jnp.full((1, BLOCK), sink_ref[g * rep + half] * LOG2E, F32),
         jnp.full((1, BLOCK), sink_ref[g * rep + 2 + half] * LOG2E, F32)], axis=1)
        for g in range(SWA_KV_HEADS) for half in range(2)]

    tiles = [(b, g, half) for b in range(tq // BLOCK) for g in range(SWA_KV_HEADS) for half in range(2)]

    def tile_scores(b, g, half):
        r0 = b * BLOCK
        qg = jnp.concatenate([q_ref[r0:r0 + BLOCK, (2 * g) * LANES:(2 * g + 1) * LANES],
                              q_ref[r0:r0 + BLOCK, (2 * g + 1) * LANES:(2 * g + 2) * LANES]],
                             axis=0)
        kmat = (k_lo, k_hi)[half][g]
        s = _dot_nt(kmat[r0:r0 + win], qg) + bias_scr[2 * g + half]
        return jnp.where(first_valid, s, NEG_BIG) if b == 0 else s

    def tile_output(b, g, half, s):
        r0 = b * BLOCK
        sink = sinks[2 * g + half]
        v1 = jnp.concatenate([vfull[g * hd:(g + 1) * hd, r0:r0 + win], ones_rows], axis=0)
        m = jnp.maximum(jnp.max(s, axis=0, keepdims=True), sink)
        p = jnp.exp2(s - m).astype(BF16)
        ot = _dot(v1, p)
        den = ot[hd:hd + 1] + jnp.exp2(sink - m)
        return ot[0:hd] / den

    outs = {}

    def tile_finish(b, g, half, s):
        outs[half] = tile_output(b, g, half, s)
        if half == 1:
            r0 = b * BLOCK
            for pr in range(2):
                x = jnp.concatenate([outs[0][:, pr * BLOCK:(pr + 1) * BLOCK],
                                     outs[1][:, pr * BLOCK:(pr + 1) * BLOCK]], axis=0)
                o_ref[r0:r0 + BLOCK, (2 * g + pr) * LANES:(2 * g + pr + 1) * LANES] = x.T.astype(BF16)

    return [(functools.partial(tile_scores, *tile), functools.partial(tile_finish, *tile)) for tile in tiles]


def _run_jobs(jobs, lookahead):
    pending = [score() for score, _ in jobs[:lookahead]]
    for t, (_, finish) in enumerate(jobs):
        s = pending.pop(0)
        if t + lookahead < len(jobs):
            pending.append(jobs[t + lookahead][0]())
        finish(s)


def _memkv_kernel(mem_ref, g_ref, w_ref, k_ref, vt_ref):
    n = MEM_HEADS * MEM_HD
    mn = _rms(mem_ref[...], g_ref[...]).astype(BF16)
    k_ref[...] = _dot(mn, w_ref[:, 0:n]).astype(BF16)
    vt_ref[...] = _dot(mn, w_ref[:, n:2 * n]).T.astype(BF16)


def _memkv_call(layer, mem, g, w):
    n = MEM_HEADS * MEM_HD
    return pl.pallas_call(
        _memkv_kernel,
        grid=(1,),
        in_specs=[_const_spec((MEM_LEN, D_MODEL)), _layer_spec((1, D_MODEL), layer),
                  _const_spec((D_MODEL, 2 * n))],
        out_specs=[_const_spec((MEM_LEN, n)), _const_spec((n, MEM_LEN))],
        out_shape=[jax.ShapeDtypeStruct((MEM_LEN, n), BF16), jax.ShapeDtypeStruct((n, MEM_LEN), BF16)],
        compiler_params=pltpu.CompilerParams(
            dimension_semantics=("arbitrary",), vmem_limit_bytes=VMEM_LIMIT),
        name="mem_kv",
    )(mem, g, w)


def _mem_jobs(q_ref, k_ref, vt_ref, o_ref):
    tq, dm = MEM_TQ, MEM_HD
    ones_rows = (lax.broadcasted_iota(jnp.int32, (BF16_SUBLANES, MEM_LEN), 0) == 0).astype(BF16)

    def head_scores(hh):
        return _dot_nt(k_ref[:, hh * dm:(hh + 1) * dm], q_ref[:, hh * dm:(hh + 1) * dm])

    def head_finish(hh, s):
        p = jnp.exp2(s - jnp.max(s, axis=0, keepdims=True)).astype(BF16)
        v1 = jnp.concatenate([vt_ref[hh * dm:(hh + 1) * dm, :], ones_rows], axis=0)
        ot = _dot(v1, p)
        o = ot[0:dm] / ot[dm:dm + 1]
        for blk in range(tq // LANES):
            o_ref[blk * LANES:(blk + 1) * LANES, hh * dm:(hh + 1) * dm] = (
                o[:, blk * LANES:(blk + 1) * LANES].T.astype(BF16))

    return [(functools.partial(head_scores, hh), functools.partial(head_finish, hh))
            for hh in range(MEM_HEADS)]


def _local_kernel(relb_ref, sink_ref, bkt_ref, q_ref, k_ref, kp_ref, ksw_ref, kswp_ref, vt_ref, vtp_ref,
                  qm_ref, km_ref, vmt_ref, o_ref, om_ref, bias_scr):
    swa = _swa_jobs(relb_ref, sink_ref, bkt_ref, q_ref, k_ref, kp_ref, ksw_ref, kswp_ref, vt_ref, vtp_ref,
                    o_ref, bias_scr)
    mem = _mem_jobs(qm_ref, km_ref, vmt_ref, om_ref)
    every = len(swa) // len(mem)
    jobs = []
    for n, job in enumerate(swa):
        jobs.append(job)
        if n % every == every - 1:
            jobs.append(mem[n // every])
    _run_jobs(jobs, LOCAL_LOOKAHEAD)


def _local_call(rel_bias, sinks, bkt, qs, ks, ksw, vst, qm, km, vmt):
    s = qs.shape[0]
    tq = SWA_TQ
    assert MEM_TQ == tq and MEM_HD == LANES
    nblk = tq // BLOCK
    n_kv = SWA_KV_HEADS * SWA_HD
    n_qs = SWA_HEADS * SWA_HD
    n_qm = MEM_HEADS * MEM_HD
    own = pl.BlockSpec((tq, n_kv), lambda i: (i, 0))
    prev = pl.BlockSpec((BLOCK, n_kv), lambda i: (jnp.maximum(i * nblk - 1, 0), 0))
    own_t = pl.BlockSpec((n_kv, tq), lambda i: (0, i))
    prev_t = pl.BlockSpec((n_kv, BLOCK), lambda i: (0, jnp.maximum(i * nblk - 1, 0)))
    smem = pl.BlockSpec(memory_space=pltpu.SMEM)
    return pl.pallas_call(
        _local_kernel,
        grid=(s // tq,),
        in_specs=[
            smem, smem, _const_spec((2 * BLOCK, BLOCK)),
            pl.BlockSpec((tq, n_qs), lambda i: (i, 0)),
            own, prev, own, prev, own_t, prev_t,
            pl.BlockSpec((tq, n_qm), lambda i: (i, 0)), _const_spec((MEM_LEN, n_qm)),
            _const_spec((n_qm, MEM_LEN)),
        ],
        out_specs=[pl.BlockSpec((tq, n_qs), lambda i: (i, 0)), pl.BlockSpec((tq, n_qm), lambda i: (i, 0))],
        out_shape=[jax.ShapeDtypeStruct((s, n_qs), BF16), jax.ShapeDtypeStruct((s, n_qm), BF16)],
        scratch_shapes=[pltpu.VMEM((2 * SWA_KV_HEADS, 2 * BLOCK, 2 * BLOCK), F32)],
        compiler_params=pltpu.CompilerParams(
            dimension_semantics=("arbitrary",), vmem_limit_bytes=VMEM_LIMIT),
        name="local_attn",
    )(rel_bias.reshape(-1), sinks, bkt, qs, ks, ks, ksw, ksw, vst, vst, qm, km, vmt)


def _post_kernel(x_ref, oa_ref, ob_ref, oc_ref, gate_ref, wa_ref, wb_ref, wc_ref, wout_ref,
                 gn_ref, wup_ref, wdn_ref, fn_ref, o_ref, *, final):
    d = D_MODEL
    y = (gate_ref[:, 0:d].astype(F32) * _dot(oa_ref[...], wa_ref[...])
         + gate_ref[:, d:2 * d].astype(F32) * _dot(ob_ref[...], wb_ref[...])
         + gate_ref[:, 2 * d:3 * d].astype(F32) * _dot(oc_ref[...], wc_ref[...]))
    x1 = x_ref[...] + _dot(y.astype(BF16), wout_ref[...])
    h = _rms(x1, gn_ref[...]).astype(BF16)
    acc = x1
    for c in range(D_FF // FF_CHUNK):
        lo = c * FF_CHUNK
        u = jnp.maximum(_dot(h, wup_ref[:, lo:lo + FF_CHUNK]), 0.0)
        acc = acc + _dot((u * u).astype(BF16), wdn_ref[lo:lo + FF_CHUNK, :])
    if final:
        acc = _rms(acc, fn_ref[...])
    o_ref[...] = acc


def _post_call(layer, x, oa, ob, oc, gates, wa, wb, wc, wout, gn, wup, wdn, fn, final):
    s = x.shape[0]
    tm = POST_TM
    d = D_MODEL

    def row_spec(n):
        return pl.BlockSpec((tm, n), lambda i: (i, 0))

    return pl.pallas_call(
        functools.partial(_post_kernel, final=final),
        grid=(s // tm,),
        in_specs=[
            row_spec(d), row_spec(oa.shape[1]), row_spec(ob.shape[1]), row_spec(oc.shape[1]),
            row_spec(N_BRANCH * d),
            _const_spec(wa.shape), _const_spec(wb.shape), _const_spec(wc.shape), _const_spec((d, d)),
            _layer_spec((1, d), layer), _const_spec((d, D_FF)), _const_spec((D_FF, d)),
            _const_spec((1, d)),
        ],
        out_specs=row_spec(d),
        out_shape=jax.ShapeDtypeStruct((s, d), F32),
        compiler_params=pltpu.CompilerParams(
            dimension_semantics=("arbitrary",), vmem_limit_bytes=VMEM_LIMIT),
        name="post",
    )(x, oa, ob, oc, gates, wa, wb, wc, wout, gn, wup, wdn, fn)


def _rot_cols(w):
    half = MLA_ROPE // 2
    return jnp.concatenate([-w[..., half:], w[..., :half]], axis=-1)


def _pad_cols(w, left, total):
    return jnp.pad(w, [(0, 0)] * (w.ndim - 1) + [(left, total - left - w.shape[-1])])


def _stacked_weights(w_in, w_uq, w_ukv):
    depth = w_in.shape[0]
    n_kv = SWA_KV_HEADS * SWA_HD
    row_scale = jnp.where(jnp.arange(_W_END) >= _W_GATE, 0.5, 1.0).astype(F32)
    w_t = (jnp.swapaxes(w_in, 1, 2) * row_scale[None, :, None]).astype(BF16)
    kpe = w_in[:, :, _H_KPE:_W_QS]
    ks_swapped = jnp.concatenate([w_in[:, :, _W_KS + SWA_HD:_W_KS + n_kv],
                                  w_in[:, :, _W_KS:_W_KS + SWA_HD]], axis=-1)
    w_aux_t = jnp.swapaxes(
        jnp.concatenate([_pad_cols(_rot_cols(kpe), MLA_NOPE, HEAD_PAD), ks_swapped], axis=-1),
        1, 2).astype(BF16)

    uq = w_uq.reshape(depth, MLA_Q_LORA, MLA_HEADS, MLA_NOPE + MLA_ROPE)
    uq_plain = _pad_cols(uq, 0, HEAD_PAD)
    uq_rot = _pad_cols(_rot_cols(uq[..., MLA_NOPE:]), MLA_NOPE, HEAD_PAD)
    wuq = jnp.concatenate([uq_plain.reshape(depth, MLA_Q_LORA, -1), uq_rot.reshape(depth, MLA_Q_LORA, -1)],
                          axis=-1).astype(BF16)

    ukv = w_ukv.reshape(depth, MLA_KV_LORA, MLA_HEADS, MLA_NOPE + MLA_V)
    wk = _pad_cols(ukv[..., :MLA_NOPE], 0, HEAD_PAD).reshape(depth, MLA_KV_LORA, -1).astype(BF16)
    wv = ukv[..., MLA_NOPE:].reshape(depth, MLA_KV_LORA, -1).astype(BF16)
    return w_t, w_aux_t, wuq, wk, wv


def _rope_table(seq):
    pos = jnp.arange(seq, dtype=F32)
    inv = 1.0 / (ROPE_THETA ** (jnp.arange(0, MLA_ROPE, 2, dtype=F32) / MLA_ROPE))
    ang = pos[:, None] * inv[None, :]
    cos = jnp.concatenate([jnp.cos(ang)] * 2, axis=-1)
    sin = jnp.concatenate([jnp.sin(ang)] * 2, axis=-1)
    return jnp.concatenate([_pad_cols(cos, MLA_NOPE, HEAD_PAD), _pad_cols(sin, MLA_NOPE, HEAD_PAD)],
                           axis=-1)


def _t5_bucket(dist):
    n = jnp.maximum(dist, 0)
    max_exact = REL_BUCKETS // 2
    nf = jnp.maximum(n, 1).astype(F32)
    large = max_exact + (jnp.log(nf / max_exact) / math.log(REL_MAX_DIST / max_exact)
                         * (REL_BUCKETS - max_exact)).astype(jnp.int32)
    large = jnp.minimum(large, REL_BUCKETS - 1)
    return jnp.where(n < max_exact, n, large)


def _swa_bucket_table():
    kj = jnp.arange(2 * BLOCK)[:, None]
    qi = jnp.arange(BLOCK)[None, :]
    dist = qi + BLOCK - kj
    band = (dist >= 0) & (dist < WINDOW)
    return jnp.where(band, _t5_bucket(dist), -1).astype(jnp.int32)


def kernel(x, mem, rel_bias, attn_norm, mem_norm, w_in, b_gate, mla_q_norm, w_uq, mla_kv_norm,
           w_ukv, attn_sinks, w_mem_kv, w_o_mla, w_o_swa, w_o_mem, w_out, mlp_norm, w_up, w_down,
           final_norm):
    batch, seq, d = x.shape
    assert batch == 1 and d == D_MODEL and mem.shape == (1, MEM_LEN, D_MODEL)
    depth = w_in.shape[0]
    xs = x[0]
    mem2 = mem[0]
    tab = _rope_table(seq)
    bkt = _swa_bucket_table()
    rel_bias = rel_bias.astype(F32)

    def rows(p):
        return p.reshape(depth, 1, -1)

    w_t, w_aux_t, wuq, wk, wv = _stacked_weights(w_in, w_uq, w_ukv)
    later_weights = (w_o_mla, w_o_swa, w_o_mem, w_out, w_up, w_down, w_mem_kv)
    attn_norm, mem_norm, mla_q_norm, mla_kv_norm, mlp_norm, b_gate = (
        rows(p) for p in (attn_norm, mem_norm, mla_q_norm, mla_kv_norm, mlp_norm, b_gate))
    half_b_gate = 0.5 * b_gate
    fn = final_norm.reshape(1, d)

    for l in range(depth):
        q, k, vt, qs, ks, ksw, vst, qm, gates, wa, wb, wc, wout, wup, wdn, wmem = _proj_call(
            l, xs, attn_norm, w_t, w_aux_t, mla_q_norm, wuq, mla_kv_norm, wk, wv, half_b_gate, tab,
            later_weights)
        o_mla = _mla_call(q, k, vt)
        km, vmt = _memkv_call(l, mem2, mem_norm, wmem)
        o_swa, o_mem = _local_call(rel_bias, attn_sinks[l], bkt, qs, ks, ksw, vst, qm, km, vmt)
        xs = _post_call(l, xs, o_mla, o_swa, o_mem, gates, wa, wb, wc, wout, mlp_norm, wup, wdn, fn,
                        final=(l == depth - 1))
    return xs[None]
```

```python
import functools
import math

import jax
import jax.numpy as jnp
from jax import lax
from jax.experimental import pallas as pl
from jax.experimental.pallas import tpu as pltpu

F32 = jnp.float32
BF16 = jnp.bfloat16

D_MODEL = 1024
MLA_HEADS = 8
MLA_Q_LORA = 256
MLA_KV_LORA = 128
MLA_NOPE = 64
MLA_ROPE = 32
MLA_V = 64
ROPE_THETA = 10000.0
SWA_HEADS = 8
SWA_KV_HEADS = 2
SWA_HD = 64
WINDOW = 128
BLOCK = 128
REL_BUCKETS = 32
REL_MAX_DIST = 128
MEM_LEN = 256
MEM_HEADS = 4
MEM_HD = 128
N_BRANCH = 3
D_FF = 4 * D_MODEL
EPS = 1e-6

HEAD_PAD = 128
BF16_SUBLANES = 16
MLA_ACC_ROWS = MLA_V + BF16_SUBLANES
NEG_BIG = -1e30
LOG2E = math.log2(math.e)
LANES = 128

PROJ_TM = 512
MLA_TQ = 2048
MLA_TK = 512
MLA_TC = 256
MLA_HEADS_PER_STEP = 2
MLA_PAIRS_PER_TRIP = 2
MLA_LOOKAHEAD = 3
SWA_TQ = 512
LOCAL_LOOKAHEAD = 5
MEM_TQ = 512
POST_TM = 512
FF_CHUNK = 1024
VMEM_LIMIT = 56 * 1024 * 1024
N_LATER_WEIGHTS = 7

_H_CQ = 0
_H_CKV = _H_CQ + MLA_Q_LORA
_H_KPE = _H_CKV + MLA_KV_LORA
_W_CQ = _H_CQ
_W_QS = _H_KPE + MLA_ROPE
_W_KS = _W_QS + SWA_HEADS * SWA_HD
_W_GATE = _W_KS + 2 * SWA_KV_HEADS * SWA_HD + MEM_HEADS * MEM_HD
_W_END = _W_GATE + N_BRANCH * D_MODEL
_MLA_IN_COLS = 512
_R_QS = 0
_R_KS = _R_QS + SWA_HEADS * SWA_HD
_R_VS = _R_KS + SWA_KV_HEADS * SWA_HD
_R_QM = _R_VS + SWA_KV_HEADS * SWA_HD
_R_GATE = _R_QM + MEM_HEADS * MEM_HD
assert _W_QS % BF16_SUBLANES == 0 and _W_GATE == _W_QS + _R_GATE and _H_KPE + LANES <= _MLA_IN_COLS


def _rms(x, g):
    return x * lax.rsqrt(jnp.mean(x * x, axis=-1, keepdims=True) + EPS) * g


def _const_spec(shape):
    zeros = (0,) * len(shape)
    return pl.BlockSpec(shape, lambda *_: zeros, pipeline_mode=pl.Buffered(1))


def _layer_spec(shape, layer):
    index = (layer,) + (0,) * len(shape)
    return pl.BlockSpec((None,) + tuple(shape), lambda *_: index, pipeline_mode=pl.Buffered(1))


def _dot(a, b):
    return jnp.dot(a, b, preferred_element_type=F32)


def _dot_nt(a, b):
    return lax.dot_general(a, b, (((1,), (1,)), ((), ())), preferred_element_type=F32)


def _proj_kernel(x_ref, g_ref, wt_ref, wa_ref, qn_ref, wuq_ref, kvn_ref, wk_ref, wv_ref, bg_ref,
                 tab_ref, *refs):
    later_f32 = refs[:N_LATER_WEIGHTS]
    q_ref, k_ref, vt_ref, qs_ref, ks_ref, ksw_ref, vst_ref, qm_ref, gate_ref = refs[N_LATER_WEIGHTS:-N_LATER_WEIGHTS]
    later_bf16 = refs[-N_LATER_WEIGHTS:]
    for src, dst in zip(later_f32, later_bf16):
        dst[...] = src[...].astype(BF16)

    h = _rms(x_ref[...], g_ref[...]).astype(BF16)

    k_cos, k_sin = tab_ref[:, 0:HEAD_PAD], tab_ref[:, HEAD_PAD:2 * HEAD_PAD]
    q_scale = (MLA_NOPE + MLA_ROPE) ** -0.5 * LOG2E
    nope = (lax.broadcasted_iota(jnp.int32, k_cos.shape, 1) < MLA_NOPE).astype(F32)
    q_cos, q_sin = (k_cos + nope) * q_scale, k_sin * q_scale
    hw = MLA_HEADS * HEAD_PAD

    mla_in = _dot_nt(h, wt_ref[_W_CQ:_W_CQ + _MLA_IN_COLS, :])
    aux_in = _dot_nt(h, wa_ref[...])
    swa_in = _dot_nt(h, wt_ref[_W_QS:_W_GATE, :])

    cq = _rms(mla_in[:, _H_CQ:_H_CKV], qn_ref[...]).astype(BF16)
    qq = _dot(cq, wuq_ref[...])
    for hh in range(MLA_HEADS):
        lo = hh * HEAD_PAD
        q_ref[hh] = (qq[:, lo:lo + HEAD_PAD] * q_cos
                     + qq[:, hw + lo:hw + lo + HEAD_PAD] * q_sin).astype(BF16)

    ckv = _rms(mla_in[:, _H_CKV:_H_KPE], kvn_ref[...]).astype(BF16)
    kn = _dot(ckv, wk_ref[...])
    kpe = (pltpu.roll(mla_in[:, _H_KPE:_H_KPE + LANES], MLA_NOPE, axis=1) * k_cos
           + aux_in[:, 0:HEAD_PAD] * k_sin)
    for hh in range(MLA_HEADS):
        lo = hh * HEAD_PAD
        k_ref[hh] = (kn[:, lo:lo + HEAD_PAD] + kpe).astype(BF16)
    vt = _dot(ckv, wv_ref[...]).T
    for hh in range(MLA_HEADS):
        vt_ref[hh] = vt[hh * MLA_V:(hh + 1) * MLA_V, :].astype(BF16)

    qs_ref[...] = (swa_in[:, _R_QS:_R_KS] * (SWA_HD ** -0.5 * LOG2E)).astype(BF16)
    ks_ref[...] = swa_in[:, _R_KS:_R_VS].astype(BF16)
    ksw_ref[...] = aux_in[:, HEAD_PAD:].astype(BF16)
    vst_ref[...] = swa_in[:, _R_VS:_R_QM].T.astype(BF16)
    qm_ref[...] = (swa_in[:, _R_QM:_R_GATE] * (MEM_HD ** -0.5 * LOG2E)).astype(BF16)
    for c in range(N_BRANCH):
        lo = _W_GATE + c * D_MODEL
        half_z = _dot_nt(h, wt_ref[lo:lo + D_MODEL, :]) + bg_ref[:, c * D_MODEL:(c + 1) * D_MODEL]
        gate_ref[:, c * D_MODEL:(c + 1) * D_MODEL] = (0.5 * jnp.tanh(half_z) + 0.5).astype(BF16)


def _proj_call(layer, x, g, wt, wa, qn, wuq, kvn, wk, wv, bg, tab, later_weights):
    s = x.shape[0]
    tm = PROJ_TM
    steps = s // tm
    hw = MLA_HEADS * HEAD_PAD
    assert len(later_weights) == N_LATER_WEIGHTS
    assert all(w.shape[1] % (steps * BF16_SUBLANES) == 0 for w in later_weights)

    def slab_in(w):
        return pl.BlockSpec((None, w.shape[1] // steps, w.shape[2]), lambda i: (layer, i, 0))

    def slab_out(w):
        return pl.BlockSpec((w.shape[1] // steps, w.shape[2]), lambda i: (i, 0))

    head_spec = pl.BlockSpec((MLA_HEADS, tm, HEAD_PAD), lambda i: (0, i, 0))
    head_shape = jax.ShapeDtypeStruct((MLA_HEADS, s, HEAD_PAD), BF16)
    vt_spec = pl.BlockSpec((MLA_HEADS, MLA_V, tm), lambda i: (0, 0, i))
    vt_shape = jax.ShapeDtypeStruct((MLA_HEADS, MLA_V, s), BF16)

    def row_spec(n):
        return pl.BlockSpec((tm, n), lambda i: (i, 0))

    def row_shape(n):
        return jax.ShapeDtypeStruct((s, n), BF16)

    n_qs, n_kv, n_qm, n_g = SWA_HEADS * SWA_HD, SWA_KV_HEADS * SWA_HD, MEM_HEADS * MEM_HD, N_BRANCH * D_MODEL
    return pl.pallas_call(
        _proj_kernel,
        grid=(s // tm,),
        in_specs=[
            row_spec(D_MODEL),
            _layer_spec((1, D_MODEL), layer),
            _layer_spec((_W_END, D_MODEL), layer),
            _layer_spec((HEAD_PAD + n_kv, D_MODEL), layer),
            _layer_spec((1, MLA_Q_LORA), layer),
            _layer_spec((MLA_Q_LORA, 2 * hw), layer),
            _layer_spec((1, MLA_KV_LORA), layer),
            _layer_spec((MLA_KV_LORA, hw), layer),
            _layer_spec((MLA_KV_LORA, MLA_HEADS * MLA_V), layer),
            _layer_spec((1, n_g), layer),
            row_spec(2 * HEAD_PAD),
        ] + [slab_in(w) for w in later_weights],
        out_specs=[head_spec, head_spec, vt_spec,
                   row_spec(n_qs), row_spec(n_kv), row_spec(n_kv),
                   pl.BlockSpec((n_kv, tm), lambda i: (0, i)), row_spec(n_qm), row_spec(n_g)]
        + [slab_out(w) for w in later_weights],
        out_shape=[head_shape, head_shape, vt_shape,
                   row_shape(n_qs), row_shape(n_kv), row_shape(n_kv),
                   jax.ShapeDtypeStruct((n_kv, s), BF16), row_shape(n_qm), row_shape(n_g)]
        + [jax.ShapeDtypeStruct(w.shape[1:], BF16) for w in later_weights],
        compiler_params=pltpu.CompilerParams(
            dimension_semantics=("arbitrary",), vmem_limit_bytes=VMEM_LIMIT),
        name="proj",
    )(x, g, wt, wa, qn, wuq, kvn, wk, wv, bg, tab, *later_weights)


def _mla_kernel(q_ref, k_ref, vt_ref, o_ref, sa_scr, sb_scr, acc_scr):
    i = pl.program_id(1)
    tq, tk, tc, hp = MLA_TQ, MLA_TK, MLA_TC, MLA_HEADS_PER_STEP
    nc = tq // tc
    nd = tq // tk
    assert tq % tk == 0 and nd % 2 == 0 and tk % tc == 0
    assert hp * MLA_V == LANES
    key = lax.broadcasted_iota(jnp.int32, (tk, tc), 0)
    lane = lax.broadcasted_iota(jnp.int32, (tk, tc), 1)
    ones_rows = (lax.broadcasted_iota(jnp.int32, (MLA_ACC_ROWS - MLA_V, tk), 0) == 0).astype(BF16)

    units = [(h, c) for h in range(hp) for c in range(nc)]
    slot = {u: n for n, u in enumerate(units)}

    def scores(j, buf, u, key_off=None, or_valid=None):
        h, c = u
        start = pl.multiple_of(j * tk, tk)
        s = _dot_nt(k_ref[h, pl.ds(start, tk), :], q_ref[h, c * tc:(c + 1) * tc, :])
        if key_off is not None:
            valid = key + key_off <= lane + c * tc
            if or_valid is not None:
                valid = jnp.logical_or(valid, or_valid)
            s = jnp.where(valid, s, NEG_BIG)
        buf[slot[u]] = s
        return jnp.max(s, axis=0, keepdims=True)

    def accumulate(j, buf, u, m, cm):
        h, c = u
        start = pl.multiple_of(j * tk, tk)
        m_new = jnp.maximum(m, cm)
        alpha = jnp.exp2(m - m_new)
        p = jnp.exp2(buf[slot[u]] - m_new).astype(BF16)
        v1 = jnp.concatenate([vt_ref[h, :, pl.ds(start, tk)], ones_rows], axis=0)
        acc_scr[slot[u]] = alpha * acc_scr[slot[u]] + _dot(v1, p)
        return m_new

    def run(j0, carry, chunks, produce_next):
        m, cm_first = dict(zip(units, carry[0])), dict(zip(units, carry[1]))
        cm = {(0, u): cm_first[u] for u in units}
        todo = [(r, u) for r in range(1, len(chunks)) for u in chunks[r][0]]
        n_acc = len(chunks) - (1 if produce_next else 0)
        order = []

        def issue():
            r, u = todo.pop(0)
            d = chunks[r][1]
            masked = d is not None and (d + 1) * tk - 1 > u[1] * tc
            cm[(r, u)] = scores(j0 + r, sb_scr if r % 2 else sa_scr, u, key_off=d * tk if masked else None)
            order.append(("score", r, u))

        to_acc = [(r, u) for r in range(n_acc) for u in chunks[r][0]]

        def pump():
            while todo and sum(o[0] == "score" for o in order) - sum(o[0] == "acc" for o in order) < MLA_LOOKAHEAD:
                r, u = todo[0]
                if (r - 2, u) in to_acc and ("acc", r - 2, u) not in order:
                    break
                issue()

        pump()
        for r, u in to_acc:
            m[u] = accumulate(j0 + r, sb_scr if r % 2 else sa_scr, u, m[u], cm[(r, u)])
            order.append(("acc", r, u))
            pump()
        assert not todo
        for pos, (what, r, u) in enumerate(order):
            if what == "score" and ("acc", r - 2, u) in order:
                assert order.index(("acc", r - 2, u)) < pos
        next_cm = tuple(cm[(len(chunks) - 1, u)] for u in units) if produce_next else None
        return tuple(m[u] for u in units), next_cm

    plain = (units, None)
    diagonal = [([u for u in units if (u[1] + 1) * tc > d * tk], d) for d in range(nd)]

    def finish(carry, with_pair_before):
        chunks = ([plain, plain] if with_pair_before else []) + diagonal
        run(nd * i - (2 if with_pair_before else 0), carry, chunks, False)
        o_t = jnp.concatenate(
            [jnp.concatenate([acc_scr[slot[(h, c)], 0:MLA_V, :] / acc_scr[slot[(h, c)], MLA_V:MLA_V + 1, :]
                              for c in range(nc)], axis=1) for h in range(hp)], axis=0)
        for blk in range(tq // LANES):
            o_ref[blk * LANES:(blk + 1) * LANES, :] = o_t[:, blk * LANES:(blk + 1) * LANES].T.astype(BF16)

    acc_scr[...] = jnp.zeros_like(acc_scr)
    cm_a = tuple(scores(0, sa_scr, u, key_off=0, or_valid=i > 0) for u in units)
    carry = (tuple(jnp.full((1, tc), NEG_BIG, F32) for _ in units), cm_a)
    n_plain = jnp.maximum((nd // 2) * i - 1, 0)
    n_long = n_plain // MLA_PAIRS_PER_TRIP
    carry = lax.fori_loop(
        0, n_long,
        lambda t, c: run(2 * MLA_PAIRS_PER_TRIP * t, c, [plain] * (2 * MLA_PAIRS_PER_TRIP + 1), True), carry)
    carry = lax.fori_loop(
        n_long * MLA_PAIRS_PER_TRIP, n_plain, lambda t, c: run(2 * t, c, [plain] * 3, True), carry)

    @pl.when(i > 0)
    def _finish_after_pairs():
        finish(carry, True)

    @pl.when(i == 0)
    def _finish_first_tile():
        finish(carry, False)


def _mla_call(q, k, vt):
    s = q.shape[1]
    tq, hp = MLA_TQ, MLA_HEADS_PER_STEP
    n_units = hp * (tq // MLA_TC)
    return pl.pallas_call(
        _mla_kernel,
        grid=(MLA_HEADS // hp, s // tq),
        in_specs=[
            pl.BlockSpec((hp, tq, HEAD_PAD), lambda h, i: (h, i, 0)),
            pl.BlockSpec((hp, s, HEAD_PAD), lambda h, i: (h, 0, 0)),
            pl.BlockSpec((hp, MLA_V, s), lambda h, i: (h, 0, 0)),
        ],
        out_specs=pl.BlockSpec((tq, hp * MLA_V), lambda h, i: (i, h)),
        out_shape=jax.ShapeDtypeStruct((s, MLA_HEADS * MLA_V), BF16),
        scratch_shapes=[pltpu.VMEM((n_units, MLA_TK, MLA_TC), F32), pltpu.VMEM((n_units, MLA_TK, MLA_TC), F32),
                        pltpu.VMEM((n_units, MLA_ACC_ROWS, MLA_TC), F32)],
        compiler_params=pltpu.CompilerParams(
            dimension_semantics=("arbitrary", "arbitrary"), vmem_limit_bytes=VMEM_LIMIT),
        name="mla_attn",
    )(q, k, vt)


def _swa_jobs(relb_ref, sink_ref, bkt_ref, q_ref, k_ref, kp_ref, ksw_ref, kswp_ref, vt_ref, vtp_ref,
              o_ref, bias_scr):
    i = pl.program_id(0)
    tq, hd, rep = SWA_TQ, SWA_HD, SWA_HEADS // SWA_KV_HEADS
    win = 2 * BLOCK
    assert 2 * hd == LANES and rep == 4 and BLOCK == LANES

    @pl.when(i == 0)
    def _build_bias():
        bkt = bkt_ref[...]
        for hh in range(SWA_HEADS):
            t = jnp.full((win, BLOCK), NEG_BIG, F32)
            for b in range(REL_BUCKETS):
                t = jnp.where(bkt == b, relb_ref[b * SWA_HEADS + hh] * LOG2E, t)
            g, r = divmod(hh, rep)
            bias_scr[2 * g + r % 2, :, (r // 2) * BLOCK:(r // 2 + 1) * BLOCK] = t

    lo_half = lax.broadcasted_iota(jnp.int32, (BLOCK + tq, LANES), 1) < hd
    kfull = jnp.concatenate([kp_ref[...], k_ref[...]], axis=0)
    kswfull = jnp.concatenate([kswp_ref[...], ksw_ref[...]], axis=0)
    zero = jnp.zeros_like(kfull)
    k_lo = (jnp.where(lo_half, kfull, zero), jnp.where(lo_half, kswfull, zero))
    k_hi = (jnp.where(lo_half, zero, kswfull), jnp.where(lo_half, zero, kfull))
    vfull = jnp.concatenate([vtp_ref[...], vt_ref[...]], axis=1)
    ones_rows = (lax.broadcasted_iota(jnp.int32, (BF16_SUBLANES, win), 0) == 0).astype(BF16)
    key_row = lax.broadcasted_iota(jnp.int32, (win, 2 * BLOCK), 0)
    first_valid = jnp.logical_or(key_row >= BLOCK, i > 0)

    sinks = [jnp.concatenate(
        [jnp.full((1, BLOCK), sink_ref[g * rep + half] * LOG2E, F32),
         jnp.full((1, BLOCK), sink_ref[g * rep + 2 + half] * LOG2E, F32)], axis=1)
        for g in range(SWA_KV_HEADS) for half in range(2)]

    tiles = [(b, g, half) for b in range(tq // BLOCK) for g in range(SWA_KV_HEADS) for half in range(2)]

    def tile_scores(b, g, half):
        r0 = b * BLOCK
        qg = jnp.concatenate([q_ref[r0:r0 + BLOCK, (2 * g) * LANES:(2 * g + 1) * LANES],
                              q_ref[r0:r0 + BLOCK, (2 * g + 1) * LANES:(2 * g + 2) * LANES]],
                             axis=0)
        kmat = (k_lo, k_hi)[half][g]
        s = _dot_nt(kmat[r0:r0 + win], qg) + bias_scr[2 * g + half]
        return jnp.where(first_valid, s, NEG_BIG) if b == 0 else s

    def tile_output(b, g, half, s):
        r0 = b * BLOCK
        sink = sinks[2 * g + half]
        v1 = jnp.concatenate([vfull[g * hd:(g + 1) * hd, r0:r0 + win], ones_rows], axis=0)
        m = jnp.maximum(jnp.max(s, axis=0, keepdims=True), sink)
        p = jnp.exp2(s - m).astype(BF16)
        ot = _dot(v1, p)
        den = ot[hd:hd + 1] + jnp.exp2(sink - m)
        return ot[0:hd] / den

    outs = {}

    def tile_finish(b, g, half, s):
        outs[half] = tile_output(b, g, half, s)
        if half == 1:
            r0 = b * BLOCK
            for pr in range(2):
                x = jnp.concatenate([outs[0][:, pr * BLOCK:(pr + 1) * BLOCK],
                                     outs[1][:, pr * BLOCK:(pr + 1) * BLOCK]], axis=0)
                o_ref[r0:r0 + BLOCK, (2 * g + pr) * LANES:(2 * g + pr + 1) * LANES] = x.T.astype(BF16)

    return [(functools.partial(tile_scores, *tile), functools.partial(tile_finish, *tile)) for tile in tiles]


def _run_jobs(jobs, lookahead):
    pending = [score() for score, _ in jobs[:lookahead]]
    for t, (_, finish) in enumerate(jobs):
        s = pending.pop(0)
        if t + lookahead < len(jobs):
            pending.append(jobs[t + lookahead][0]())
        finish(s)


def _memkv_kernel(mem_ref, g_ref, w_ref, k_ref, vt_ref):
    n = MEM_HEADS * MEM_HD
    mn = _rms(mem_ref[...], g_ref[...]).astype(BF16)
    k_ref[...] = _dot(mn, w_ref[:, 0:n]).astype(BF16)
    vt_ref[...] = _dot(mn, w_ref[:, n:2 * n]).T.astype(BF16)


def _memkv_call(layer, mem, g, w):
    n = MEM_HEADS * MEM_HD
    return pl.pallas_call(
        _memkv_kernel,
        grid=(1,),
        in_specs=[_const_spec((MEM_LEN, D_MODEL)), _layer_spec((1, D_MODEL), layer),
                  _const_spec((D_MODEL, 2 * n))],
        out_specs=[_const_spec((MEM_LEN, n)), _const_spec((n, MEM_LEN))],
        out_shape=[jax.ShapeDtypeStruct((MEM_LEN, n), BF16), jax.ShapeDtypeStruct((n, MEM_LEN), BF16)],
        compiler_params=pltpu.CompilerParams(
            dimension_semantics=("arbitrary",), vmem_limit_bytes=VMEM_LIMIT),
        name="mem_kv",
    )(mem, g, w)


def _mem_jobs(q_ref, k_ref, vt_ref, o_ref):
    tq, dm = MEM_TQ, MEM_HD
    ones_rows = (lax.broadcasted_iota(jnp.int32, (BF16_SUBLANES, MEM_LEN), 0) == 0).astype(BF16)

    def head_scores(hh):
        return _dot_nt(k_ref[:, hh * dm:(hh + 1) * dm], q_ref[:, hh * dm:(hh + 1) * dm])

    def head_finish(hh, s):
        p = jnp.exp2(s - jnp.max(s, axis=0, keepdims=True)).astype(BF16)
        v1 = jnp.concatenate([vt_ref[hh * dm:(hh + 1) * dm, :], ones_rows], axis=0)
        ot = _dot(v1, p)
        o = ot[0:dm] / ot[dm:dm + 1]
        for blk in range(tq // LANES):
            o_ref[blk * LANES:(blk + 1) * LANES, hh * dm:(hh + 1) * dm] = (
                o[:, blk * LANES:(blk + 1) * LANES].T.astype(BF16))

    return [(functools.partial(head_scores, hh), functools.partial(head_finish, hh))
            for hh in range(MEM_HEADS)]


def _local_kernel(relb_ref, sink_ref, bkt_ref, q_ref, k_ref, kp_ref, ksw_ref, kswp_ref, vt_ref, vtp_ref,
                  qm_ref, km_ref, vmt_ref, o_ref, om_ref, bias_scr):
    swa = _swa_jobs(relb_ref, sink_ref, bkt_ref, q_ref, k_ref, kp_ref, ksw_ref, kswp_ref, vt_ref, vtp_ref,
                    o_ref, bias_scr)
    mem = _mem_jobs(qm_ref, km_ref, vmt_ref, om_ref)
    every = len(swa) // len(mem)
    jobs = []
    for n, job in enumerate(swa):
        jobs.append(job)
        if n % every == every - 1:
            jobs.append(mem[n // every])
    _run_jobs(jobs, LOCAL_LOOKAHEAD)


def _local_call(rel_bias, sinks, bkt, qs, ks, ksw, vst, qm, km, vmt):
    s = qs.shape[0]
    tq = SWA_TQ
    assert MEM_TQ == tq and MEM_HD == LANES
    nblk = tq // BLOCK
    n_kv = SWA_KV_HEADS * SWA_HD
    n_qs = SWA_HEADS * SWA_HD
    n_qm = MEM_HEADS * MEM_HD
    own = pl.BlockSpec((tq, n_kv), lambda i: (i, 0))
    prev = pl.BlockSpec((BLOCK, n_kv), lambda i: (jnp.maximum(i * nblk - 1, 0), 0))
    own_t = pl.BlockSpec((n_kv, tq), lambda i: (0, i))
    prev_t = pl.BlockSpec((n_kv, BLOCK), lambda i: (0, jnp.maximum(i * nblk - 1, 0)))
    smem = pl.BlockSpec(memory_space=pltpu.SMEM)
    return pl.pallas_call(
        _local_kernel,
        grid=(s // tq,),
        in_specs=[
            smem, smem, _const_spec((2 * BLOCK, BLOCK)),
            pl.BlockSpec((tq, n_qs), lambda i: (i, 0)),
            own, prev, own, prev, own_t, prev_t,
            pl.BlockSpec((tq, n_qm), lambda i: (i, 0)), _const_spec((MEM_LEN, n_qm)),
            _const_spec((n_qm, MEM_LEN)),
        ],
        out_specs=[pl.BlockSpec((tq, n_qs), lambda i: (i, 0)), pl.BlockSpec((tq, n_qm), lambda i: (i, 0))],
        out_shape=[jax.ShapeDtypeStruct((s, n_qs), BF16), jax.ShapeDtypeStruct((s, n_qm), BF16)],
        scratch_shapes=[pltpu.VMEM((2 * SWA_KV_HEADS, 2 * BLOCK, 2 * BLOCK), F32)],
        compiler_params=pltpu.CompilerParams(
            dimension_semantics=("arbitrary",), vmem_limit_bytes=VMEM_LIMIT),
        name="local_attn",
    )(rel_bias.reshape(-1), sinks, bkt, qs, ks, ks, ksw, ksw, vst, vst, qm, km, vmt)


def _post_kernel(x_ref, oa_ref, ob_ref, oc_ref, gate_ref, wa_ref, wb_ref, wc_ref, wout_ref,
                 gn_ref, wup_ref, wdn_ref, fn_ref, o_ref, *, final):
    d = D_MODEL
    y = (gate_ref[:, 0:d].astype(F32) * _dot(oa_ref[...], wa_ref[...])
         + gate_ref[:, d:2 * d].astype(F32) * _dot(ob_ref[...], wb_ref[...])
         + gate_ref[:, 2 * d:3 * d].astype(F32) * _dot(oc_ref[...], wc_ref[...]))
    x1 = x_ref[...] + _dot(y.astype(BF16), wout_ref[...])
    h = _rms(x1, gn_ref[...]).astype(BF16)
    acc = x1
    for c in range(D_FF // FF_CHUNK):
        lo = c * FF_CHUNK
        u = jnp.maximum(_dot(h, wup_ref[:, lo:lo + FF_CHUNK]), 0.0)
        acc = acc + _dot((u * u).astype(BF16), wdn_ref[lo:lo + FF_CHUNK, :])
    if final:
        acc = _rms(acc, fn_ref[...])
    o_ref[...] = acc


def _post_call(layer, x, oa, ob, oc, gates, wa, wb, wc, wout, gn, wup, wdn, fn, final):
    s = x.shape[0]
    tm = POST_TM
    d = D_MODEL

    def row_spec(n):
        return pl.BlockSpec((tm, n), lambda i: (i, 0))

    return pl.pallas_call(
        functools.partial(_post_kernel, final=final),
        grid=(s // tm,),
        in_specs=[
            row_spec(d), row_spec(oa.shape[1]), row_spec(ob.shape[1]), row_spec(oc.shape[1]),
            row_spec(N_BRANCH * d),
            _const_spec(wa.shape), _const_spec(wb.shape), _const_spec(wc.shape), _const_spec((d, d)),
            _layer_spec((1, d), layer), _const_spec((d, D_FF)), _const_spec((D_FF, d)),
            _const_spec((1, d)),
        ],
        out_specs=row_spec(d),
        out_shape=jax.ShapeDtypeStruct((s, d), F32),
        compiler_params=pltpu.CompilerParams(
            dimension_semantics=("arbitrary",), vmem_limit_bytes=VMEM_LIMIT),
        name="post",
    )(x, oa, ob, oc, gates, wa, wb, wc, wout, gn, wup, wdn, fn)


def _rot_cols(w):
    half = MLA_ROPE // 2
    return jnp.concatenate([-w[..., half:], w[..., :half]], axis=-1)


def _pad_cols(w, left, total):
    return jnp.pad(w, [(0, 0)] * (w.ndim - 1) + [(left, total - left - w.shape[-1])])


def _stacked_weights(w_in, w_uq, w_ukv):
    depth = w_in.shape[0]
    n_kv = SWA_KV_HEADS * SWA_HD
    row_scale = jnp.where(jnp.arange(_W_END) >= _W_GATE, 0.5, 1.0).astype(F32)
    w_t = (jnp.swapaxes(w_in, 1, 2) * row_scale[None, :, None]).astype(BF16)
    kpe = w_in[:, :, _H_KPE:_W_QS]
    ks_swapped = jnp.concatenate([w_in[:, :, _W_KS + SWA_HD:_W_KS + n_kv],
                                  w_in[:, :, _W_KS:_W_KS + SWA_HD]], axis=-1)
    w_aux_t = jnp.swapaxes(
        jnp.concatenate([_pad_cols(_rot_cols(kpe), MLA_NOPE, HEAD_PAD), ks_swapped], axis=-1),
        1, 2).astype(BF16)

    uq = w_uq.reshape(depth, MLA_Q_LORA, MLA_HEADS, MLA_NOPE + MLA_ROPE)
    uq_plain = _pad_cols(uq, 0, HEAD_PAD)
    uq_rot = _pad_cols(_rot_cols(uq[..., MLA_NOPE:]), MLA_NOPE, HEAD_PAD)
    wuq = jnp.concatenate([uq_plain.reshape(depth, MLA_Q_LORA, -1), uq_rot.reshape(depth, MLA_Q_LORA, -1)],
                          axis=-1).astype(BF16)

    ukv = w_ukv.reshape(depth, MLA_KV_LORA, MLA_HEADS, MLA_NOPE + MLA_V)
    wk = _pad_cols(ukv[..., :MLA_NOPE], 0, HEAD_PAD).reshape(depth, MLA_KV_LORA, -1).astype(BF16)
    wv = ukv[..., MLA_NOPE:].reshape(depth, MLA_KV_LORA, -1).astype(BF16)
    return w_t, w_aux_t, wuq, wk, wv


def _rope_table(seq):
    pos = jnp.arange(seq, dtype=F32)
    inv = 1.0 / (ROPE_THETA ** (jnp.arange(0, MLA_ROPE, 2, dtype=F32) / MLA_ROPE))
    ang = pos[:, None] * inv[None, :]
    cos = jnp.concatenate([jnp.cos(ang)] * 2, axis=-1)
    sin = jnp.concatenate([jnp.sin(ang)] * 2, axis=-1)
    return jnp.concatenate([_pad_cols(cos, MLA_NOPE, HEAD_PAD), _pad_cols(sin, MLA_NOPE, HEAD_PAD)],
                           axis=-1)


def _t5_bucket(dist):
    n = jnp.maximum(dist, 0)
    max_exact = REL_BUCKETS // 2
    nf = jnp.maximum(n, 1).astype(F32)
    large = max_exact + (jnp.log(nf / max_exact) / math.log(REL_MAX_DIST / max_exact)
                         * (REL_BUCKETS - max_exact)).astype(jnp.int32)
    large = jnp.minimum(large, REL_BUCKETS - 1)
    return jnp.where(n < max_exact, n, large)


def _swa_bucket_table():
    kj = jnp.arange(2 * BLOCK)[:, None]
    qi = jnp.arange(BLOCK)[None, :]
    dist = qi + BLOCK - kj
    band = (dist >= 0) & (dist < WINDOW)
    return jnp.where(band, _t5_bucket(dist), -1).astype(jnp.int32)


def kernel(x, mem, rel_bias, attn_norm, mem_norm, w_in, b_gate, mla_q_norm, w_uq, mla_kv_norm,
           w_ukv, attn_sinks, w_mem_kv, w_o_mla, w_o_swa, w_o_mem, w_out, mlp_norm, w_up, w_down,
           final_norm):
    batch, seq, d = x.shape
    assert batch == 1 and d == D_MODEL and mem.shape == (1, MEM_LEN, D_MODEL)
    depth = w_in.shape[0]
    xs = x[0]
    mem2 = mem[0]
    tab = _rope_table(seq)
    bkt = _swa_bucket_table()
    rel_bias = rel_bias.astype(F32)

    def rows(p):
        return p.reshape(depth, 1, -1)

    w_t, w_aux_t, wuq, wk, wv = _stacked_weights(w_in, w_uq, w_ukv)
    later_weights = (w_o_mla, w_o_swa, w_o_mem, w_out, w_up, w_down, w_mem_kv)
    attn_norm, mem_norm, mla_q_norm, mla_kv_norm, mlp_norm, b_gate = (
        rows(p) for p in (attn_norm, mem_norm, mla_q_norm, mla_kv_norm, mlp_norm, b_gate))
    half_b_gate = 0.5 * b_gate
    fn = final_norm.reshape(1, d)

    for l in range(depth):
        q, k, vt, qs, ks, ksw, vst, qm, gates, wa, wb, wc, wout, wup, wdn, wmem = _proj_call(
            l, xs, attn_norm, w_t, w_aux_t, mla_q_norm, wuq, mla_kv_norm, wk, wv, half_b_gate, tab,
            later_weights)
        o_mla = _mla_call(q, k, vt)
        km, vmt = _memkv_call(l, mem2, mem_norm, wmem)
        o_swa, o_mem = _local_call(rel_bias, attn_sinks[l], bkt, qs, ks, ksw, vst, qm, km, vmt)
        xs = _post_call(l, xs, o_mla, o_swa, o_mem, gates, wa, wb, wc, wout, mlp_norm, wup, wdn, fn,
                        final=(l == depth - 1))
    return xs[None]
```

```python
import functools
import math

import jax
import jax.numpy as jnp
from jax import lax
from jax.experimental import pallas as pl
from jax.experimental.pallas import tpu as pltpu

F32 = jnp.float32
BF16 = jnp.bfloat16

D_MODEL = 1024
MLA_HEADS = 8
MLA_Q_LORA = 256
MLA_KV_LORA = 128
MLA_NOPE = 64
MLA_ROPE = 32
MLA_V = 64
ROPE_THETA = 10000.0
SWA_HEADS = 8
SWA_KV_HEADS = 2
SWA_HD = 64
WINDOW = 128
BLOCK = 128
REL_BUCKETS = 32
REL_MAX_DIST = 128
MEM_LEN = 256
MEM_HEADS = 4
MEM_HD = 128
N_BRANCH = 3
D_FF = 4 * D_MODEL
EPS = 1e-6

HEAD_PAD = 128
BF16_SUBLANES = 16
MLA_ACC_ROWS = MLA_V + BF16_SUBLANES
NEG_BIG = -1e30
LOG2E = math.log2(math.e)
LANES = 128

PROJ_TM = 512
MLA_TQ = 2048
MLA_TK = 512
MLA_TC = 256
MLA_HEADS_PER_STEP = 2
MLA_PAIRS_PER_TRIP = 2
MLA_LOOKAHEAD = 3
SWA_TQ = 1024
LOCAL_LOOKAHEAD = 5
MEM_TQ = 1024
POST_TM = 512
FF_CHUNK = 1024
VMEM_LIMIT = 56 * 1024 * 1024
N_LATER_WEIGHTS = 7

_H_CQ = 0
_H_CKV = _H_CQ + MLA_Q_LORA
_H_KPE = _H_CKV + MLA_KV_LORA
_W_CQ = _H_CQ
_W_QS = _H_KPE + MLA_ROPE
_W_KS = _W_QS + SWA_HEADS * SWA_HD
_W_GATE = _W_KS + 2 * SWA_KV_HEADS * SWA_HD + MEM_HEADS * MEM_HD
_W_END = _W_GATE + N_BRANCH * D_MODEL
_MLA_IN_COLS = 512
_R_QS = 0
_R_KS = _R_QS + SWA_HEADS * SWA_HD
_R_VS = _R_KS + SWA_KV_HEADS * SWA_HD
_R_QM = _R_VS + SWA_KV_HEADS * SWA_HD
_R_GATE = _R_QM + MEM_HEADS * MEM_HD
assert _W_QS % BF16_SUBLANES == 0 and _W_GATE == _W_QS + _R_GATE and _H_KPE + LANES <= _MLA_IN_COLS


def _rms(x, g):
    return x * lax.rsqrt(jnp.mean(x * x, axis=-1, keepdims=True) + EPS) * g


def _const_spec(shape):
    zeros = (0,) * len(shape)
    return pl.BlockSpec(shape, lambda *_: zeros, pipeline_mode=pl.Buffered(1))


def _layer_spec(shape, layer):
    index = (layer,) + (0,) * len(shape)
    return pl.BlockSpec((None,) + tuple(shape), lambda *_: index, pipeline_mode=pl.Buffered(1))


def _dot(a, b):
    return jnp.dot(a, b, preferred_element_type=F32)


def _dot_nt(a, b):
    return lax.dot_general(a, b, (((1,), (1,)), ((), ())), preferred_element_type=F32)


def _proj_kernel(x_ref, g_ref, wt_ref, wa_ref, qn_ref, wuq_ref, kvn_ref, wk_ref, wv_ref, bg_ref,
                 tab_ref, *refs):
    later_f32 = refs[:N_LATER_WEIGHTS]
    q_ref, k_ref, vt_ref, qs_ref, ks_ref, ksw_ref, vst_ref, qm_ref, gate_ref = refs[N_LATER_WEIGHTS:-N_LATER_WEIGHTS]
    later_bf16 = refs[-N_LATER_WEIGHTS:]
    for src, dst in zip(later_f32, later_bf16):
        dst[...] = src[...].astype(BF16)

    h = _rms(x_ref[...], g_ref[...]).astype(BF16)

    k_cos, k_sin = tab_ref[:, 0:HEAD_PAD], tab_ref[:, HEAD_PAD:2 * HEAD_PAD]
    q_scale = (MLA_NOPE + MLA_ROPE) ** -0.5 * LOG2E
    nope = (lax.broadcasted_iota(jnp.int32, k_cos.shape, 1) < MLA_NOPE).astype(F32)
    q_cos, q_sin = (k_cos + nope) * q_scale, k_sin * q_scale
    hw = MLA_HEADS * HEAD_PAD

    mla_in = _dot_nt(h, wt_ref[_W_CQ:_W_CQ + _MLA_IN_COLS, :])
    aux_in = _dot_nt(h, wa_ref[...])
    swa_in = _dot_nt(h, wt_ref[_W_QS:_W_GATE, :])

    cq = _rms(mla_in[:, _H_CQ:_H_CKV], qn_ref[...]).astype(BF16)
    qq = _dot(cq, wuq_ref[...])
    for hh in range(MLA_HEADS):
        lo = hh * HEAD_PAD
        q_ref[hh] = (qq[:, lo:lo + HEAD_PAD] * q_cos
                     + qq[:, hw + lo:hw + lo + HEAD_PAD] * q_sin).astype(BF16)

    ckv = _rms(mla_in[:, _H_CKV:_H_KPE], kvn_ref[...]).astype(BF16)
    kn = _dot(ckv, wk_ref[...])
    kpe = (pltpu.roll(mla_in[:, _H_KPE:_H_KPE + LANES], MLA_NOPE, axis=1) * k_cos
           + aux_in[:, 0:HEAD_PAD] * k_sin)
    for hh in range(MLA_HEADS):
        lo = hh * HEAD_PAD
        k_ref[hh] = (kn[:, lo:lo + HEAD_PAD] + kpe).astype(BF16)
    vt = _dot(ckv, wv_ref[...]).T
    for hh in range(MLA_HEADS):
        vt_ref[hh] = vt[hh * MLA_V:(hh + 1) * MLA_V, :].astype(BF16)

    qs_ref[...] = (swa_in[:, _R_QS:_R_KS] * (SWA_HD ** -0.5 * LOG2E)).astype(BF16)
    ks_ref[...] = swa_in[:, _R_KS:_R_VS].astype(BF16)
    ksw_ref[...] = aux_in[:, HEAD_PAD:].astype(BF16)
    vst_ref[...] = swa_in[:, _R_VS:_R_QM].T.astype(BF16)
    qm_ref[...] = (swa_in[:, _R_QM:_R_GATE] * (MEM_HD ** -0.5 * LOG2E)).astype(BF16)
    for c in range(N_BRANCH):
        lo = _W_GATE + c * D_MODEL
        half_z = _dot_nt(h, wt_ref[lo:lo + D_MODEL, :]) + bg_ref[:, c * D_MODEL:(c + 1) * D_MODEL]
        gate_ref[:, c * D_MODEL:(c + 1) * D_MODEL] = (0.5 * jnp.tanh(half_z) + 0.5).astype(BF16)


def _proj_call(layer, x, g, wt, wa, qn, wuq, kvn, wk, wv, bg, tab, later_weights):
    s = x.shape[0]
    tm = PROJ_TM
    steps = s // tm
    hw = MLA_HEADS * HEAD_PAD
    assert len(later_weights) == N_LATER_WEIGHTS
    assert all(w.shape[1] % (steps * BF16_SUBLANES) == 0 for w in later_weights)

    def slab_in(w):
        return pl.BlockSpec((None, w.shape[1] // steps, w.shape[2]), lambda i: (layer, i, 0))

    def slab_out(w):
        return pl.BlockSpec((w.shape[1] // steps, w.shape[2]), lambda i: (i, 0))

    head_spec = pl.BlockSpec((MLA_HEADS, tm, HEAD_PAD), lambda i: (0, i, 0))
    head_shape = jax.ShapeDtypeStruct((MLA_HEADS, s, HEAD_PAD), BF16)
    vt_spec = pl.BlockSpec((MLA_HEADS, MLA_V, tm), lambda i: (0, 0, i))
    vt_shape = jax.ShapeDtypeStruct((MLA_HEADS, MLA_V, s), BF16)

    def row_spec(n):
        return pl.BlockSpec((tm, n), lambda i: (i, 0))

    def row_shape(n):
        return jax.ShapeDtypeStruct((s, n), BF16)

    n_qs, n_kv, n_qm, n_g = SWA_HEADS * SWA_HD, SWA_KV_HEADS * SWA_HD, MEM_HEADS * MEM_HD, N_BRANCH * D_MODEL
    return pl.pallas_call(
        _proj_kernel,
        grid=(s // tm,),
        in_specs=[
            row_spec(D_MODEL),
            _layer_spec((1, D_MODEL), layer),
            _layer_spec((_W_END, D_MODEL), layer),
            _layer_spec((HEAD_PAD + n_kv, D_MODEL), layer),
            _layer_spec((1, MLA_Q_LORA), layer),
            _layer_spec((MLA_Q_LORA, 2 * hw), layer),
            _layer_spec((1, MLA_KV_LORA), layer),
            _layer_spec((MLA_KV_LORA, hw), layer),
            _layer_spec((MLA_KV_LORA, MLA_HEADS * MLA_V), layer),
            _layer_spec((1, n_g), layer),
            row_spec(2 * HEAD_PAD),
        ] + [slab_in(w) for w in later_weights],
        out_specs=[head_spec, head_spec, vt_spec,
                   row_spec(n_qs), row_spec(n_kv), row_spec(n_kv),
                   pl.BlockSpec((n_kv, tm), lambda i: (0, i)), row_spec(n_qm), row_spec(n_g)]
        + [slab_out(w) for w in later_weights],
        out_shape=[head_shape, head_shape, vt_shape,
                   row_shape(n_qs), row_shape(n_kv), row_shape(n_kv),
                   jax.ShapeDtypeStruct((n_kv, s), BF16), row_shape(n_qm), row_shape(n_g)]
        + [jax.ShapeDtypeStruct(w.shape[1:], BF16) for w in later_weights],
        compiler_params=pltpu.CompilerParams(
            dimension_semantics=("arbitrary",), vmem_limit_bytes=VMEM_LIMIT),
        name="proj",
    )(x, g, wt, wa, qn, wuq, kvn, wk, wv, bg, tab, *later_weights)


def _mla_kernel(q_ref, k_ref, vt_ref, o_ref, sa_scr, sb_scr, acc_scr):
    i = pl.program_id(1)
    tq, tk, tc, hp = MLA_TQ, MLA_TK, MLA_TC, MLA_HEADS_PER_STEP
    nc = tq // tc
    nd = tq // tk
    assert tq % tk == 0 and nd % 2 == 0 and tk % tc == 0
    assert hp * MLA_V == LANES
    key = lax.broadcasted_iota(jnp.int32, (tk, tc), 0)
    lane = lax.broadcasted_iota(jnp.int32, (tk, tc), 1)
    ones_rows = (lax.broadcasted_iota(jnp.int32, (MLA_ACC_ROWS - MLA_V, tk), 0) == 0).astype(BF16)

    units = [(h, c) for h in range(hp) for c in range(nc)]
    slot = {u: n for n, u in enumerate(units)}

    def scores(j, buf, u, key_off=None, or_valid=None):
        h, c = u
        start = pl.multiple_of(j * tk, tk)
        s = _dot_nt(k_ref[h, pl.ds(start, tk), :], q_ref[h, c * tc:(c + 1) * tc, :])
        if key_off is not None:
            valid = key + key_off <= lane + c * tc
            if or_valid is not None:
                valid = jnp.logical_or(valid, or_valid)
            s = jnp.where(valid, s, NEG_BIG)
        buf[slot[u]] = s
        return jnp.max(s, axis=0, keepdims=True)

    def accumulate(j, buf, u, m, cm):
        h, c = u
        start = pl.multiple_of(j * tk, tk)
        m_new = jnp.maximum(m, cm)
        alpha = jnp.exp2(m - m_new)
        p = jnp.exp2(buf[slot[u]] - m_new).astype(BF16)
        v1 = jnp.concatenate([vt_ref[h, :, pl.ds(start, tk)], ones_rows], axis=0)
        acc_scr[slot[u]] = alpha * acc_scr[slot[u]] + _dot(v1, p)
        return m_new

    def run(j0, carry, chunks, produce_next):
        m, cm_first = dict(zip(units, carry[0])), dict(zip(units, carry[1]))
        cm = {(0, u): cm_first[u] for u in units}
        todo = [(r, u) for r in range(1, len(chunks)) for u in chunks[r][0]]
        n_acc = len(chunks) - (1 if produce_next else 0)
        order = []

        def issue():
            r, u = todo.pop(0)
            d = chunks[r][1]
            masked = d is not None and (d + 1) * tk - 1 > u[1] * tc
            cm[(r, u)] = scores(j0 + r, sb_scr if r % 2 else sa_scr, u, key_off=d * tk if masked else None)
            order.append(("score", r, u))

        to_acc = [(r, u) for r in range(n_acc) for u in chunks[r][0]]

        def pump():
            while todo and sum(o[0] == "score" for o in order) - sum(o[0] == "acc" for o in order) < MLA_LOOKAHEAD:
                r, u = todo[0]
                if (r - 2, u) in to_acc and ("acc", r - 2, u) not in order:
                    break
                issue()

        pump()
        for r, u in to_acc:
            m[u] = accumulate(j0 + r, sb_scr if r % 2 else sa_scr, u, m[u], cm[(r, u)])
            order.append(("acc", r, u))
            pump()
        assert not todo
        for pos, (what, r, u) in enumerate(order):
            if what == "score" and ("acc", r - 2, u) in order:
                assert order.index(("acc", r - 2, u)) < pos
        next_cm = tuple(cm[(len(chunks) - 1, u)] for u in units) if produce_next else None
        return tuple(m[u] for u in units), next_cm

    plain = (units, None)
    diagonal = [([u for u in units if (u[1] + 1) * tc > d * tk], d) for d in range(nd)]

    def finish(carry, with_pair_before):
        chunks = ([plain, plain] if with_pair_before else []) + diagonal
        run(nd * i - (2 if with_pair_before else 0), carry, chunks, False)
        o_t = jnp.concatenate(
            [jnp.concatenate([acc_scr[slot[(h, c)], 0:MLA_V, :] / acc_scr[slot[(h, c)], MLA_V:MLA_V + 1, :]
                              for c in range(nc)], axis=1) for h in range(hp)], axis=0)
        for blk in range(tq // LANES):
            o_ref[blk * LANES:(blk + 1) * LANES, :] = o_t[:, blk * LANES:(blk + 1) * LANES].T.astype(BF16)

    acc_scr[...] = jnp.zeros_like(acc_scr)
    cm_a = tuple(scores(0, sa_scr, u, key_off=0, or_valid=i > 0) for u in units)
    carry = (tuple(jnp.full((1, tc), NEG_BIG, F32) for _ in units), cm_a)
    n_plain = jnp.maximum((nd // 2) * i - 1, 0)
    n_long = n_plain // MLA_PAIRS_PER_TRIP
    carry = lax.fori_loop(
        0, n_long,
        lambda t, c: run(2 * MLA_PAIRS_PER_TRIP * t, c, [plain] * (2 * MLA_PAIRS_PER_TRIP + 1), True), carry)
    carry = lax.fori_loop(
        n_long * MLA_PAIRS_PER_TRIP, n_plain, lambda t, c: run(2 * t, c, [plain] * 3, True), carry)

    @pl.when(i > 0)
    def _finish_after_pairs():
        finish(carry, True)

    @pl.when(i == 0)
    def _finish_first_tile():
        finish(carry, False)


def _mla_call(q, k, vt):
    s = q.shape[1]
    tq, hp = MLA_TQ, MLA_HEADS_PER_STEP
    n_units = hp * (tq // MLA_TC)
    return pl.pallas_call(
        _mla_kernel,
        grid=(MLA_HEADS // hp, s // tq),
        in_specs=[
            pl.BlockSpec((hp, tq, HEAD_PAD), lambda h, i: (h, i, 0)),
            pl.BlockSpec((hp, s, HEAD_PAD), lambda h, i: (h, 0, 0)),
            pl.BlockSpec((hp, MLA_V, s), lambda h, i: (h, 0, 0)),
        ],
        out_specs=pl.BlockSpec((tq, hp * MLA_V), lambda h, i: (i, h)),
        out_shape=jax.ShapeDtypeStruct((s, MLA_HEADS * MLA_V), BF16),
        scratch_shapes=[pltpu.VMEM((n_units, MLA_TK, MLA_TC), F32), pltpu.VMEM((n_units, MLA_TK, MLA_TC), F32),
                        pltpu.VMEM((n_units, MLA_ACC_ROWS, MLA_TC), F32)],
        compiler_params=pltpu.CompilerParams(
            dimension_semantics=("arbitrary", "arbitrary"), vmem_limit_bytes=VMEM_LIMIT),
        name="mla_attn",
    )(q, k, vt)


def _swa_jobs(relb_ref, sink_ref, bkt_ref, q_ref, k_ref, kp_ref, ksw_ref, kswp_ref, vt_ref, vtp_ref,
              o_ref, bias_scr):
    i = pl.program_id(0)
    tq, hd, rep = SWA_TQ, SWA_HD, SWA_HEADS // SWA_KV_HEADS
    win = 2 * BLOCK
    assert 2 * hd == LANES and rep == 4 and BLOCK == LANES

    @pl.when(i == 0)
    def _build_bias():
        bkt = bkt_ref[...]
        for hh in range(SWA_HEADS):
            t = jnp.full((win, BLOCK), NEG_BIG, F32)
            for b in range(REL_BUCKETS):
                t = jnp.where(bkt == b, relb_ref[b * SWA_HEADS + hh] * LOG2E, t)
            g, r = divmod(hh, rep)
            bias_scr[2 * g + r % 2, :, (r // 2) * BLOCK:(r // 2 + 1) * BLOCK] = t

    lo_half = lax.broadcasted_iota(jnp.int32, (BLOCK + tq, LANES), 1) < hd
    kfull = jnp.concatenate([kp_ref[...], k_ref[...]], axis=0)
    kswfull = jnp.concatenate([kswp_ref[...], ksw_ref[...]], axis=0)
    zero = jnp.zeros_like(kfull)
    k_lo = (jnp.where(lo_half, kfull, zero), jnp.where(lo_half, kswfull, zero))
    k_hi = (jnp.where(lo_half, zero, kswfull), jnp.where(lo_half, zero, kfull))
    vfull = jnp.concatenate([vtp_ref[...], vt_ref[...]], axis=1)
    ones_rows = (lax.broadcasted_iota(jnp.int32, (BF16_SUBLANES, win), 0) == 0).astype(BF16)
    key_row = lax.broadcasted_iota(jnp.int32, (win, 2 * BLOCK), 0)
    first_valid = jnp.logical_or(key_row >= BLOCK, i > 0)

    sinks = [jnp.concatenate(
        [jnp.full((1, BLOCK), sink_ref[g * rep + half] * LOG2E, F32),
         jnp.full((1, BLOCK), sink_ref[g * rep + 2 + half] * LOG2E, F32)], axis=1)
        for g in range(SWA_KV_HEADS) for half in range(2)]

    tiles = [(b, g, half) for b in range(tq // BLOCK) for g in range(SWA_KV_HEADS) for half in range(2)]

    def tile_scores(b, g, half):
        r0 = b * BLOCK
        qg = jnp.concatenate([q_ref[r0:r0 + BLOCK, (2 * g) * LANES:(2 * g + 1) * LANES],
                              q_ref[r0:r0 + BLOCK, (2 * g + 1) * LANES:(2 * g + 2) * LANES]],
                             axis=0)
        kmat = (k_lo, k_hi)[half][g]
        s = _dot_nt(kmat[r0:r0 + win], qg) + bias_scr[2 * g + half]
        return jnp.where(first_valid, s, NEG_BIG) if b == 0 else s

    def tile_output(b, g, half, s):
        r0 = b * BLOCK
        sink = sinks[2 * g + half]
        v1 = jnp.concatenate([vfull[g * hd:(g + 1) * hd, r0:r0 + win], ones_rows], axis=0)
        m = jnp.maximum(jnp.max(s, axis=0, keepdims=True), sink)
        p = jnp.exp2(s - m).astype(BF16)
        ot = _dot(v1, p)
        den = ot[hd:hd + 1] + jnp.exp2(sink - m)
        return ot[0:hd] / den

    outs = {}

    def tile_finish(b, g, half, s):
        outs[half] = tile_output(b, g, half, s)
        if half == 1:
            r0 = b * BLOCK
            for pr in range(2):
                x = jnp.concatenate([outs[0][:, pr * BLOCK:(pr + 1) * BLOCK],
                                     outs[1][:, pr * BLOCK:(pr + 1) * BLOCK]], axis=0)
                o_ref[r0:r0 + BLOCK, (2 * g + pr) * LANES:(2 * g + pr + 1) * LANES] = x.T.astype(BF16)

    return [(functools.partial(tile_scores, *tile), functools.partial(tile_finish, *tile)) for tile in tiles]


def _run_jobs(jobs, lookahead):
    pending = [score() for score, _ in jobs[:lookahead]]
    for t, (_, finish) in enumerate(jobs):
        s = pending.pop(0)
        if t + lookahead < len(jobs):
            pending.append(jobs[t + lookahead][0]())
        finish(s)


def _memkv_kernel(mem_ref, g_ref, w_ref, k_ref, vt_ref):
    n = MEM_HEADS * MEM_HD
    mn = _rms(mem_ref[...], g_ref[...]).astype(BF16)
    k_ref[...] = _dot(mn, w_ref[:, 0:n]).astype(BF16)
    vt_ref[...] = _dot(mn, w_ref[:, n:2 * n]).T.astype(BF16)


def _memkv_call(layer, mem, g, w):
    n = MEM_HEADS * MEM_HD
    return pl.pallas_call(
        _memkv_kernel,
        grid=(1,),
        in_specs=[_const_spec((MEM_LEN, D_MODEL)), _layer_spec((1, D_MODEL), layer),
                  _const_spec((D_MODEL, 2 * n))],
        out_specs=[_const_spec((MEM_LEN, n)), _const_spec((n, MEM_LEN))],
        out_shape=[jax.ShapeDtypeStruct((MEM_LEN, n), BF16), jax.ShapeDtypeStruct((n, MEM_LEN), BF16)],
        compiler_params=pltpu.CompilerParams(
            dimension_semantics=("arbitrary",), vmem_limit_bytes=VMEM_LIMIT),
        name="mem_kv",
    )(mem, g, w)


def _mem_jobs(q_ref, k_ref, vt_ref, o_ref):
    tq, dm = MEM_TQ, MEM_HD
    ones_rows = (lax.broadcasted_iota(jnp.int32, (BF16_SUBLANES, MEM_LEN), 0) == 0).astype(BF16)

    def head_scores(hh):
        return _dot_nt(k_ref[:, hh * dm:(hh + 1) * dm], q_ref[:, hh * dm:(hh + 1) * dm])

    def head_finish(hh, s):
        p = jnp.exp2(s - jnp.max(s, axis=0, keepdims=True)).astype(BF16)
        v1 = jnp.concatenate([vt_ref[hh * dm:(hh + 1) * dm, :], ones_rows], axis=0)
        ot = _dot(v1, p)
        o = ot[0:dm] / ot[dm:dm + 1]
        for blk in range(tq // LANES):
            o_ref[blk * LANES:(blk + 1) * LANES, hh * dm:(hh + 1) * dm] = (
                o[:, blk * LANES:(blk + 1) * LANES].T.astype(BF16))

    return [(functools.partial(head_scores, hh), functools.partial(head_finish, hh))
            for hh in range(MEM_HEADS)]


def _local_kernel(relb_ref, sink_ref, bkt_ref, q_ref, k_ref, kp_ref, ksw_ref, kswp_ref, vt_ref, vtp_ref,
                  qm_ref, km_ref, vmt_ref, o_ref, om_ref, bias_scr):
    swa = _swa_jobs(relb_ref, sink_ref, bkt_ref, q_ref, k_ref, kp_ref, ksw_ref, kswp_ref, vt_ref, vtp_ref,
                    o_ref, bias_scr)
    mem = _mem_jobs(qm_ref, km_ref, vmt_ref, om_ref)
    every = len(swa) // len(mem)
    jobs = []
    for n, job in enumerate(swa):
        jobs.append(job)
        if n % every == every - 1:
            jobs.append(mem[n // every])
    _run_jobs(jobs, LOCAL_LOOKAHEAD)


def _local_call(rel_bias, sinks, bkt, qs, ks, ksw, vst, qm, km, vmt):
    s = qs.shape[0]
    tq = SWA_TQ
    assert MEM_TQ == tq and MEM_HD == LANES
    nblk = tq // BLOCK
    n_kv = SWA_KV_HEADS * SWA_HD
    n_qs = SWA_HEADS * SWA_HD
    n_qm = MEM_HEADS * MEM_HD
    own = pl.BlockSpec((tq, n_kv), lambda i: (i, 0))
    prev = pl.BlockSpec((BLOCK, n_kv), lambda i: (jnp.maximum(i * nblk - 1, 0), 0))
    own_t = pl.BlockSpec((n_kv, tq), lambda i: (0, i))
    prev_t = pl.BlockSpec((n_kv, BLOCK), lambda i: (0, jnp.maximum(i * nblk - 1, 0)))
    smem = pl.BlockSpec(memory_space=pltpu.SMEM)
    return pl.pallas_call(
        _local_kernel,
        grid=(s // tq,),
        in_specs=[
            smem, smem, _const_spec((2 * BLOCK, BLOCK)),
            pl.BlockSpec((tq, n_qs), lambda i: (i, 0)),
            own, prev, own, prev, own_t, prev_t,
            pl.BlockSpec((tq, n_qm), lambda i: (i, 0)), _const_spec((MEM_LEN, n_qm)),
            _const_spec((n_qm, MEM_LEN)),
        ],
        out_specs=[pl.BlockSpec((tq, n_qs), lambda i: (i, 0)), pl.BlockSpec((tq, n_qm), lambda i: (i, 0))],
        out_shape=[jax.ShapeDtypeStruct((s, n_qs), BF16), jax.ShapeDtypeStruct((s, n_qm), BF16)],
        scratch_shapes=[pltpu.VMEM((2 * SWA_KV_HEADS, 2 * BLOCK, 2 * BLOCK), F32)],
        compiler_params=pltpu.CompilerParams(
            dimension_semantics=("arbitrary",), vmem_limit_bytes=VMEM_LIMIT),
        name="local_attn",
    )(rel_bias.reshape(-1), sinks, bkt, qs, ks, ks, ksw, ksw, vst, vst, qm, km, vmt)


def _post_kernel(x_ref, oa_ref, ob_ref, oc_ref, gate_ref, wa_ref, wb_ref, wc_ref, wout_ref,
                 gn_ref, wup_ref, wdn_ref, fn_ref, o_ref, *, final):
    d = D_MODEL
    y = (gate_ref[:, 0:d].astype(F32) * _dot(oa_ref[...], wa_ref[...])
         + gate_ref[:, d:2 * d].astype(F32) * _dot(ob_ref[...], wb_ref[...])
         + gate_ref[:, 2 * d:3 * d].astype(F32) * _dot(oc_ref[...], wc_ref[...]))
    x1 = x_ref[...] + _dot(y.astype(BF16), wout_ref[...])
    h = _rms(x1, gn_ref[...]).astype(BF16)
    acc = x1
    for c in range(D_FF // FF_CHUNK):
        lo = c * FF_CHUNK
        u = jnp.maximum(_dot(h, wup_ref[:, lo:lo + FF_CHUNK]), 0.0)
        acc = acc + _dot((u * u).astype(BF16), wdn_ref[lo:lo + FF_CHUNK, :])
    if final:
        acc = _rms(acc, fn_ref[...])
    o_ref[...] = acc


def _post_call(layer, x, oa, ob, oc, gates, wa, wb, wc, wout, gn, wup, wdn, fn, final):
    s = x.shape[0]
    tm = POST_TM
    d = D_MODEL

    def row_spec(n):
        return pl.BlockSpec((tm, n), lambda i: (i, 0))

    return pl.pallas_call(
        functools.partial(_post_kernel, final=final),
        grid=(s // tm,),
        in_specs=[
            row_spec(d), row_spec(oa.shape[1]), row_spec(ob.shape[1]), row_spec(oc.shape[1]),
            row_spec(N_BRANCH * d),
            _const_spec(wa.shape), _const_spec(wb.shape), _const_spec(wc.shape), _const_spec((d, d)),
            _layer_spec((1, d), layer), _const_spec((d, D_FF)), _const_spec((D_FF, d)),
            _const_spec((1, d)),
        ],
        out_specs=row_spec(d),
        out_shape=jax.ShapeDtypeStruct((s, d), F32),
        compiler_params=pltpu.CompilerParams(
            dimension_semantics=("arbitrary",), vmem_limit_bytes=VMEM_LIMIT),
        name="post",
    )(x, oa, ob, oc, gates, wa, wb, wc, wout, gn, wup, wdn, fn)


def _rot_cols(w):
    half = MLA_ROPE // 2
    return jnp.concatenate([-w[..., half:], w[..., :half]], axis=-1)


def _pad_cols(w, left, total):
    return jnp.pad(w, [(0, 0)] * (w.ndim - 1) + [(left, total - left - w.shape[-1])])


def _stacked_weights(w_in, w_uq, w_ukv):
    depth = w_in.shape[0]
    n_kv = SWA_KV_HEADS * SWA_HD
    row_scale = jnp.where(jnp.arange(_W_END) >= _W_GATE, 0.5, 1.0).astype(F32)
    w_t = (jnp.swapaxes(w_in, 1, 2) * row_scale[None, :, None]).astype(BF16)
    kpe = w_in[:, :, _H_KPE:_W_QS]
    ks_swapped = jnp.concatenate([w_in[:, :, _W_KS + SWA_HD:_W_KS + n_kv],
                                  w_in[:, :, _W_KS:_W_KS + SWA_HD]], axis=-1)
    w_aux_t = jnp.swapaxes(
        jnp.concatenate([_pad_cols(_rot_cols(kpe), MLA_NOPE, HEAD_PAD), ks_swapped], axis=-1),
        1, 2).astype(BF16)

    uq = w_uq.reshape(depth, MLA_Q_LORA, MLA_HEADS, MLA_NOPE + MLA_ROPE)
    uq_plain = _pad_cols(uq, 0, HEAD_PAD)
    uq_rot = _pad_cols(_rot_cols(uq[..., MLA_NOPE:]), MLA_NOPE, HEAD_PAD)
    wuq = jnp.concatenate([uq_plain.reshape(depth, MLA_Q_LORA, -1), uq_rot.reshape(depth, MLA_Q_LORA, -1)],
                          axis=-1).astype(BF16)

    ukv = w_ukv.reshape(depth, MLA_KV_LORA, MLA_HEADS, MLA_NOPE + MLA_V)
    wk = _pad_cols(ukv[..., :MLA_NOPE], 0, HEAD_PAD).reshape(depth, MLA_KV_LORA, -1).astype(BF16)
    wv = ukv[..., MLA_NOPE:].reshape(depth, MLA_KV_LORA, -1).astype(BF16)
    return w_t, w_aux_t, wuq, wk, wv


def _rope_table(seq):
    pos = jnp.arange(seq, dtype=F32)
    inv = 1.0 / (ROPE_THETA ** (jnp.arange(0, MLA_ROPE, 2, dtype=F32) / MLA_ROPE))
    ang = pos[:, None] * inv[None, :]
    cos = jnp.concatenate([jnp.cos(ang)] * 2, axis=-1)
    sin = jnp.concatenate([jnp.sin(ang)] * 2, axis=-1)
    return jnp.concatenate([_pad_cols(cos, MLA_NOPE, HEAD_PAD), _pad_cols(sin, MLA_NOPE, HEAD_PAD)],
                           axis=-1)


def _t5_bucket(dist):
    n = jnp.maximum(dist, 0)
    max_exact = REL_BUCKETS // 2
    nf = jnp.maximum(n, 1).astype(F32)
    large = max_exact + (jnp.log(nf / max_exact) / math.log(REL_MAX_DIST / max_exact)
                         * (REL_BUCKETS - max_exact)).astype(jnp.int32)
    large = jnp.minimum(large, REL_BUCKETS - 1)
    return jnp.where(n < max_exact, n, large)


def _swa_bucket_table():
    kj = jnp.arange(2 * BLOCK)[:, None]
    qi = jnp.arange(BLOCK)[None, :]
    dist = qi + BLOCK - kj
    band = (dist >= 0) & (dist < WINDOW)
    return jnp.where(band, _t5_bucket(dist), -1).astype(jnp.int32)


def kernel(x, mem, rel_bias, attn_norm, mem_norm, w_in, b_gate, mla_q_norm, w_uq, mla_kv_norm,
           w_ukv, attn_sinks, w_mem_kv, w_o_mla, w_o_swa, w_o_mem, w_out, mlp_norm, w_up, w_down,
           final_norm):
    batch, seq, d = x.shape
    assert batch == 1 and d == D_MODEL and mem.shape == (1, MEM_LEN, D_MODEL)
    depth = w_in.shape[0]
    xs = x[0]
    mem2 = mem[0]
    tab = _rope_table(seq)
    bkt = _swa_bucket_table()
    rel_bias = rel_bias.astype(F32)

    def rows(p):
        return p.reshape(depth, 1, -1)

    w_t, w_aux_t, wuq, wk, wv = _stacked_weights(w_in, w_uq, w_ukv)
    later_weights = (w_o_mla, w_o_swa, w_o_mem, w_out, w_up, w_down, w_mem_kv)
    attn_norm, mem_norm, mla_q_norm, mla_kv_norm, mlp_norm, b_gate = (
        rows(p) for p in (attn_norm, mem_norm, mla_q_norm, mla_kv_norm, mlp_norm, b_gate))
    half_b_gate = 0.5 * b_gate
    fn = final_norm.reshape(1, d)

    for l in range(depth):
        q, k, vt, qs, ks, ksw, vst, qm, gates, wa, wb, wc, wout, wup, wdn, wmem = _proj_call(
            l, xs, attn_norm, w_t, w_aux_t, mla_q_norm, wuq, mla_kv_norm, wk, wv, half_b_gate, tab,
            later_weights)
        o_mla = _mla_call(q, k, vt)
        km, vmt = _memkv_call(l, mem2, mem_norm, wmem)
        o_swa, o_mem = _local_call(rel_bias, attn_sinks[l], bkt, qs, ks, ksw, vst, qm, km, vmt)
        xs = _post_call(l, xs, o_mla, o_swa, o_mem, gates, wa, wb, wc, wout, mlp_norm, wup, wdn, fn,
                        final=(l == depth - 1))
    return xs[None]
```

```python
import functools
import math

import jax
import jax.numpy as jnp
from jax import lax
from jax.experimental import pallas as pl
from jax.experimental.pallas import tpu as pltpu

F32 = jnp.float32
BF16 = jnp.bfloat16

D_MODEL = 1024
MLA_HEADS = 8
MLA_Q_LORA = 256
MLA_KV_LORA = 128
MLA_NOPE = 64
MLA_ROPE = 32
MLA_V = 64
ROPE_THETA = 10000.0
SWA_HEADS = 8
SWA_KV_HEADS = 2
SWA_HD = 64
WINDOW = 128
BLOCK = 128
REL_BUCKETS = 32
REL_MAX_DIST = 128
MEM_LEN = 256
MEM_HEADS = 4
MEM_HD = 128
N_BRANCH = 3
D_FF = 4 * D_MODEL
EPS = 1e-6

HEAD_PAD = 128
BF16_SUBLANES = 16
MLA_ACC_ROWS = MLA_V + BF16_SUBLANES
NEG_BIG = -1e30
LOG2E = math.log2(math.e)
LANES = 128

PROJ_TM = 512
MLA_TQ = 2048
MLA_TK = 512
MLA_TC = 256
MLA_HEADS_PER_STEP = 2
MLA_PAIRS_PER_TRIP = 2
MLA_LOOKAHEAD = 3
SWA_TQ = 1024
LOCAL_LOOKAHEAD = 5
MEM_TQ = 1024
POST_TM = 512
FF_CHUNK = 1024
VMEM_LIMIT = 56 * 1024 * 1024
N_LATER_WEIGHTS = 7

_H_CQ = 0
_H_CKV = _H_CQ + MLA_Q_LORA
_H_KPE = _H_CKV + MLA_KV_LORA
_W_CQ = _H_CQ
_W_QS = _H_KPE + MLA_ROPE
_W_KS = _W_QS + SWA_HEADS * SWA_HD
_W_GATE = _W_KS + 2 * SWA_KV_HEADS * SWA_HD + MEM_HEADS * MEM_HD
_W_END = _W_GATE + N_BRANCH * D_MODEL
_MLA_IN_COLS = 512
_R_QS = 0
_R_KS = _R_QS + SWA_HEADS * SWA_HD
_R_VS = _R_KS + SWA_KV_HEADS * SWA_HD
_R_QM = _R_VS + SWA_KV_HEADS * SWA_HD
_R_GATE = _R_QM + MEM_HEADS * MEM_HD
assert _W_QS % BF16_SUBLANES == 0 and _W_GATE == _W_QS + _R_GATE and _H_KPE + LANES <= _MLA_IN_COLS


def _rms(x, g):
    return x * lax.rsqrt(jnp.mean(x * x, axis=-1, keepdims=True) + EPS) * g


def _const_spec(shape):
    zeros = (0,) * len(shape)
    return pl.BlockSpec(shape, lambda *_: zeros, pipeline_mode=pl.Buffered(1))


def _layer_spec(shape, layer):
    index = (layer,) + (0,) * len(shape)
    return pl.BlockSpec((None,) + tuple(shape), lambda *_: index, pipeline_mode=pl.Buffered(1))


def _dot(a, b):
    return jnp.dot(a, b, preferred_element_type=F32)


def _dot_nt(a, b):
    return lax.dot_general(a, b, (((1,), (1,)), ((), ())), preferred_element_type=F32)


def _proj_kernel(x_ref, g_ref, wt_ref, wa_ref, qn_ref, wuq_ref, kvn_ref, wk_ref, wv_ref, bg_ref,
                 tab_ref, *refs):
    later_f32 = refs[:N_LATER_WEIGHTS]
    q_ref, k_ref, vt_ref, qs_ref, ks_ref, ksw_ref, vst_ref, qm_ref, gate_ref = refs[N_LATER_WEIGHTS:-N_LATER_WEIGHTS]
    later_bf16 = refs[-N_LATER_WEIGHTS:]
    for src, dst in zip(later_f32, later_bf16):
        dst[...] = src[...].astype(BF16)

    h = _rms(x_ref[...], g_ref[...]).astype(BF16)

    tab = tab_ref[...].T
    k_cos, k_sin = tab[:, 0:HEAD_PAD], tab[:, HEAD_PAD:2 * HEAD_PAD]
    q_scale = (MLA_NOPE + MLA_ROPE) ** -0.5 * LOG2E
    nope = (lax.broadcasted_iota(jnp.int32, k_cos.shape, 1) < MLA_NOPE).astype(F32)
    q_cos, q_sin = (k_cos + nope) * q_scale, k_sin * q_scale
    hw = MLA_HEADS * HEAD_PAD

    mla_in = _dot_nt(h, wt_ref[_W_CQ:_W_CQ + _MLA_IN_COLS, :])
    aux_in = _dot_nt(h, wa_ref[...])
    swa_in = _dot_nt(h, wt_ref[_W_QS:_W_GATE, :])

    cq = _rms(mla_in[:, _H_CQ:_H_CKV], qn_ref[...]).astype(BF16)
    qq = _dot(cq, wuq_ref[...])
    for hh in range(MLA_HEADS):
        lo = hh * HEAD_PAD
        q_ref[hh] = (qq[:, lo:lo + HEAD_PAD] * q_cos
                     + qq[:, hw + lo:hw + lo + HEAD_PAD] * q_sin).astype(BF16)

    ckv = _rms(mla_in[:, _H_CKV:_H_KPE], kvn_ref[...]).astype(BF16)
    kn = _dot(ckv, wk_ref[...])
    kpe = (pltpu.roll(mla_in[:, _H_KPE:_H_KPE + LANES], MLA_NOPE, axis=1) * k_cos
           + aux_in[:, 0:HEAD_PAD] * k_sin)
    for hh in range(MLA_HEADS):
        lo = hh * HEAD_PAD
        k_ref[hh] = (kn[:, lo:lo + HEAD_PAD] + kpe).astype(BF16)
    vt = _dot(ckv, wv_ref[...]).T
    for hh in range(MLA_HEADS):
        vt_ref[hh] = vt[hh * MLA_V:(hh + 1) * MLA_V, :].astype(BF16)

    qs_ref[...] = (swa_in[:, _R_QS:_R_KS] * (SWA_HD ** -0.5 * LOG2E)).astype(BF16)
    ks_ref[...] = swa_in[:, _R_KS:_R_VS].astype(BF16)
    ksw_ref[...] = aux_in[:, HEAD_PAD:].astype(BF16)
    vst_ref[...] = swa_in[:, _R_VS:_R_QM].T.astype(BF16)
    qm_ref[...] = (swa_in[:, _R_QM:_R_GATE] * (MEM_HD ** -0.5 * LOG2E)).astype(BF16)
    for c in range(N_BRANCH):
        lo = _W_GATE + c * D_MODEL
        half_z = _dot_nt(h, wt_ref[lo:lo + D_MODEL, :]) + bg_ref[:, c * D_MODEL:(c + 1) * D_MODEL]
        gate_ref[:, c * D_MODEL:(c + 1) * D_MODEL] = (0.5 * jnp.tanh(half_z) + 0.5).astype(BF16)


def _proj_call(layer, x, g, wt, wa, qn, wuq, kvn, wk, wv, bg, tab, later_weights):
    s = x.shape[0]
    tm = PROJ_TM
    steps = s // tm
    hw = MLA_HEADS * HEAD_PAD
    assert len(later_weights) == N_LATER_WEIGHTS
    assert all(w.shape[1] % (steps * BF16_SUBLANES) == 0 for w in later_weights)

    def slab_in(w):
        return pl.BlockSpec((None, w.shape[1] // steps, w.shape[2]), lambda i: (layer, i, 0))

    def slab_out(w):
        return pl.BlockSpec((w.shape[1] // steps, w.shape[2]), lambda i: (i, 0))

    head_spec = pl.BlockSpec((MLA_HEADS, tm, HEAD_PAD), lambda i: (0, i, 0))
    head_shape = jax.ShapeDtypeStruct((MLA_HEADS, s, HEAD_PAD), BF16)
    vt_spec = pl.BlockSpec((MLA_HEADS, MLA_V, tm), lambda i: (0, 0, i))
    vt_shape = jax.ShapeDtypeStruct((MLA_HEADS, MLA_V, s), BF16)

    def row_spec(n):
        return pl.BlockSpec((tm, n), lambda i: (i, 0))

    def row_shape(n):
        return jax.ShapeDtypeStruct((s, n), BF16)

    n_qs, n_kv, n_qm, n_g = SWA_HEADS * SWA_HD, SWA_KV_HEADS * SWA_HD, MEM_HEADS * MEM_HD, N_BRANCH * D_MODEL
    return pl.pallas_call(
        _proj_kernel,
        grid=(s // tm,),
        in_specs=[
            row_spec(D_MODEL),
            _layer_spec((1, D_MODEL), layer),
            _layer_spec((_W_END, D_MODEL), layer),
            _layer_spec((HEAD_PAD + n_kv, D_MODEL), layer),
            _layer_spec((1, MLA_Q_LORA), layer),
            _layer_spec((MLA_Q_LORA, 2 * hw), layer),
            _layer_spec((1, MLA_KV_LORA), layer),
            _layer_spec((MLA_KV_LORA, hw), layer),
            _layer_spec((MLA_KV_LORA, MLA_HEADS * MLA_V), layer),
            _layer_spec((1, n_g), layer),
            pl.BlockSpec((2 * HEAD_PAD, tm), lambda i: (0, i)),
        ] + [slab_in(w) for w in later_weights],
        out_specs=[head_spec, head_spec, vt_spec,
                   row_spec(n_qs), row_spec(n_kv), row_spec(n_kv),
                   pl.BlockSpec((n_kv, tm), lambda i: (0, i)), row_spec(n_qm), row_spec(n_g)]
        + [slab_out(w) for w in later_weights],
        out_shape=[head_shape, head_shape, vt_shape,
                   row_shape(n_qs), row_shape(n_kv), row_shape(n_kv),
                   jax.ShapeDtypeStruct((n_kv, s), BF16), row_shape(n_qm), row_shape(n_g)]
        + [jax.ShapeDtypeStruct(w.shape[1:], BF16) for w in later_weights],
        compiler_params=pltpu.CompilerParams(
            dimension_semantics=("arbitrary",), vmem_limit_bytes=VMEM_LIMIT),
        name="proj",
    )(x, g, wt, wa, qn, wuq, kvn, wk, wv, bg, tab, *later_weights)


def _mla_kernel(q_ref, k_ref, vt_ref, o_ref, sa_scr, sb_scr, acc_scr):
    i = pl.program_id(1)
    tq, tk, tc, hp = MLA_TQ, MLA_TK, MLA_TC, MLA_HEADS_PER_STEP
    nc = tq // tc
    nd = tq // tk
    assert tq % tk == 0 and nd % 2 == 0 and tk % tc == 0
    assert hp * MLA_V == LANES
    key = lax.broadcasted_iota(jnp.int32, (tk, tc), 0)
    lane = lax.broadcasted_iota(jnp.int32, (tk, tc), 1)
    ones_rows = (lax.broadcasted_iota(jnp.int32, (MLA_ACC_ROWS - MLA_V, tk), 0) == 0).astype(BF16)

    units = [(h, c) for h in range(hp) for c in range(nc)]
    slot = {u: n for n, u in enumerate(units)}

    def scores(j, buf, u, key_off=None, or_valid=None):
        h, c = u
        start = pl.multiple_of(j * tk, tk)
        s = _dot_nt(k_ref[h, pl.ds(start, tk), :], q_ref[h, c * tc:(c + 1) * tc, :])
        if key_off is not None:
            valid = key + key_off <= lane + c * tc
            if or_valid is not None:
                valid = jnp.logical_or(valid, or_valid)
            s = jnp.where(valid, s, NEG_BIG)
        buf[slot[u]] = s
        return jnp.max(s, axis=0, keepdims=True)

    def accumulate(j, buf, u, m, cm):
        h, c = u
        start = pl.multiple_of(j * tk, tk)
        m_new = jnp.maximum(m, cm)
        alpha = jnp.exp2(m - m_new)
        p = jnp.exp2(buf[slot[u]] - m_new).astype(BF16)
        v1 = jnp.concatenate([vt_ref[h, :, pl.ds(start, tk)], ones_rows], axis=0)
        acc_scr[slot[u]] = alpha * acc_scr[slot[u]] + _dot(v1, p)
        return m_new

    def run(j0, carry, chunks, produce_next):
        m, cm_first = dict(zip(units, carry[0])), dict(zip(units, carry[1]))
        cm = {(0, u): cm_first[u] for u in units}
        todo = [(r, u) for r in range(1, len(chunks)) for u in chunks[r][0]]
        n_acc = len(chunks) - (1 if produce_next else 0)
        order = []

        def issue():
            r, u = todo.pop(0)
            d = chunks[r][1]
            masked = d is not None and (d + 1) * tk - 1 > u[1] * tc
            cm[(r, u)] = scores(j0 + r, sb_scr if r % 2 else sa_scr, u, key_off=d * tk if masked else None)
            order.append(("score", r, u))

        to_acc = [(r, u) for r in range(n_acc) for u in chunks[r][0]]

        def pump():
            while todo and sum(o[0] == "score" for o in order) - sum(o[0] == "acc" for o in order) < MLA_LOOKAHEAD:
                r, u = todo[0]
                if (r - 2, u) in to_acc and ("acc", r - 2, u) not in order:
                    break
                issue()

        pump()
        for r, u in to_acc:
            m[u] = accumulate(j0 + r, sb_scr if r % 2 else sa_scr, u, m[u], cm[(r, u)])
            order.append(("acc", r, u))
            pump()
        assert not todo
        for pos, (what, r, u) in enumerate(order):
            if what == "score" and ("acc", r - 2, u) in order:
                assert order.index(("acc", r - 2, u)) < pos
        next_cm = tuple(cm[(len(chunks) - 1, u)] for u in units) if produce_next else None
        return tuple(m[u] for u in units), next_cm

    plain = (units, None)
    diagonal = [([u for u in units if (u[1] + 1) * tc > d * tk], d) for d in range(nd)]

    def finish(carry, with_pair_before):
        chunks = ([plain, plain] if with_pair_before else []) + diagonal
        run(nd * i - (2 if with_pair_before else 0), carry, chunks, False)
        o_t = jnp.concatenate(
            [jnp.concatenate([acc_scr[slot[(h, c)], 0:MLA_V, :] / acc_scr[slot[(h, c)], MLA_V:MLA_V + 1, :]
                              for c in range(nc)], axis=1) for h in range(hp)], axis=0)
        for blk in range(tq // LANES):
            o_ref[blk * LANES:(blk + 1) * LANES, :] = o_t[:, blk * LANES:(blk + 1) * LANES].T.astype(BF16)

    acc_scr[...] = jnp.zeros_like(acc_scr)
    cm_a = tuple(scores(0, sa_scr, u, key_off=0, or_valid=i > 0) for u in units)
    carry = (tuple(jnp.full((1, tc), NEG_BIG, F32) for _ in units), cm_a)
    n_plain = jnp.maximum((nd // 2) * i - 1, 0)
    n_long = n_plain // MLA_PAIRS_PER_TRIP
    carry = lax.fori_loop(
        0, n_long,
        lambda t, c: run(2 * MLA_PAIRS_PER_TRIP * t, c, [plain] * (2 * MLA_PAIRS_PER_TRIP + 1), True), carry)
    carry = lax.fori_loop(
        n_long * MLA_PAIRS_PER_TRIP, n_plain, lambda t, c: run(2 * t, c, [plain] * 3, True), carry)

    @pl.when(i > 0)
    def _finish_after_pairs():
        finish(carry, True)

    @pl.when(i == 0)
    def _finish_first_tile():
        finish(carry, False)


def _mla_call(q, k, vt):
    s = q.shape[1]
    tq, hp = MLA_TQ, MLA_HEADS_PER_STEP
    n_units = hp * (tq // MLA_TC)
    return pl.pallas_call(
        _mla_kernel,
        grid=(MLA_HEADS // hp, s // tq),
        in_specs=[
            pl.BlockSpec((hp, tq, HEAD_PAD), lambda h, i: (h, i, 0)),
            pl.BlockSpec((hp, s, HEAD_PAD), lambda h, i: (h, 0, 0)),
            pl.BlockSpec((hp, MLA_V, s), lambda h, i: (h, 0, 0)),
        ],
        out_specs=pl.BlockSpec((tq, hp * MLA_V), lambda h, i: (i, h)),
        out_shape=jax.ShapeDtypeStruct((s, MLA_HEADS * MLA_V), BF16),
        scratch_shapes=[pltpu.VMEM((n_units, MLA_TK, MLA_TC), F32), pltpu.VMEM((n_units, MLA_TK, MLA_TC), F32),
                        pltpu.VMEM((n_units, MLA_ACC_ROWS, MLA_TC), F32)],
        compiler_params=pltpu.CompilerParams(
            dimension_semantics=("arbitrary", "arbitrary"), vmem_limit_bytes=VMEM_LIMIT),
        name="mla_attn",
    )(q, k, vt)


def _swa_jobs(relb_ref, sink_ref, bkt_ref, q_ref, k_ref, kp_ref, ksw_ref, kswp_ref, vt_ref, vtp_ref,
              o_ref, bias_scr):
    i = pl.program_id(0)
    tq, hd, rep = SWA_TQ, SWA_HD, SWA_HEADS // SWA_KV_HEADS
    win = 2 * BLOCK
    assert 2 * hd == LANES and rep == 4 and BLOCK == LANES

    @pl.when(i == 0)
    def _build_bias():
        bkt = bkt_ref[...]
        for hh in range(SWA_HEADS):
            t = jnp.full((win, BLOCK), NEG_BIG, F32)
            for b in range(REL_BUCKETS):
                t = jnp.where(bkt == b, relb_ref[b * SWA_HEADS + hh] * LOG2E, t)
            g, r = divmod(hh, rep)
            bias_scr[2 * g + r % 2, :, (r // 2) * BLOCK:(r // 2 + 1) * BLOCK] = t

    lo_half = lax.broadcasted_iota(jnp.int32, (BLOCK + tq, LANES), 1) < hd
    kfull = jnp.concatenate([kp_ref[...], k_ref[...]], axis=0)
    kswfull = jnp.concatenate([kswp_ref[...], ksw_ref[...]], axis=0)
    zero = jnp.zeros_like(kfull)
    k_lo = (jnp.where(lo_half, kfull, zero), jnp.where(lo_half, kswfull, zero))
    k_hi = (jnp.where(lo_half, zero, kswfull), jnp.where(lo_half, zero, kfull))
    vfull = jnp.concatenate([vtp_ref[...], vt_ref[...]], axis=1)
    ones_rows = (lax.broadcasted_iota(jnp.int32, (BF16_SUBLANES, win), 0) == 0).astype(BF16)
    key_row = lax.broadcasted_iota(jnp.int32, (win, 2 * BLOCK), 0)
    first_valid = jnp.logical_or(key_row >= BLOCK, i > 0)

    sinks = [jnp.concatenate(
        [jnp.full((1, BLOCK), sink_ref[g * rep + half] * LOG2E, F32),
         jnp.full((1, BLOCK), sink_ref[g * rep + 2 + half] * LOG2E, F32)], axis=1)
        for g in range(SWA_KV_HEADS) for half in range(2)]

    tiles = [(b, g, half) for b in range(tq // BLOCK) for g in range(SWA_KV_HEADS) for half in range(2)]

    def tile_scores(b, g, half):
        r0 = b * BLOCK
        qg = jnp.concatenate([q_ref[r0:r0 + BLOCK, (2 * g) * LANES:(2 * g + 1) * LANES],
                              q_ref[r0:r0 + BLOCK, (2 * g + 1) * LANES:(2 * g + 2) * LANES]],
                             axis=0)
        kmat = (k_lo, k_hi)[half][g]
        s = _dot_nt(kmat[r0:r0 + win], qg) + bias_scr[2 * g + half]
        return jnp.where(first_valid, s, NEG_BIG) if b == 0 else s

    def tile_output(b, g, half, s):
        r0 = b * BLOCK
        sink = sinks[2 * g + half]
        v1 = jnp.concatenate([vfull[g * hd:(g + 1) * hd, r0:r0 + win], ones_rows], axis=0)
        m = jnp.maximum(jnp.max(s, axis=0, keepdims=True), sink)
        p = jnp.exp2(s - m).astype(BF16)
        ot = _dot(v1, p)
        den = ot[hd:hd + 1] + jnp.exp2(sink - m)
        return ot[0:hd] / den

    outs = {}

    def tile_finish(b, g, half, s):
        outs[half] = tile_output(b, g, half, s)
        if half == 1:
            r0 = b * BLOCK
            for pr in range(2):
                x = jnp.concatenate([outs[0][:, pr * BLOCK:(pr + 1) * BLOCK],
                                     outs[1][:, pr * BLOCK:(pr + 1) * BLOCK]], axis=0)
                o_ref[r0:r0 + BLOCK, (2 * g + pr) * LANES:(2 * g + pr + 1) * LANES] = x.T.astype(BF16)

    return [(functools.partial(tile_scores, *tile), functools.partial(tile_finish, *tile)) for tile in tiles]


def _run_jobs(jobs, lookahead):
    pending = [score() for score, _ in jobs[:lookahead]]
    for t, (_, finish) in enumerate(jobs):
        s = pending.pop(0)
        if t + lookahead < len(jobs):
            pending.append(jobs[t + lookahead][0]())
        finish(s)


def _memkv_kernel(mem_ref, g_ref, w_ref, k_ref, vt_ref):
    n = MEM_HEADS * MEM_HD
    mn = _rms(mem_ref[...], g_ref[...]).astype(BF16)
    k_ref[...] = _dot(mn, w_ref[:, 0:n]).astype(BF16)
    vt_ref[...] = _dot(mn, w_ref[:, n:2 * n]).T.astype(BF16)


def _memkv_call(layer, mem, g, w):
    n = MEM_HEADS * MEM_HD
    return pl.pallas_call(
        _memkv_kernel,
        grid=(1,),
        in_specs=[_const_spec((MEM_LEN, D_MODEL)), _layer_spec((1, D_MODEL), layer),
                  _const_spec((D_MODEL, 2 * n))],
        out_specs=[_const_spec((MEM_LEN, n)), _const_spec((n, MEM_LEN))],
        out_shape=[jax.ShapeDtypeStruct((MEM_LEN, n), BF16), jax.ShapeDtypeStruct((n, MEM_LEN), BF16)],
        compiler_params=pltpu.CompilerParams(
            dimension_semantics=("arbitrary",), vmem_limit_bytes=VMEM_LIMIT),
        name="mem_kv",
    )(mem, g, w)


def _mem_jobs(q_ref, k_ref, vt_ref, o_ref):
    tq, dm = MEM_TQ, MEM_HD
    ones_rows = (lax.broadcasted_iota(jnp.int32, (BF16_SUBLANES, MEM_LEN), 0) == 0).astype(BF16)

    def head_scores(hh):
        return _dot_nt(k_ref[:, hh * dm:(hh + 1) * dm], q_ref[:, hh * dm:(hh + 1) * dm])

    def head_finish(hh, s):
        p = jnp.exp2(s - jnp.max(s, axis=0, keepdims=True)).astype(BF16)
        v1 = jnp.concatenate([vt_ref[hh * dm:(hh + 1) * dm, :], ones_rows], axis=0)
        ot = _dot(v1, p)
        o = ot[0:dm] / ot[dm:dm + 1]
        for blk in range(tq // LANES):
            o_ref[blk * LANES:(blk + 1) * LANES, hh * dm:(hh + 1) * dm] = (
                o[:, blk * LANES:(blk + 1) * LANES].T.astype(BF16))

    return [(functools.partial(head_scores, hh), functools.partial(head_finish, hh))
            for hh in range(MEM_HEADS)]


def _local_kernel(relb_ref, sink_ref, bkt_ref, q_ref, k_ref, kp_ref, ksw_ref, kswp_ref, vt_ref, vtp_ref,
                  qm_ref, km_ref, vmt_ref, o_ref, om_ref, bias_scr):
    swa = _swa_jobs(relb_ref, sink_ref, bkt_ref, q_ref, k_ref, kp_ref, ksw_ref, kswp_ref, vt_ref, vtp_ref,
                    o_ref, bias_scr)
    mem = _mem_jobs(qm_ref, km_ref, vmt_ref, om_ref)
    every = len(swa) // len(mem)
    jobs = []
    for n, job in enumerate(swa):
        jobs.append(job)
        if n % every == every - 1:
            jobs.append(mem[n // every])
    _run_jobs(jobs, LOCAL_LOOKAHEAD)


def _local_call(rel_bias, sinks, bkt, qs, ks, ksw, vst, qm, km, vmt):
    s = qs.shape[0]
    tq = SWA_TQ
    assert MEM_TQ == tq and MEM_HD == LANES
    nblk = tq // BLOCK
    n_kv = SWA_KV_HEADS * SWA_HD
    n_qs = SWA_HEADS * SWA_HD
    n_qm = MEM_HEADS * MEM_HD
    own = pl.BlockSpec((tq, n_kv), lambda i: (i, 0))
    prev = pl.BlockSpec((BLOCK, n_kv), lambda i: (jnp.maximum(i * nblk - 1, 0), 0))
    own_t = pl.BlockSpec((n_kv, tq), lambda i: (0, i))
    prev_t = pl.BlockSpec((n_kv, BLOCK), lambda i: (0, jnp.maximum(i * nblk - 1, 0)))
    smem = pl.BlockSpec(memory_space=pltpu.SMEM)
    return pl.pallas_call(
        _local_kernel,
        grid=(s // tq,),
        in_specs=[
            smem, smem, _const_spec((2 * BLOCK, BLOCK)),
            pl.BlockSpec((tq, n_qs), lambda i: (i, 0)),
            own, prev, own, prev, own_t, prev_t,
            pl.BlockSpec((tq, n_qm), lambda i: (i, 0)), _const_spec((MEM_LEN, n_qm)),
            _const_spec((n_qm, MEM_LEN)),
        ],
        out_specs=[pl.BlockSpec((tq, n_qs), lambda i: (i, 0)), pl.BlockSpec((tq, n_qm), lambda i: (i, 0))],
        out_shape=[jax.ShapeDtypeStruct((s, n_qs), BF16), jax.ShapeDtypeStruct((s, n_qm), BF16)],
        scratch_shapes=[pltpu.VMEM((2 * SWA_KV_HEADS, 2 * BLOCK, 2 * BLOCK), F32)],
        compiler_params=pltpu.CompilerParams(
            dimension_semantics=("arbitrary",), vmem_limit_bytes=VMEM_LIMIT),
        name="local_attn",
    )(rel_bias.reshape(-1), sinks, bkt, qs, ks, ks, ksw, ksw, vst, vst, qm, km, vmt)


def _post_kernel(x_ref, oa_ref, ob_ref, oc_ref, gate_ref, wa_ref, wb_ref, wc_ref, wout_ref,
                 gn_ref, wup_ref, wdn_ref, fn_ref, o_ref, *, final):
    d = D_MODEL
    y = (gate_ref[:, 0:d].astype(F32) * _dot(oa_ref[...], wa_ref[...])
         + gate_ref[:, d:2 * d].astype(F32) * _dot(ob_ref[...], wb_ref[...])
         + gate_ref[:, 2 * d:3 * d].astype(F32) * _dot(oc_ref[...], wc_ref[...]))
    x1 = x_ref[...] + _dot(y.astype(BF16), wout_ref[...])
    h = _rms(x1, gn_ref[...]).astype(BF16)
    acc = x1
    for c in range(D_FF // FF_CHUNK):
        lo = c * FF_CHUNK
        u = jnp.maximum(_dot(h, wup_ref[:, lo:lo + FF_CHUNK]), 0.0)
        acc = acc + _dot((u * u).astype(BF16), wdn_ref[lo:lo + FF_CHUNK, :])
    if final:
        acc = _rms(acc, fn_ref[...])
    o_ref[...] = acc


def _post_call(layer, x, oa, ob, oc, gates, wa, wb, wc, wout, gn, wup, wdn, fn, final):
    s = x.shape[0]
    tm = POST_TM
    d = D_MODEL

    def row_spec(n):
        return pl.BlockSpec((tm, n), lambda i: (i, 0))

    return pl.pallas_call(
        functools.partial(_post_kernel, final=final),
        grid=(s // tm,),
        in_specs=[
            row_spec(d), row_spec(oa.shape[1]), row_spec(ob.shape[1]), row_spec(oc.shape[1]),
            row_spec(N_BRANCH * d),
            _const_spec(wa.shape), _const_spec(wb.shape), _const_spec(wc.shape), _const_spec((d, d)),
            _layer_spec((1, d), layer), _const_spec((d, D_FF)), _const_spec((D_FF, d)),
            _const_spec((1, d)),
        ],
        out_specs=row_spec(d),
        out_shape=jax.ShapeDtypeStruct((s, d), F32),
        compiler_params=pltpu.CompilerParams(
            dimension_semantics=("arbitrary",), vmem_limit_bytes=VMEM_LIMIT),
        name="post",
    )(x, oa, ob, oc, gates, wa, wb, wc, wout, gn, wup, wdn, fn)


def _rot_cols(w):
    half = MLA_ROPE // 2
    return jnp.concatenate([-w[..., half:], w[..., :half]], axis=-1)


def _pad_cols(w, left, total):
    return jnp.pad(w, [(0, 0)] * (w.ndim - 1) + [(left, total - left - w.shape[-1])])


def _stacked_weights(w_in, w_uq, w_ukv):
    depth = w_in.shape[0]
    n_kv = SWA_KV_HEADS * SWA_HD
    row_scale = jnp.where(jnp.arange(_W_END) >= _W_GATE, 0.5, 1.0).astype(F32)
    w_t = (jnp.swapaxes(w_in, 1, 2) * row_scale[None, :, None]).astype(BF16)
    kpe = w_in[:, :, _H_KPE:_W_QS]
    ks_swapped = jnp.concatenate([w_in[:, :, _W_KS + SWA_HD:_W_KS + n_kv],
                                  w_in[:, :, _W_KS:_W_KS + SWA_HD]], axis=-1)
    w_aux_t = jnp.swapaxes(
        jnp.concatenate([_pad_cols(_rot_cols(kpe), MLA_NOPE, HEAD_PAD), ks_swapped], axis=-1),
        1, 2).astype(BF16)

    uq = w_uq.reshape(depth, MLA_Q_LORA, MLA_HEADS, MLA_NOPE + MLA_ROPE)
    uq_plain = _pad_cols(uq, 0, HEAD_PAD)
    uq_rot = _pad_cols(_rot_cols(uq[..., MLA_NOPE:]), MLA_NOPE, HEAD_PAD)
    wuq = jnp.concatenate([uq_plain.reshape(depth, MLA_Q_LORA, -1), uq_rot.reshape(depth, MLA_Q_LORA, -1)],
                          axis=-1).astype(BF16)

    ukv = w_ukv.reshape(depth, MLA_KV_LORA, MLA_HEADS, MLA_NOPE + MLA_V)
    wk = _pad_cols(ukv[..., :MLA_NOPE], 0, HEAD_PAD).reshape(depth, MLA_KV_LORA, -1).astype(BF16)
    wv = ukv[..., MLA_NOPE:].reshape(depth, MLA_KV_LORA, -1).astype(BF16)
    return w_t, w_aux_t, wuq, wk, wv


def _rope_table(seq):
    pos = jnp.arange(seq, dtype=F32)
    inv = 1.0 / (ROPE_THETA ** (jnp.arange(0, MLA_ROPE, 2, dtype=F32) / MLA_ROPE))
    ang = inv[:, None] * pos[None, :]
    cos, sin = jnp.cos(ang), jnp.sin(ang)
    lead = jnp.zeros((MLA_NOPE, seq), F32)
    tail = jnp.zeros((HEAD_PAD - MLA_NOPE - MLA_ROPE, seq), F32)
    return jnp.concatenate([lead, cos, cos, tail, lead, sin, sin, tail], axis=0)


def _t5_bucket(dist):
    n = jnp.maximum(dist, 0)
    max_exact = REL_BUCKETS // 2
    nf = jnp.maximum(n, 1).astype(F32)
    large = max_exact + (jnp.log(nf / max_exact) / math.log(REL_MAX_DIST / max_exact)
                         * (REL_BUCKETS - max_exact)).astype(jnp.int32)
    large = jnp.minimum(large, REL_BUCKETS - 1)
    return jnp.where(n < max_exact, n, large)


def _swa_bucket_table():
    kj = jnp.arange(2 * BLOCK)[:, None]
    qi = jnp.arange(BLOCK)[None, :]
    dist = qi + BLOCK - kj
    band = (dist >= 0) & (dist < WINDOW)
    return jnp.where(band, _t5_bucket(dist), -1).astype(jnp.int32)


def kernel(x, mem, rel_bias, attn_norm, mem_norm, w_in, b_gate, mla_q_norm, w_uq, mla_kv_norm,
           w_ukv, attn_sinks, w_mem_kv, w_o_mla, w_o_swa, w_o_mem, w_out, mlp_norm, w_up, w_down,
           final_norm):
    batch, seq, d = x.shape
    assert batch == 1 and d == D_MODEL and mem.shape == (1, MEM_LEN, D_MODEL)
    depth = w_in.shape[0]
    xs = x[0]
    mem2 = mem[0]
    tab = _rope_table(seq)
    bkt = _swa_bucket_table()
    rel_bias = rel_bias.astype(F32)

    def rows(p):
        return p.reshape(depth, 1, -1)

    w_t, w_aux_t, wuq, wk, wv = _stacked_weights(w_in, w_uq, w_ukv)
    later_weights = (w_o_mla, w_o_swa, w_o_mem, w_out, w_up, w_down, w_mem_kv)
    attn_norm, mem_norm, mla_q_norm, mla_kv_norm, mlp_norm, b_gate = (
        rows(p) for p in (attn_norm, mem_norm, mla_q_norm, mla_kv_norm, mlp_norm, b_gate))
    half_b_gate = 0.5 * b_gate
    fn = final_norm.reshape(1, d)

    for l in range(depth):
        q, k, vt, qs, ks, ksw, vst, qm, gates, wa, wb, wc, wout, wup, wdn, wmem = _proj_call(
            l, xs, attn_norm, w_t, w_aux_t, mla_q_norm, wuq, mla_kv_norm, wk, wv, half_b_gate, tab,
            later_weights)
        o_mla = _mla_call(q, k, vt)
        km, vmt = _memkv_call(l, mem2, mem_norm, wmem)
        o_swa, o_mem = _local_call(rel_bias, attn_sinks[l], bkt, qs, ks, ksw, vst, qm, km, vmt)
        xs = _post_call(l, xs, o_mla, o_swa, o_mem, gates, wa, wb, wc, wout, mlp_norm, wup, wdn, fn,
                        final=(l == depth - 1))
    return xs[None]
```

```python
import functools
import math

import jax
import jax.numpy as jnp
from jax import lax
from jax.experimental import pallas as pl
from jax.experimental.pallas import tpu as pltpu

F32 = jnp.float32
BF16 = jnp.bfloat16

D_MODEL = 1024
MLA_HEADS = 8
MLA_Q_LORA = 256
MLA_KV_LORA = 128
MLA_NOPE = 64
MLA_ROPE = 32
MLA_V = 64
ROPE_THETA = 10000.0
SWA_HEADS = 8
SWA_KV_HEADS = 2
SWA_HD = 64
WINDOW = 128
BLOCK = 128
REL_BUCKETS = 32
REL_MAX_DIST = 128
MEM_LEN = 256
MEM_HEADS = 4
MEM_HD = 128
N_BRANCH = 3
D_FF = 4 * D_MODEL
EPS = 1e-6

HEAD_PAD = 128
BF16_SUBLANES = 16
MLA_ACC_ROWS = MLA_V + BF16_SUBLANES
NEG_BIG = -1e30
LOG2E = math.log2(math.e)
LANES = 128

PROJ_TM = 512
MLA_TQ = 2048
MLA_TK = 512
MLA_TC = 256
MLA_HEADS_PER_STEP = 2
MLA_PAIRS_PER_TRIP = 2
MLA_LOOKAHEAD = 3
SWA_TQ = 1024
LOCAL_LOOKAHEAD = 5
MEM_TQ = 1024
POST_TM = 512
FF_CHUNK = 1024
VMEM_LIMIT = 56 * 1024 * 1024
N_LATER_WEIGHTS = 7

_H_CQ = 0
_H_CKV = _H_CQ + MLA_Q_LORA
_H_KPE = _H_CKV + MLA_KV_LORA
_W_CQ = _H_CQ
_W_QS = _H_KPE + MLA_ROPE
_W_KS = _W_QS + SWA_HEADS * SWA_HD
_W_GATE = _W_KS + 2 * SWA_KV_HEADS * SWA_HD + MEM_HEADS * MEM_HD
_W_END = _W_GATE + N_BRANCH * D_MODEL
_MLA_IN_COLS = 512
_R_QS = 0
_R_KS = _R_QS + SWA_HEADS * SWA_HD
_R_VS = _R_KS + SWA_KV_HEADS * SWA_HD
_R_QM = _R_VS + SWA_KV_HEADS * SWA_HD
_R_GATE = _R_QM + MEM_HEADS * MEM_HD
assert _W_QS % BF16_SUBLANES == 0 and _W_GATE == _W_QS + _R_GATE and _H_KPE + LANES <= _MLA_IN_COLS


def _rms(x, g):
    return x * lax.rsqrt(jnp.mean(x * x, axis=-1, keepdims=True) + EPS) * g


def _const_spec(shape):
    zeros = (0,) * len(shape)
    return pl.BlockSpec(shape, lambda *_: zeros, pipeline_mode=pl.Buffered(1))


def _layer_spec(shape, layer):
    index = (layer,) + (0,) * len(shape)
    return pl.BlockSpec((None,) + tuple(shape), lambda *_: index, pipeline_mode=pl.Buffered(1))


def _dot(a, b):
    return jnp.dot(a, b, preferred_element_type=F32)


def _dot_nt(a, b):
    return lax.dot_general(a, b, (((1,), (1,)), ((), ())), preferred_element_type=F32)


def _proj_kernel(x_ref, g_ref, wt_ref, wa_ref, qn_ref, wuq_ref, kvn_ref, wk_ref, wv_ref, bg_ref,
                 tab_ref, *refs):
    later_f32 = refs[:N_LATER_WEIGHTS]
    q_ref, k_ref, vt_ref, qs_ref, ks_ref, ksw_ref, vst_ref, qm_ref, gate_ref = refs[N_LATER_WEIGHTS:-N_LATER_WEIGHTS]
    later_bf16 = refs[-N_LATER_WEIGHTS:]
    for src, dst in zip(later_f32, later_bf16):
        dst[...] = src[...].astype(BF16)

    h = _rms(x_ref[...], g_ref[...]).astype(BF16)

    tab = tab_ref[...].T
    k_cos, k_sin = tab[:, 0:HEAD_PAD], tab[:, HEAD_PAD:2 * HEAD_PAD]
    q_scale = (MLA_NOPE + MLA_ROPE) ** -0.5 * LOG2E
    nope = (lax.broadcasted_iota(jnp.int32, k_cos.shape, 1) < MLA_NOPE).astype(F32)
    q_cos, q_sin = (k_cos + nope) * q_scale, k_sin * q_scale
    hw = MLA_HEADS * HEAD_PAD

    mla_in = _dot_nt(h, wt_ref[_W_CQ:_W_CQ + _MLA_IN_COLS, :])
    aux_in = _dot_nt(h, wa_ref[...])
    swa_in = _dot_nt(h, wt_ref[_W_QS:_W_GATE, :])

    cq = _rms(mla_in[:, _H_CQ:_H_CKV], qn_ref[...]).astype(BF16)
    qq = _dot(cq, wuq_ref[...])
    for hh in range(MLA_HEADS):
        lo = hh * HEAD_PAD
        q_ref[hh] = (qq[:, lo:lo + HEAD_PAD] * q_cos
                     + qq[:, hw + lo:hw + lo + HEAD_PAD] * q_sin).astype(BF16)

    ckv = _rms(mla_in[:, _H_CKV:_H_KPE], kvn_ref[...]).astype(BF16)
    kn = _dot(ckv, wk_ref[...])
    kpe = (pltpu.roll(mla_in[:, _H_KPE:_H_KPE + LANES], MLA_NOPE, axis=1) * k_cos
           + aux_in[:, 0:HEAD_PAD] * k_sin)
    for hh in range(MLA_HEADS):
        lo = hh * HEAD_PAD
        k_ref[hh] = (kn[:, lo:lo + HEAD_PAD] + kpe).astype(BF16)
    vt = _dot(ckv, wv_ref[...]).T
    for hh in range(MLA_HEADS):
        vt_ref[hh] = vt[hh * MLA_V:(hh + 1) * MLA_V, :].astype(BF16)

    qs_ref[...] = (swa_in[:, _R_QS:_R_KS] * (SWA_HD ** -0.5 * LOG2E)).astype(BF16)
    ks_ref[...] = swa_in[:, _R_KS:_R_VS].astype(BF16)
    ksw_ref[...] = aux_in[:, HEAD_PAD:].astype(BF16)
    vst_ref[...] = swa_in[:, _R_VS:_R_QM].T.astype(BF16)
    qm_ref[...] = (swa_in[:, _R_QM:_R_GATE] * (MEM_HD ** -0.5 * LOG2E)).astype(BF16)
    for c in range(N_BRANCH):
        lo = _W_GATE + c * D_MODEL
        half_z = _dot_nt(h, wt_ref[lo:lo + D_MODEL, :]) + bg_ref[:, c * D_MODEL:(c + 1) * D_MODEL]
        gate_ref[:, c * D_MODEL:(c + 1) * D_MODEL] = (0.5 * jnp.tanh(half_z) + 0.5).astype(BF16)


def _proj_call(layer, x, g, wt, wa, qn, wuq, kvn, wk, wv, bg, tab, later_weights):
    s = x.shape[0]
    tm = PROJ_TM
    steps = s // tm
    hw = MLA_HEADS * HEAD_PAD
    assert len(later_weights) == N_LATER_WEIGHTS
    assert all(w.shape[1] % (steps * BF16_SUBLANES) == 0 for w in later_weights)

    def slab_in(w):
        return pl.BlockSpec((None, w.shape[1] // steps, w.shape[2]), lambda i: (layer, i, 0))

    def slab_out(w):
        return pl.BlockSpec((w.shape[1] // steps, w.shape[2]), lambda i: (i, 0))

    head_spec = pl.BlockSpec((MLA_HEADS, tm, HEAD_PAD), lambda i: (0, i, 0))
    head_shape = jax.ShapeDtypeStruct((MLA_HEADS, s, HEAD_PAD), BF16)
    vt_spec = pl.BlockSpec((MLA_HEADS, MLA_V, tm), lambda i: (0, 0, i))
    vt_shape = jax.ShapeDtypeStruct((MLA_HEADS, MLA_V, s), BF16)

    def row_spec(n):
        return pl.BlockSpec((tm, n), lambda i: (i, 0))

    def row_shape(n):
        return jax.ShapeDtypeStruct((s, n), BF16)

    n_qs, n_kv, n_qm, n_g = SWA_HEADS * SWA_HD, SWA_KV_HEADS * SWA_HD, MEM_HEADS * MEM_HD, N_BRANCH * D_MODEL
    return pl.pallas_call(
        _proj_kernel,
        grid=(s // tm,),
        in_specs=[
            row_spec(D_MODEL),
            _layer_spec((1, D_MODEL), layer),
            _layer_spec((_W_END, D_MODEL), layer),
            _layer_spec((HEAD_PAD + n_kv, D_MODEL), layer),
            _layer_spec((1, MLA_Q_LORA), layer),
            _layer_spec((MLA_Q_LORA, 2 * hw), layer),
            _layer_spec((1, MLA_KV_LORA), layer),
            _layer_spec((MLA_KV_LORA, hw), layer),
            _layer_spec((MLA_KV_LORA, MLA_HEADS * MLA_V), layer),
            _layer_spec((1, n_g), layer),
            pl.BlockSpec((2 * HEAD_PAD, tm), lambda i: (0, i)),
        ] + [slab_in(w) for w in later_weights],
        out_specs=[head_spec, head_spec, vt_spec,
                   row_spec(n_qs), row_spec(n_kv), row_spec(n_kv),
                   pl.BlockSpec((n_kv, tm), lambda i: (0, i)), row_spec(n_qm), row_spec(n_g)]
        + [slab_out(w) for w in later_weights],
        out_shape=[head_shape, head_shape, vt_shape,
                   row_shape(n_qs), row_shape(n_kv), row_shape(n_kv),
                   jax.ShapeDtypeStruct((n_kv, s), BF16), row_shape(n_qm), row_shape(n_g)]
        + [jax.ShapeDtypeStruct(w.shape[1:], BF16) for w in later_weights],
        compiler_params=pltpu.CompilerParams(
            dimension_semantics=("arbitrary",), vmem_limit_bytes=VMEM_LIMIT),
        name="proj",
    )(x, g, wt, wa, qn, wuq, kvn, wk, wv, bg, tab, *later_weights)


def _mla_kernel(q_ref, k_ref, vt_ref, o_ref, sa_scr, sb_scr, acc_scr):
    i = pl.program_id(1)
    tq, tk, tc, hp = MLA_TQ, MLA_TK, MLA_TC, MLA_HEADS_PER_STEP
    nc = tq // tc
    nd = tq // tk
    assert tq % tk == 0 and nd % 2 == 0 and tk % tc == 0
    assert hp * MLA_V == LANES
    key = lax.broadcasted_iota(jnp.int32, (tk, tc), 0)
    lane = lax.broadcasted_iota(jnp.int32, (tk, tc), 1)
    ones_rows = (lax.broadcasted_iota(jnp.int32, (MLA_ACC_ROWS - MLA_V, tk), 0) == 0).astype(BF16)

    units = [(h, c) for h in range(hp) for c in range(nc)]
    slot = {u: n for n, u in enumerate(units)}

    def scores(j, buf, u, key_off=None, or_valid=None):
        h, c = u
        start = pl.multiple_of(j * tk, tk)
        s = _dot_nt(k_ref[h, pl.ds(start, tk), :], q_ref[h, c * tc:(c + 1) * tc, :])
        if key_off is not None:
            valid = key + key_off <= lane + c * tc
            if or_valid is not None:
                valid = jnp.logical_or(valid, or_valid)
            s = jnp.where(valid, s, NEG_BIG)
        buf[slot[u]] = s
        return jnp.max(s, axis=0, keepdims=True)

    def accumulate(j, buf, u, m, cm):
        h, c = u
        start = pl.multiple_of(j * tk, tk)
        m_new = jnp.maximum(m, cm)
        alpha = jnp.exp2(m - m_new)
        p = jnp.exp2(buf[slot[u]] - m_new).astype(BF16)
        v1 = jnp.concatenate([vt_ref[h, :, pl.ds(start, tk)], ones_rows], axis=0)
        acc_scr[slot[u]] = alpha * acc_scr[slot[u]] + _dot(v1, p)
        return m_new

    def run(j0, carry, chunks, produce_next):
        m, cm_first = dict(zip(units, carry[0])), dict(zip(units, carry[1]))
        cm = {(0, u): cm_first[u] for u in units}
        todo = [(r, u) for r in range(1, len(chunks)) for u in chunks[r][0]]
        n_acc = len(chunks) - (1 if produce_next else 0)
        order = []

        def issue():
            r, u = todo.pop(0)
            d = chunks[r][1]
            masked = d is not None and (d + 1) * tk - 1 > u[1] * tc
            cm[(r, u)] = scores(j0 + r, sb_scr if r % 2 else sa_scr, u, key_off=d * tk if masked else None)
            order.append(("score", r, u))

        to_acc = [(r, u) for r in range(n_acc) for u in chunks[r][0]]

        def pump():
            while todo and sum(o[0] == "score" for o in order) - sum(o[0] == "acc" for o in order) < MLA_LOOKAHEAD:
                r, u = todo[0]
                if (r - 2, u) in to_acc and ("acc", r - 2, u) not in order:
                    break
                issue()

        pump()
        for r, u in to_acc:
            m[u] = accumulate(j0 + r, sb_scr if r % 2 else sa_scr, u, m[u], cm[(r, u)])
            order.append(("acc", r, u))
            pump()
        assert not todo
        for pos, (what, r, u) in enumerate(order):
            if what == "score" and ("acc", r - 2, u) in order:
                assert order.index(("acc", r - 2, u)) < pos
        next_cm = tuple(cm[(len(chunks) - 1, u)] for u in units) if produce_next else None
        return tuple(m[u] for u in units), next_cm

    plain = (units, None)
    diagonal = [([u for u in units if (u[1] + 1) * tc > d * tk], d) for d in range(nd)]

    def finish(carry, with_pair_before):
        chunks = ([plain, plain] if with_pair_before else []) + diagonal
        run(nd * i - (2 if with_pair_before else 0), carry, chunks, False)
        o_t = jnp.concatenate(
            [jnp.concatenate([acc_scr[slot[(h, c)], 0:MLA_V, :] / acc_scr[slot[(h, c)], MLA_V:MLA_V + 1, :]
                              for c in range(nc)], axis=1) for h in range(hp)], axis=0)
        for blk in range(tq // LANES):
            o_ref[blk * LANES:(blk + 1) * LANES, :] = o_t[:, blk * LANES:(blk + 1) * LANES].T.astype(BF16)

    acc_scr[...] = jnp.zeros_like(acc_scr)
    cm_a = tuple(scores(0, sa_scr, u, key_off=0, or_valid=i > 0) for u in units)
    carry = (tuple(jnp.full((1, tc), NEG_BIG, F32) for _ in units), cm_a)
    n_plain = jnp.maximum((nd // 2) * i - 1, 0)
    n_long = n_plain // MLA_PAIRS_PER_TRIP
    carry = lax.fori_loop(
        0, n_long,
        lambda t, c: run(2 * MLA_PAIRS_PER_TRIP * t, c, [plain] * (2 * MLA_PAIRS_PER_TRIP + 1), True), carry)
    carry = lax.fori_loop(
        n_long * MLA_PAIRS_PER_TRIP, n_plain, lambda t, c: run(2 * t, c, [plain] * 3, True), carry)

    @pl.when(i > 0)
    def _finish_after_pairs():
        finish(carry, True)

    @pl.when(i == 0)
    def _finish_first_tile():
        finish(carry, False)


def _mla_call(q, k, vt):
    s = q.shape[1]
    tq, hp = MLA_TQ, MLA_HEADS_PER_STEP
    n_units = hp * (tq // MLA_TC)
    return pl.pallas_call(
        _mla_kernel,
        grid=(MLA_HEADS // hp, s // tq),
        in_specs=[
            pl.BlockSpec((hp, tq, HEAD_PAD), lambda h, i: (h, i, 0)),
            pl.BlockSpec((hp, s, HEAD_PAD), lambda h, i: (h, 0, 0)),
            pl.BlockSpec((hp, MLA_V, s), lambda h, i: (h, 0, 0)),
        ],
        out_specs=pl.BlockSpec((tq, hp * MLA_V), lambda h, i: (i, h)),
        out_shape=jax.ShapeDtypeStruct((s, MLA_HEADS * MLA_V), BF16),
        scratch_shapes=[pltpu.VMEM((n_units, MLA_TK, MLA_TC), F32), pltpu.VMEM((n_units, MLA_TK, MLA_TC), F32),
                        pltpu.VMEM((n_units, MLA_ACC_ROWS, MLA_TC), F32)],
        compiler_params=pltpu.CompilerParams(
            dimension_semantics=("arbitrary", "arbitrary"), vmem_limit_bytes=VMEM_LIMIT),
        name="mla_attn",
    )(q, k, vt)


def _swa_jobs(relb_ref, sink_ref, bkt_ref, q_ref, k_ref, kp_ref, ksw_ref, kswp_ref, vt_ref, vtp_ref,
              o_ref, bias_scr):
    i = pl.program_id(0)
    tq, hd, rep = SWA_TQ, SWA_HD, SWA_HEADS // SWA_KV_HEADS
    win = 2 * BLOCK
    assert 2 * hd == LANES and rep == 4 and BLOCK == LANES

    @pl.when(i == 0)
    def _build_bias():
        bkt = bkt_ref[...]
        for hh in range(SWA_HEADS):
            t = jnp.full((win, BLOCK), NEG_BIG, F32)
            for b in range(REL_BUCKETS):
                t = jnp.where(bkt == b, relb_ref[b * SWA_HEADS + hh] * LOG2E, t)
            g, r = divmod(hh, rep)
            bias_scr[2 * g + r % 2, :, (r // 2) * BLOCK:(r // 2 + 1) * BLOCK] = t

    lo_half = lax.broadcasted_iota(jnp.int32, (BLOCK + tq, LANES), 1) < hd
    kfull = jnp.concatenate([kp_ref[...], k_ref[...]], axis=0)
    kswfull = jnp.concatenate([kswp_ref[...], ksw_ref[...]], axis=0)
    zero = jnp.zeros_like(kfull)
    k_lo = (jnp.where(lo_half, kfull, zero), jnp.where(lo_half, kswfull, zero))
    k_hi = (jnp.where(lo_half, zero, kswfull), jnp.where(lo_half, zero, kfull))
    vfull = jnp.concatenate([vtp_ref[...], vt_ref[...]], axis=1)
    ones_rows = (lax.broadcasted_iota(jnp.int32, (BF16_SUBLANES, win), 0) == 0).astype(BF16)
    key_row = lax.broadcasted_iota(jnp.int32, (win, 2 * BLOCK), 0)
    first_valid = jnp.logical_or(key_row >= BLOCK, i > 0)

    sinks = [jnp.concatenate(
        [jnp.full((1, BLOCK), sink_ref[g * rep + half] * LOG2E, F32),
         jnp.full((1, BLOCK), sink_ref[g * rep + 2 + half] * LOG2E, F32)], axis=1)
        for g in range(SWA_KV_HEADS) for half in range(2)]

    tiles = [(b, g, half) for b in range(tq // BLOCK) for g in range(SWA_KV_HEADS) for half in range(2)]

    def tile_scores(b, g, half):
        r0 = b * BLOCK
        qg = jnp.concatenate([q_ref[r0:r0 + BLOCK, (2 * g) * LANES:(2 * g + 1) * LANES],
                              q_ref[r0:r0 + BLOCK, (2 * g + 1) * LANES:(2 * g + 2) * LANES]],
                             axis=0)
        kmat = (k_lo, k_hi)[half][g]
        s = _dot_nt(kmat[r0:r0 + win], qg) + bias_scr[2 * g + half]
        return jnp.where(first_valid, s, NEG_BIG) if b == 0 else s

    def tile_output(b, g, half, s):
        r0 = b * BLOCK
        sink = sinks[2 * g + half]
        v1 = jnp.concatenate([vfull[g * hd:(g + 1) * hd, r0:r0 + win], ones_rows], axis=0)
        m = jnp.maximum(jnp.max(s, axis=0, keepdims=True), sink)
        p = jnp.exp2(s - m).astype(BF16)
        ot = _dot(v1, p)
        den = ot[hd:hd + 1] + jnp.exp2(sink - m)
        return ot[0:hd] / den

    outs = {}

    def tile_finish(b, g, half, s):
        outs[half] = tile_output(b, g, half, s)
        if half == 1:
            r0 = b * BLOCK
            for pr in range(2):
                x = jnp.concatenate([outs[0][:, pr * BLOCK:(pr + 1) * BLOCK],
                                     outs[1][:, pr * BLOCK:(pr + 1) * BLOCK]], axis=0)
                o_ref[r0:r0 + BLOCK, (2 * g + pr) * LANES:(2 * g + pr + 1) * LANES] = x.T.astype(BF16)

    return [(functools.partial(tile_scores, *tile), functools.partial(tile_finish, *tile)) for tile in tiles]


def _run_jobs(jobs, lookahead):
    pending = [score() for score, _ in jobs[:lookahead]]
    for t, (_, finish) in enumerate(jobs):
        s = pending.pop(0)
        if t + lookahead < len(jobs):
            pending.append(jobs[t + lookahead][0]())
        finish(s)


def _memkv_kernel(mem_ref, g_ref, w_ref, k_ref, vt_ref):
    n = MEM_HEADS * MEM_HD
    mn = _rms(mem_ref[...], g_ref[...]).astype(BF16)
    k_ref[...] = _dot(mn, w_ref[:, 0:n]).astype(BF16)
    vt_ref[...] = _dot(mn, w_ref[:, n:2 * n]).T.astype(BF16)


def _mem_jobs(q_ref, k_ref, vt_ref, o_ref):
    tq, dm = MEM_TQ, MEM_HD
    ones_rows = (lax.broadcasted_iota(jnp.int32, (BF16_SUBLANES, MEM_LEN), 0) == 0).astype(BF16)

    def head_scores(hh):
        return _dot_nt(k_ref[:, hh * dm:(hh + 1) * dm], q_ref[:, hh * dm:(hh + 1) * dm])

    def head_finish(hh, s):
        p = jnp.exp2(s - jnp.max(s, axis=0, keepdims=True)).astype(BF16)
        v1 = jnp.concatenate([vt_ref[hh * dm:(hh + 1) * dm, :], ones_rows], axis=0)
        ot = _dot(v1, p)
        o = ot[0:dm] / ot[dm:dm + 1]
        for blk in range(tq // LANES):
            o_ref[blk * LANES:(blk + 1) * LANES, hh * dm:(hh + 1) * dm] = (
                o[:, blk * LANES:(blk + 1) * LANES].T.astype(BF16))

    return [(functools.partial(head_scores, hh), functools.partial(head_finish, hh))
            for hh in range(MEM_HEADS)]


def _local_kernel(relb_ref, sink_ref, bkt_ref, q_ref, k_ref, kp_ref, ksw_ref, kswp_ref, vt_ref, vtp_ref,
                  qm_ref, mem_ref, gmem_ref, wmem_ref, o_ref, om_ref, bias_scr, km_scr, vmt_scr):
    @pl.when(pl.program_id(0) == 0)
    def _project_memory():
        _memkv_kernel(mem_ref, gmem_ref, wmem_ref, km_scr, vmt_scr)

    swa = _swa_jobs(relb_ref, sink_ref, bkt_ref, q_ref, k_ref, kp_ref, ksw_ref, kswp_ref, vt_ref, vtp_ref,
                    o_ref, bias_scr)
    mem = _mem_jobs(qm_ref, km_scr, vmt_scr, om_ref)
    every = len(swa) // len(mem)
    jobs = []
    for n, job in enumerate(swa):
        jobs.append(job)
        if n % every == every - 1:
            jobs.append(mem[n // every])
    _run_jobs(jobs, LOCAL_LOOKAHEAD)


def _local_call(layer, rel_bias, sinks, bkt, qs, ks, ksw, vst, qm, mem, gmem, wmem):
    s = qs.shape[0]
    tq = SWA_TQ
    assert MEM_TQ == tq and MEM_HD == LANES
    nblk = tq // BLOCK
    n_kv = SWA_KV_HEADS * SWA_HD
    n_qs = SWA_HEADS * SWA_HD
    n_qm = MEM_HEADS * MEM_HD
    own = pl.BlockSpec((tq, n_kv), lambda i: (i, 0))
    prev = pl.BlockSpec((BLOCK, n_kv), lambda i: (jnp.maximum(i * nblk - 1, 0), 0))
    own_t = pl.BlockSpec((n_kv, tq), lambda i: (0, i))
    prev_t = pl.BlockSpec((n_kv, BLOCK), lambda i: (0, jnp.maximum(i * nblk - 1, 0)))
    smem = pl.BlockSpec(memory_space=pltpu.SMEM)
    return pl.pallas_call(
        _local_kernel,
        grid=(s // tq,),
        in_specs=[
            smem, smem, _const_spec((2 * BLOCK, BLOCK)),
            pl.BlockSpec((tq, n_qs), lambda i: (i, 0)),
            own, prev, own, prev, own_t, prev_t,
            pl.BlockSpec((tq, n_qm), lambda i: (i, 0)), _const_spec((MEM_LEN, D_MODEL)),
            _layer_spec((1, D_MODEL), layer), _const_spec((D_MODEL, 2 * n_qm)),
        ],
        out_specs=[pl.BlockSpec((tq, n_qs), lambda i: (i, 0)), pl.BlockSpec((tq, n_qm), lambda i: (i, 0))],
        out_shape=[jax.ShapeDtypeStruct((s, n_qs), BF16), jax.ShapeDtypeStruct((s, n_qm), BF16)],
        scratch_shapes=[pltpu.VMEM((2 * SWA_KV_HEADS, 2 * BLOCK, 2 * BLOCK), F32),
                        pltpu.VMEM((MEM_LEN, n_qm), BF16), pltpu.VMEM((n_qm, MEM_LEN), BF16)],
        compiler_params=pltpu.CompilerParams(
            dimension_semantics=("arbitrary",), vmem_limit_bytes=VMEM_LIMIT),
        name="local_attn",
    )(rel_bias.reshape(-1), sinks, bkt, qs, ks, ks, ksw, ksw, vst, vst, qm, mem, gmem, wmem)


def _post_kernel(x_ref, oa_ref, ob_ref, oc_ref, gate_ref, wa_ref, wb_ref, wc_ref, wout_ref,
                 gn_ref, wup_ref, wdn_ref, fn_ref, o_ref, *, final):
    d = D_MODEL
    y = (gate_ref[:, 0:d].astype(F32) * _dot(oa_ref[...], wa_ref[...])
         + gate_ref[:, d:2 * d].astype(F32) * _dot(ob_ref[...], wb_ref[...])
         + gate_ref[:, 2 * d:3 * d].astype(F32) * _dot(oc_ref[...], wc_ref[...]))
    x1 = x_ref[...] + _dot(y.astype(BF16), wout_ref[...])
    h = _rms(x1, gn_ref[...]).astype(BF16)
    acc = x1
    for c in range(D_FF // FF_CHUNK):
        lo = c * FF_CHUNK
        u = jnp.maximum(_dot(h, wup_ref[:, lo:lo + FF_CHUNK]), 0.0)
        acc = acc + _dot((u * u).astype(BF16), wdn_ref[lo:lo + FF_CHUNK, :])
    if final:
        acc = _rms(acc, fn_ref[...])
    o_ref[...] = acc


def _post_call(layer, x, oa, ob, oc, gates, wa, wb, wc, wout, gn, wup, wdn, fn, final):
    s = x.shape[0]
    tm = POST_TM
    d = D_MODEL

    def row_spec(n):
        return pl.BlockSpec((tm, n), lambda i: (i, 0))

    return pl.pallas_call(
        functools.partial(_post_kernel, final=final),
        grid=(s // tm,),
        in_specs=[
            row_spec(d), row_spec(oa.shape[1]), row_spec(ob.shape[1]), row_spec(oc.shape[1]),
            row_spec(N_BRANCH * d),
            _const_spec(wa.shape), _const_spec(wb.shape), _const_spec(wc.shape), _const_spec((d, d)),
            _layer_spec((1, d), layer), _const_spec((d, D_FF)), _const_spec((D_FF, d)),
            _const_spec((1, d)),
        ],
        out_specs=row_spec(d),
        out_shape=jax.ShapeDtypeStruct((s, d), F32),
        compiler_params=pltpu.CompilerParams(
            dimension_semantics=("arbitrary",), vmem_limit_bytes=VMEM_LIMIT),
        name="post",
    )(x, oa, ob, oc, gates, wa, wb, wc, wout, gn, wup, wdn, fn)


def _rot_cols(w):
    half = MLA_ROPE // 2
    return jnp.concatenate([-w[..., half:], w[..., :half]], axis=-1)


def _pad_cols(w, left, total):
    return jnp.pad(w, [(0, 0)] * (w.ndim - 1) + [(left, total - left - w.shape[-1])])


def _stacked_weights(w_in, w_uq, w_ukv):
    depth = w_in.shape[0]
    n_kv = SWA_KV_HEADS * SWA_HD
    row_scale = jnp.where(jnp.arange(_W_END) >= _W_GATE, 0.5, 1.0).astype(F32)
    w_t = (jnp.swapaxes(w_in, 1, 2) * row_scale[None, :, None]).astype(BF16)
    kpe = w_in[:, :, _H_KPE:_W_QS]
    ks_swapped = jnp.concatenate([w_in[:, :, _W_KS + SWA_HD:_W_KS + n_kv],
                                  w_in[:, :, _W_KS:_W_KS + SWA_HD]], axis=-1)
    w_aux_t = jnp.swapaxes(
        jnp.concatenate([_pad_cols(_rot_cols(kpe), MLA_NOPE, HEAD_PAD), ks_swapped], axis=-1),
        1, 2).astype(BF16)

    uq = w_uq.reshape(depth, MLA_Q_LORA, MLA_HEADS, MLA_NOPE + MLA_ROPE)
    uq_plain = _pad_cols(uq, 0, HEAD_PAD)
    uq_rot = _pad_cols(_rot_cols(uq[..., MLA_NOPE:]), MLA_NOPE, HEAD_PAD)
    wuq = jnp.concatenate([uq_plain.reshape(depth, MLA_Q_LORA, -1), uq_rot.reshape(depth, MLA_Q_LORA, -1)],
                          axis=-1).astype(BF16)

    ukv = w_ukv.reshape(depth, MLA_KV_LORA, MLA_HEADS, MLA_NOPE + MLA_V)
    wk = _pad_cols(ukv[..., :MLA_NOPE], 0, HEAD_PAD).reshape(depth, MLA_KV_LORA, -1).astype(BF16)
    wv = ukv[..., MLA_NOPE:].reshape(depth, MLA_KV_LORA, -1).astype(BF16)
    return w_t, w_aux_t, wuq, wk, wv


def _rope_table(seq):
    pos = jnp.arange(seq, dtype=F32)
    inv = 1.0 / (ROPE_THETA ** (jnp.arange(0, MLA_ROPE, 2, dtype=F32) / MLA_ROPE))
    ang = inv[:, None] * pos[None, :]
    cos, sin = jnp.cos(ang), jnp.sin(ang)
    lead = jnp.zeros((MLA_NOPE, seq), F32)
    tail = jnp.zeros((HEAD_PAD - MLA_NOPE - MLA_ROPE, seq), F32)
    return jnp.concatenate([lead, cos, cos, tail, lead, sin, sin, tail], axis=0)


def _t5_bucket(dist):
    n = jnp.maximum(dist, 0)
    max_exact = REL_BUCKETS // 2
    nf = jnp.maximum(n, 1).astype(F32)
    large = max_exact + (jnp.log(nf / max_exact) / math.log(REL_MAX_DIST / max_exact)
                         * (REL_BUCKETS - max_exact)).astype(jnp.int32)
    large = jnp.minimum(large, REL_BUCKETS - 1)
    return jnp.where(n < max_exact, n, large)


def _swa_bucket_table():
    kj = jnp.arange(2 * BLOCK)[:, None]
    qi = jnp.arange(BLOCK)[None, :]
    dist = qi + BLOCK - kj
    band = (dist >= 0) & (dist < WINDOW)
    return jnp.where(band, _t5_bucket(dist), -1).astype(jnp.int32)


def kernel(x, mem, rel_bias, attn_norm, mem_norm, w_in, b_gate, mla_q_norm, w_uq, mla_kv_norm,
           w_ukv, attn_sinks, w_mem_kv, w_o_mla, w_o_swa, w_o_mem, w_out, mlp_norm, w_up, w_down,
           final_norm):
    batch, seq, d = x.shape
    assert batch == 1 and d == D_MODEL and mem.shape == (1, MEM_LEN, D_MODEL)
    depth = w_in.shape[0]
    xs = x[0]
    mem2 = mem[0]
    tab = _rope_table(seq)
    bkt = _swa_bucket_table()
    rel_bias = rel_bias.astype(F32)

    def rows(p):
        return p.reshape(depth, 1, -1)

    w_t, w_aux_t, wuq, wk, wv = _stacked_weights(w_in, w_uq, w_ukv)
    later_weights = (w_o_mla, w_o_swa, w_o_mem, w_out, w_up, w_down, w_mem_kv)
    attn_norm, mem_norm, mla_q_norm, mla_kv_norm, mlp_norm, b_gate = (
        rows(p) for p in (attn_norm, mem_norm, mla_q_norm, mla_kv_norm, mlp_norm, b_gate))
    half_b_gate = 0.5 * b_gate
    fn = final_norm.reshape(1, d)

    for l in range(depth):
        q, k, vt, qs, ks, ksw, vst, qm, gates, wa, wb, wc, wout, wup, wdn, wmem = _proj_call(
            l, xs, attn_norm, w_t, w_aux_t, mla_q_norm, wuq, mla_kv_norm, wk, wv, half_b_gate, tab,
            later_weights)
        o_mla = _mla_call(q, k, vt)
        o_swa, o_mem = _local_call(l, rel_bias, attn_sinks[l], bkt, qs, ks, ksw, vst, qm, mem2, mem_norm,
                                   wmem)
        xs = _post_call(l, xs, o_mla, o_swa, o_mem, gates, wa, wb, wc, wout, mlp_norm, wup, wdn, fn,
                        final=(l == depth - 1))
    return xs[None]
```

```python
import functools
import math

import jax
import jax.numpy as jnp
from jax import lax
from jax.experimental import pallas as pl
from jax.experimental.pallas import tpu as pltpu

F32 = jnp.float32
BF16 = jnp.bfloat16

D_MODEL = 1024
MLA_HEADS = 8
MLA_Q_LORA = 256
MLA_KV_LORA = 128
MLA_NOPE = 64
MLA_ROPE = 32
MLA_V = 64
ROPE_THETA = 10000.0
SWA_HEADS = 8
SWA_KV_HEADS = 2
SWA_HD = 64
WINDOW = 128
BLOCK = 128
REL_BUCKETS = 32
REL_MAX_DIST = 128
MEM_LEN = 256
MEM_HEADS = 4
MEM_HD = 128
N_BRANCH = 3
D_FF = 4 * D_MODEL
EPS = 1e-6

HEAD_PAD = 128
BF16_SUBLANES = 16
MLA_ACC_ROWS = MLA_V + BF16_SUBLANES
NEG_BIG = -1e30
LOG2E = math.log2(math.e)
LANES = 128

PROJ_TM = 512
MLA_TQ = 2048
MLA_TK = 512
MLA_TC = 256
MLA_HEADS_PER_STEP = 2
MLA_PAIRS_PER_TRIP = 2
MLA_LOOKAHEAD = 3
SWA_TQ = 1024
LOCAL_LOOKAHEAD = 5
MEM_TQ = 1024
POST_TM = 512
FF_CHUNK = 1024
VMEM_LIMIT = 56 * 1024 * 1024
N_LATER_WEIGHTS = 7

_H_CQ = 0
_H_CKV = _H_CQ + MLA_Q_LORA
_H_KPE = _H_CKV + MLA_KV_LORA
_W_CQ = _H_CQ
_W_QS = _H_KPE + MLA_ROPE
_W_KS = _W_QS + SWA_HEADS * SWA_HD
_W_GATE = _W_KS + 2 * SWA_KV_HEADS * SWA_HD + MEM_HEADS * MEM_HD
_W_END = _W_GATE + N_BRANCH * D_MODEL
_MLA_IN_COLS = 512
_R_QS = 0
_R_KS = _R_QS + SWA_HEADS * SWA_HD
_R_VS = _R_KS + SWA_KV_HEADS * SWA_HD
_R_QM = _R_VS + SWA_KV_HEADS * SWA_HD
_R_GATE = _R_QM + MEM_HEADS * MEM_HD
assert _W_QS % BF16_SUBLANES == 0 and _W_GATE == _W_QS + _R_GATE and _H_KPE + LANES <= _MLA_IN_COLS


def _rms(x, g):
    return x * lax.rsqrt(jnp.mean(x * x, axis=-1, keepdims=True) + EPS) * g


def _const_spec(shape):
    zeros = (0,) * len(shape)
    return pl.BlockSpec(shape, lambda *_: zeros, pipeline_mode=pl.Buffered(1))


def _layer_spec(shape, layer):
    index = (layer,) + (0,) * len(shape)
    return pl.BlockSpec((None,) + tuple(shape), lambda *_: index, pipeline_mode=pl.Buffered(1))


def _dot(a, b):
    return jnp.dot(a, b, preferred_element_type=F32)


def _dot_nt(a, b):
    return lax.dot_general(a, b, (((1,), (1,)), ((), ())), preferred_element_type=F32)


def _proj_kernel(x_ref, g_ref, wt_ref, wa_ref, qn_ref, wuq_ref, kvn_ref, wk_ref, wv_ref, bg_ref,
                 tab_ref, *refs):
    later_f32 = refs[:N_LATER_WEIGHTS]
    q_ref, k_ref, vt_ref, qs_ref, ks_ref, ksw_ref, vst_ref, qm_ref, gate_ref = refs[N_LATER_WEIGHTS:-N_LATER_WEIGHTS]
    later_bf16 = refs[-N_LATER_WEIGHTS:]
    for src, dst in zip(later_f32, later_bf16):
        dst[...] = src[...].astype(BF16)

    h = _rms(x_ref[...], g_ref[...]).astype(BF16)

    tab = tab_ref[...].T
    k_cos, k_sin = tab[:, 0:HEAD_PAD], tab[:, HEAD_PAD:2 * HEAD_PAD]
    q_scale = (MLA_NOPE + MLA_ROPE) ** -0.5 * LOG2E
    nope = (lax.broadcasted_iota(jnp.int32, k_cos.shape, 1) < MLA_NOPE).astype(F32)
    q_cos, q_sin = (k_cos + nope) * q_scale, k_sin * q_scale
    hw = MLA_HEADS * HEAD_PAD

    mla_in = _dot_nt(h, wt_ref[_W_CQ:_W_CQ + _MLA_IN_COLS, :])
    aux_in = _dot_nt(h, wa_ref[...])
    swa_in = _dot_nt(h, wt_ref[_W_QS:_W_GATE, :])

    cq = _rms(mla_in[:, _H_CQ:_H_CKV], qn_ref[...]).astype(BF16)
    qq = _dot(cq, wuq_ref[...])
    for hh in range(MLA_HEADS):
        lo = hh * HEAD_PAD
        q_ref[hh] = (qq[:, lo:lo + HEAD_PAD] * q_cos
                     + qq[:, hw + lo:hw + lo + HEAD_PAD] * q_sin).astype(BF16)

    ckv = _rms(mla_in[:, _H_CKV:_H_KPE], kvn_ref[...]).astype(BF16)
    kn = _dot(ckv, wk_ref[...])
    kpe = (pltpu.roll(mla_in[:, _H_KPE:_H_KPE + LANES], MLA_NOPE, axis=1) * k_cos
           + aux_in[:, 0:HEAD_PAD] * k_sin)
    for hh in range(MLA_HEADS):
        lo = hh * HEAD_PAD
        k_ref[hh] = (kn[:, lo:lo + HEAD_PAD] + kpe).astype(BF16)
    vt = _dot(ckv, wv_ref[...]).T
    for hh in range(MLA_HEADS):
        vt_ref[hh] = vt[hh * MLA_V:(hh + 1) * MLA_V, :].astype(BF16)

    qs_ref[...] = (swa_in[:, _R_QS:_R_KS] * (SWA_HD ** -0.5 * LOG2E)).astype(BF16)
    ks_ref[...] = swa_in[:, _R_KS:_R_VS].astype(BF16)
    ksw_ref[...] = aux_in[:, HEAD_PAD:].astype(BF16)
    vst_ref[...] = swa_in[:, _R_VS:_R_QM].T.astype(BF16)
    qm_ref[...] = (swa_in[:, _R_QM:_R_GATE] * (MEM_HD ** -0.5 * LOG2E)).astype(BF16)
    for c in range(N_BRANCH):
        lo = _W_GATE + c * D_MODEL
        half_z = _dot_nt(h, wt_ref[lo:lo + D_MODEL, :]) + bg_ref[:, c * D_MODEL:(c + 1) * D_MODEL]
        gate_ref[:, c * D_MODEL:(c + 1) * D_MODEL] = (0.5 * jnp.tanh(half_z) + 0.5).astype(BF16)


def _proj_call(layer, x, g, wt, wa, qn, wuq, kvn, wk, wv, bg, tab, later_weights):
    s = x.shape[0]
    tm = PROJ_TM
    steps = s // tm
    hw = MLA_HEADS * HEAD_PAD
    assert len(later_weights) == N_LATER_WEIGHTS
    assert all(w.shape[1] % (steps * BF16_SUBLANES) == 0 for w in later_weights)

    def slab_in(w):
        return pl.BlockSpec((None, w.shape[1] // steps, w.shape[2]), lambda i: (layer, i, 0))

    def slab_out(w):
        return pl.BlockSpec((w.shape[1] // steps, w.shape[2]), lambda i: (i, 0))

    head_spec = pl.BlockSpec((MLA_HEADS, tm, HEAD_PAD), lambda i: (0, i, 0))
    head_shape = jax.ShapeDtypeStruct((MLA_HEADS, s, HEAD_PAD), BF16)
    vt_spec = pl.BlockSpec((MLA_HEADS, MLA_V, tm), lambda i: (0, 0, i))
    vt_shape = jax.ShapeDtypeStruct((MLA_HEADS, MLA_V, s), BF16)

    def row_spec(n):
        return pl.BlockSpec((tm, n), lambda i: (i, 0))

    def row_shape(n):
        return jax.ShapeDtypeStruct((s, n), BF16)

    n_qs, n_kv, n_qm, n_g = SWA_HEADS * SWA_HD, SWA_KV_HEADS * SWA_HD, MEM_HEADS * MEM_HD, N_BRANCH * D_MODEL
    return pl.pallas_call(
        _proj_kernel,
        grid=(s // tm,),
        in_specs=[
            row_spec(D_MODEL),
            _layer_spec((1, D_MODEL), layer),
            _layer_spec((_W_END, D_MODEL), layer),
            _layer_spec((HEAD_PAD + n_kv, D_MODEL), layer),
            _layer_spec((1, MLA_Q_LORA), layer),
            _layer_spec((MLA_Q_LORA, 2 * hw), layer),
            _layer_spec((1, MLA_KV_LORA), layer),
            _layer_spec((MLA_KV_LORA, hw), layer),
            _layer_spec((MLA_KV_LORA, MLA_HEADS * MLA_V), layer),
            _layer_spec((1, n_g), layer),
            pl.BlockSpec((2 * HEAD_PAD, tm), lambda i: (0, i)),
        ] + [slab_in(w) for w in later_weights],
        out_specs=[head_spec, head_spec, vt_spec,
                   row_spec(n_qs), row_spec(n_kv), row_spec(n_kv),
                   pl.BlockSpec((n_kv, tm), lambda i: (0, i)), row_spec(n_qm), row_spec(n_g)]
        + [slab_out(w) for w in later_weights],
        out_shape=[head_shape, head_shape, vt_shape,
                   row_shape(n_qs), row_shape(n_kv), row_shape(n_kv),
                   jax.ShapeDtypeStruct((n_kv, s), BF16), row_shape(n_qm), row_shape(n_g)]
        + [jax.ShapeDtypeStruct(w.shape[1:], BF16) for w in later_weights],
        compiler_params=pltpu.CompilerParams(
            dimension_semantics=("arbitrary",), vmem_limit_bytes=VMEM_LIMIT,
            allow_input_fusion=[n == 2 for n in range(11 + N_LATER_WEIGHTS)]),
        name="proj",
    )(x, g, wt, wa, qn, wuq, kvn, wk, wv, bg, tab, *later_weights)


def _mla_kernel(q_ref, k_ref, vt_ref, o_ref, sa_scr, sb_scr, acc_scr):
    i = pl.program_id(1)
    tq, tk, tc, hp = MLA_TQ, MLA_TK, MLA_TC, MLA_HEADS_PER_STEP
    nc = tq // tc
    nd = tq // tk
    assert tq % tk == 0 and nd % 2 == 0 and tk % tc == 0
    assert hp * MLA_V == LANES
    key = lax.broadcasted_iota(jnp.int32, (tk, tc), 0)
    lane = lax.broadcasted_iota(jnp.int32, (tk, tc), 1)
    ones_rows = (lax.broadcasted_iota(jnp.int32, (MLA_ACC_ROWS - MLA_V, tk), 0) == 0).astype(BF16)

    units = [(h, c) for h in range(hp) for c in range(nc)]
    slot = {u: n for n, u in enumerate(units)}

    def scores(j, buf, u, key_off=None, or_valid=None):
        h, c = u
        start = pl.multiple_of(j * tk, tk)
        s = _dot_nt(k_ref[h, pl.ds(start, tk), :], q_ref[h, c * tc:(c + 1) * tc, :])
        if key_off is not None:
            valid = key + key_off <= lane + c * tc
            if or_valid is not None:
                valid = jnp.logical_or(valid, or_valid)
            s = jnp.where(valid, s, NEG_BIG)
        buf[slot[u]] = s
        return jnp.max(s, axis=0, keepdims=True)

    def accumulate(j, buf, u, m, cm):
        h, c = u
        start = pl.multiple_of(j * tk, tk)
        m_new = jnp.maximum(m, cm)
        alpha = jnp.exp2(m - m_new)
        p = jnp.exp2(buf[slot[u]] - m_new).astype(BF16)
        v1 = jnp.concatenate([vt_ref[h, :, pl.ds(start, tk)], ones_rows], axis=0)
        acc_scr[slot[u]] = alpha * acc_scr[slot[u]] + _dot(v1, p)
        return m_new

    def run(j0, carry, chunks, produce_next):
        m, cm_first = dict(zip(units, carry[0])), dict(zip(units, carry[1]))
        cm = {(0, u): cm_first[u] for u in units}
        todo = [(r, u) for r in range(1, len(chunks)) for u in chunks[r][0]]
        n_acc = len(chunks) - (1 if produce_next else 0)
        order = []

        def issue():
            r, u = todo.pop(0)
            d = chunks[r][1]
            masked = d is not None and (d + 1) * tk - 1 > u[1] * tc
            cm[(r, u)] = scores(j0 + r, sb_scr if r % 2 else sa_scr, u, key_off=d * tk if masked else None)
            order.append(("score", r, u))

        to_acc = [(r, u) for r in range(n_acc) for u in chunks[r][0]]

        def pump():
            while todo and sum(o[0] == "score" for o in order) - sum(o[0] == "acc" for o in order) < MLA_LOOKAHEAD:
                r, u = todo[0]
                if (r - 2, u) in to_acc and ("acc", r - 2, u) not in order:
                    break
                issue()

        pump()
        for r, u in to_acc:
            m[u] = accumulate(j0 + r, sb_scr if r % 2 else sa_scr, u, m[u], cm[(r, u)])
            order.append(("acc", r, u))
            pump()
        assert not todo
        for pos, (what, r, u) in enumerate(order):
            if what == "score" and ("acc", r - 2, u) in order:
                assert order.index(("acc", r - 2, u)) < pos
        next_cm = tuple(cm[(len(chunks) - 1, u)] for u in units) if produce_next else None
        return tuple(m[u] for u in units), next_cm

    plain = (units, None)
    diagonal = [([u for u in units if (u[1] + 1) * tc > d * tk], d) for d in range(nd)]

    def finish(carry, with_pair_before):
        chunks = ([plain, plain] if with_pair_before else []) + diagonal
        run(nd * i - (2 if with_pair_before else 0), carry, chunks, False)
        o_t = jnp.concatenate(
            [jnp.concatenate([acc_scr[slot[(h, c)], 0:MLA_V, :] / acc_scr[slot[(h, c)], MLA_V:MLA_V + 1, :]
                              for c in range(nc)], axis=1) for h in range(hp)], axis=0)
        for blk in range(tq // LANES):
            o_ref[blk * LANES:(blk + 1) * LANES, :] = o_t[:, blk * LANES:(blk + 1) * LANES].T.astype(BF16)

    acc_scr[...] = jnp.zeros_like(acc_scr)
    cm_a = tuple(scores(0, sa_scr, u, key_off=0, or_valid=i > 0) for u in units)
    carry = (tuple(jnp.full((1, tc), NEG_BIG, F32) for _ in units), cm_a)
    n_plain = jnp.maximum((nd // 2) * i - 1, 0)
    n_long = n_plain // MLA_PAIRS_PER_TRIP
    carry = lax.fori_loop(
        0, n_long,
        lambda t, c: run(2 * MLA_PAIRS_PER_TRIP * t, c, [plain] * (2 * MLA_PAIRS_PER_TRIP + 1), True), carry)
    carry = lax.fori_loop(
        n_long * MLA_PAIRS_PER_TRIP, n_plain, lambda t, c: run(2 * t, c, [plain] * 3, True), carry)

    @pl.when(i > 0)
    def _finish_after_pairs():
        finish(carry, True)

    @pl.when(i == 0)
    def _finish_first_tile():
        finish(carry, False)


def _mla_call(q, k, vt):
    s = q.shape[1]
    tq, hp = MLA_TQ, MLA_HEADS_PER_STEP
    n_units = hp * (tq // MLA_TC)
    return pl.pallas_call(
        _mla_kernel,
        grid=(MLA_HEADS // hp, s // tq),
        in_specs=[
            pl.BlockSpec((hp, tq, HEAD_PAD), lambda h, i: (h, i, 0)),
            pl.BlockSpec((hp, s, HEAD_PAD), lambda h, i: (h, 0, 0)),
            pl.BlockSpec((hp, MLA_V, s), lambda h, i: (h, 0, 0)),
        ],
        out_specs=pl.BlockSpec((tq, hp * MLA_V), lambda h, i: (i, h)),
        out_shape=jax.ShapeDtypeStruct((s, MLA_HEADS * MLA_V), BF16),
        scratch_shapes=[pltpu.VMEM((n_units, MLA_TK, MLA_TC), F32), pltpu.VMEM((n_units, MLA_TK, MLA_TC), F32),
                        pltpu.VMEM((n_units, MLA_ACC_ROWS, MLA_TC), F32)],
        compiler_params=pltpu.CompilerParams(
            dimension_semantics=("arbitrary", "arbitrary"), vmem_limit_bytes=VMEM_LIMIT),
        name="mla_attn",
    )(q, k, vt)


def _swa_jobs(relb_ref, sink_ref, bkt_ref, q_ref, k_ref, kp_ref, ksw_ref, kswp_ref, vt_ref, vtp_ref,
              o_ref, bias_scr):
    i = pl.program_id(0)
    tq, hd, rep = SWA_TQ, SWA_HD, SWA_HEADS // SWA_KV_HEADS
    win = 2 * BLOCK
    assert 2 * hd == LANES and rep == 4 and BLOCK == LANES

    @pl.when(i == 0)
    def _build_bias():
        bkt = bkt_ref[...]
        for hh in range(SWA_HEADS):
            t = jnp.full((win, BLOCK), NEG_BIG, F32)
            for b in range(REL_BUCKETS):
                t = jnp.where(bkt == b, relb_ref[b * SWA_HEADS + hh] * LOG2E, t)
            g, r = divmod(hh, rep)
            bias_scr[2 * g + r % 2, :, (r // 2) * BLOCK:(r // 2 + 1) * BLOCK] = t

    lo_half = lax.broadcasted_iota(jnp.int32, (BLOCK + tq, LANES), 1) < hd
    kfull = jnp.concatenate([kp_ref[...], k_ref[...]], axis=0)
    kswfull = jnp.concatenate([kswp_ref[...], ksw_ref[...]], axis=0)
    zero = jnp.zeros_like(kfull)
    k_lo = (jnp.where(lo_half, kfull, zero), jnp.where(lo_half, kswfull, zero))
    k_hi = (jnp.where(lo_half, zero, kswfull), jnp.where(lo_half, zero, kfull))
    vfull = jnp.concatenate([vtp_ref[...], vt_ref[...]], axis=1)
    ones_rows = (lax.broadcasted_iota(jnp.int32, (BF16_SUBLANES, win), 0) == 0).astype(BF16)
    key_row = lax.broadcasted_iota(jnp.int32, (win, 2 * BLOCK), 0)
    first_valid = jnp.logical_or(key_row >= BLOCK, i > 0)

    sinks = [jnp.concatenate(
        [jnp.full((1, BLOCK), sink_ref[g * rep + half] * LOG2E, F32),
         jnp.full((1, BLOCK), sink_ref[g * rep + 2 + half] * LOG2E, F32)], axis=1)
        for g in range(SWA_KV_HEADS) for half in range(2)]

    tiles = [(b, g, half) for b in range(tq // BLOCK) for g in range(SWA_KV_HEADS) for half in range(2)]

    def tile_scores(b, g, half):
        r0 = b * BLOCK
        qg = jnp.concatenate([q_ref[r0:r0 + BLOCK, (2 * g) * LANES:(2 * g + 1) * LANES],
                              q_ref[r0:r0 + BLOCK, (2 * g + 1) * LANES:(2 * g + 2) * LANES]],
                             axis=0)
        kmat = (k_lo, k_hi)[half][g]
        s = _dot_nt(kmat[r0:r0 + win], qg) + bias_scr[2 * g + half]
        return jnp.where(first_valid, s, NEG_BIG) if b == 0 else s

    def tile_output(b, g, half, s):
        r0 = b * BLOCK
        sink = sinks[2 * g + half]
        v1 = jnp.concatenate([vfull[g * hd:(g + 1) * hd, r0:r0 + win], ones_rows], axis=0)
        m = jnp.maximum(jnp.max(s, axis=0, keepdims=True), sink)
        p = jnp.exp2(s - m).astype(BF16)
        ot = _dot(v1, p)
        den = ot[hd:hd + 1] + jnp.exp2(sink - m)
        return ot[0:hd] / den

    outs = {}

    def tile_finish(b, g, half, s):
        outs[half] = tile_output(b, g, half, s)
        if half == 1:
            r0 = b * BLOCK
            for pr in range(2):
                x = jnp.concatenate([outs[0][:, pr * BLOCK:(pr + 1) * BLOCK],
                                     outs[1][:, pr * BLOCK:(pr + 1) * BLOCK]], axis=0)
                o_ref[r0:r0 + BLOCK, (2 * g + pr) * LANES:(2 * g + pr + 1) * LANES] = x.T.astype(BF16)

    return [(functools.partial(tile_scores, *tile), functools.partial(tile_finish, *tile)) for tile in tiles]


def _run_jobs(jobs, lookahead):
    pending = [score() for score, _ in jobs[:lookahead]]
    for t, (_, finish) in enumerate(jobs):
        s = pending.pop(0)
        if t + lookahead < len(jobs):
            pending.append(jobs[t + lookahead][0]())
        finish(s)


def _memkv_kernel(mem_ref, g_ref, w_ref, k_ref, vt_ref):
    n = MEM_HEADS * MEM_HD
    mn = _rms(mem_ref[...], g_ref[...]).astype(BF16)
    k_ref[...] = _dot(mn, w_ref[:, 0:n]).astype(BF16)
    vt_ref[...] = _dot(mn, w_ref[:, n:2 * n]).T.astype(BF16)


def _memkv_call(layer, mem, g, w):
    n = MEM_HEADS * MEM_HD
    return pl.pallas_call(
        _memkv_kernel,
        grid=(1,),
        in_specs=[_const_spec((MEM_LEN, D_MODEL)), _layer_spec((1, D_MODEL), layer),
                  _const_spec((D_MODEL, 2 * n))],
        out_specs=[_const_spec((MEM_LEN, n)), _const_spec((n, MEM_LEN))],
        out_shape=[jax.ShapeDtypeStruct((MEM_LEN, n), BF16), jax.ShapeDtypeStruct((n, MEM_LEN), BF16)],
        compiler_params=pltpu.CompilerParams(
            dimension_semantics=("arbitrary",), vmem_limit_bytes=VMEM_LIMIT),
        name="mem_kv",
    )(mem, g, w)


def _mem_jobs(q_ref, k_ref, vt_ref, o_ref):
    tq, dm = MEM_TQ, MEM_HD
    ones_rows = (lax.broadcasted_iota(jnp.int32, (BF16_SUBLANES, MEM_LEN), 0) == 0).astype(BF16)

    def head_scores(hh):
        return _dot_nt(k_ref[:, hh * dm:(hh + 1) * dm], q_ref[:, hh * dm:(hh + 1) * dm])

    def head_finish(hh, s):
        p = jnp.exp2(s - jnp.max(s, axis=0, keepdims=True)).astype(BF16)
        v1 = jnp.concatenate([vt_ref[hh * dm:(hh + 1) * dm, :], ones_rows], axis=0)
        ot = _dot(v1, p)
        o = ot[0:dm] / ot[dm:dm + 1]
        for blk in range(tq // LANES):
            o_ref[blk * LANES:(blk + 1) * LANES, hh * dm:(hh + 1) * dm] = (
                o[:, blk * LANES:(blk + 1) * LANES].T.astype(BF16))

    return [(functools.partial(head_scores, hh), functools.partial(head_finish, hh))
            for hh in range(MEM_HEADS)]


def _local_kernel(relb_ref, sink_ref, bkt_ref, q_ref, k_ref, kp_ref, ksw_ref, kswp_ref, vt_ref, vtp_ref,
                  qm_ref, km_ref, vmt_ref, o_ref, om_ref, bias_scr):
    swa = _swa_jobs(relb_ref, sink_ref, bkt_ref, q_ref, k_ref, kp_ref, ksw_ref, kswp_ref, vt_ref, vtp_ref,
                    o_ref, bias_scr)
    mem = _mem_jobs(qm_ref, km_ref, vmt_ref, om_ref)
    every = len(swa) // len(mem)
    jobs = []
    for n, job in enumerate(swa):
        jobs.append(job)
        if n % every == every - 1:
            jobs.append(mem[n // every])
    _run_jobs(jobs, LOCAL_LOOKAHEAD)


def _local_call(rel_bias, sinks, bkt, qs, ks, ksw, vst, qm, km, vmt):
    s = qs.shape[0]
    tq = SWA_TQ
    assert MEM_TQ == tq and MEM_HD == LANES
    nblk = tq // BLOCK
    n_kv = SWA_KV_HEADS * SWA_HD
    n_qs = SWA_HEADS * SWA_HD
    n_qm = MEM_HEADS * MEM_HD
    own = pl.BlockSpec((tq, n_kv), lambda i: (i, 0))
    prev = pl.BlockSpec((BLOCK, n_kv), lambda i: (jnp.maximum(i * nblk - 1, 0), 0))
    own_t = pl.BlockSpec((n_kv, tq), lambda i: (0, i))
    prev_t = pl.BlockSpec((n_kv, BLOCK), lambda i: (0, jnp.maximum(i * nblk - 1, 0)))
    smem = pl.BlockSpec(memory_space=pltpu.SMEM)
    return pl.pallas_call(
        _local_kernel,
        grid=(s // tq,),
        in_specs=[
            smem, smem, _const_spec((2 * BLOCK, BLOCK)),
            pl.BlockSpec((tq, n_qs), lambda i: (i, 0)),
            own, prev, own, prev, own_t, prev_t,
            pl.BlockSpec((tq, n_qm), lambda i: (i, 0)), _const_spec((MEM_LEN, n_qm)),
            _const_spec((n_qm, MEM_LEN)),
        ],
        out_specs=[pl.BlockSpec((tq, n_qs), lambda i: (i, 0)), pl.BlockSpec((tq, n_qm), lambda i: (i, 0))],
        out_shape=[jax.ShapeDtypeStruct((s, n_qs), BF16), jax.ShapeDtypeStruct((s, n_qm), BF16)],
        scratch_shapes=[pltpu.VMEM((2 * SWA_KV_HEADS, 2 * BLOCK, 2 * BLOCK), F32)],
        compiler_params=pltpu.CompilerParams(
            dimension_semantics=("arbitrary",), vmem_limit_bytes=VMEM_LIMIT),
        name="local_attn",
    )(rel_bias.reshape(-1), sinks, bkt, qs, ks, ks, ksw, ksw, vst, vst, qm, km, vmt)


def _post_kernel(x_ref, oa_ref, ob_ref, oc_ref, gate_ref, wa_ref, wb_ref, wc_ref, wout_ref,
                 gn_ref, wup_ref, wdn_ref, fn_ref, o_ref, *, final):
    d = D_MODEL
    y = (gate_ref[:, 0:d].astype(F32) * _dot(oa_ref[...], wa_ref[...])
         + gate_ref[:, d:2 * d].astype(F32) * _dot(ob_ref[...], wb_ref[...])
         + gate_ref[:, 2 * d:3 * d].astype(F32) * _dot(oc_ref[...], wc_ref[...]))
    x1 = x_ref[...] + _dot(y.astype(BF16), wout_ref[...])
    h = _rms(x1, gn_ref[...]).astype(BF16)
    acc = x1
    for c in range(D_FF // FF_CHUNK):
        lo = c * FF_CHUNK
        u = jnp.maximum(_dot(h, wup_ref[:, lo:lo + FF_CHUNK]), 0.0)
        acc = acc + _dot((u * u).astype(BF16), wdn_ref[lo:lo + FF_CHUNK, :])
    if final:
        acc = _rms(acc, fn_ref[...])
    o_ref[...] = acc


def _post_call(layer, x, oa, ob, oc, gates, wa, wb, wc, wout, gn, wup, wdn, fn, final):
    s = x.shape[0]
    tm = POST_TM
    d = D_MODEL

    def row_spec(n):
        return pl.BlockSpec((tm, n), lambda i: (i, 0))

    return pl.pallas_call(
        functools.partial(_post_kernel, final=final),
        grid=(s // tm,),
        in_specs=[
            row_spec(d), row_spec(oa.shape[1]), row_spec(ob.shape[1]), row_spec(oc.shape[1]),
            row_spec(N_BRANCH * d),
            _const_spec(wa.shape), _const_spec(wb.shape), _const_spec(wc.shape), _const_spec((d, d)),
            _layer_spec((1, d), layer), _const_spec((d, D_FF)), _const_spec((D_FF, d)),
            _const_spec((1, d)),
        ],
        out_specs=row_spec(d),
        out_shape=jax.ShapeDtypeStruct((s, d), F32),
        compiler_params=pltpu.CompilerParams(
            dimension_semantics=("arbitrary",), vmem_limit_bytes=VMEM_LIMIT),
        name="post",
    )(x, oa, ob, oc, gates, wa, wb, wc, wout, gn, wup, wdn, fn)


def _rot_cols(w):
    half = MLA_ROPE // 2
    return jnp.concatenate([-w[..., half:], w[..., :half]], axis=-1)


def _pad_cols(w, left, total):
    return jnp.pad(w, [(0, 0)] * (w.ndim - 1) + [(left, total - left - w.shape[-1])])


def _stacked_weights(w_in, w_uq, w_ukv):
    depth = w_in.shape[0]
    n_kv = SWA_KV_HEADS * SWA_HD
    row_scale = jnp.where(jnp.arange(_W_END) >= _W_GATE, 0.5, 1.0).astype(F32)
    w_t = (jnp.swapaxes(w_in, 1, 2) * row_scale[None, :, None]).astype(BF16)
    kpe = w_in[:, :, _H_KPE:_W_QS]
    ks_swapped = jnp.concatenate([w_in[:, :, _W_KS + SWA_HD:_W_KS + n_kv],
                                  w_in[:, :, _W_KS:_W_KS + SWA_HD]], axis=-1)
    w_aux_t = jnp.swapaxes(
        jnp.concatenate([_pad_cols(_rot_cols(kpe), MLA_NOPE, HEAD_PAD), ks_swapped], axis=-1),
        1, 2).astype(BF16)

    uq = w_uq.reshape(depth, MLA_Q_LORA, MLA_HEADS, MLA_NOPE + MLA_ROPE)
    uq_plain = _pad_cols(uq, 0, HEAD_PAD)
    uq_rot = _pad_cols(_rot_cols(uq[..., MLA_NOPE:]), MLA_NOPE, HEAD_PAD)
    wuq = jnp.concatenate([uq_plain.reshape(depth, MLA_Q_LORA, -1), uq_rot.reshape(depth, MLA_Q_LORA, -1)],
                          axis=-1).astype(BF16)

    ukv = w_ukv.reshape(depth, MLA_KV_LORA, MLA_HEADS, MLA_NOPE + MLA_V)
    wk = _pad_cols(ukv[..., :MLA_NOPE], 0, HEAD_PAD).reshape(depth, MLA_KV_LORA, -1).astype(BF16)
    wv = ukv[..., MLA_NOPE:].reshape(depth, MLA_KV_LORA, -1).astype(BF16)
    return w_t, w_aux_t, wuq, wk, wv


def _rope_table(seq):
    pos = jnp.arange(seq, dtype=F32)
    inv = 1.0 / (ROPE_THETA ** (jnp.arange(0, MLA_ROPE, 2, dtype=F32) / MLA_ROPE))
    ang = inv[:, None] * pos[None, :]
    cos, sin = jnp.cos(ang), jnp.sin(ang)
    lead = jnp.zeros((MLA_NOPE, seq), F32)
    tail = jnp.zeros((HEAD_PAD - MLA_NOPE - MLA_ROPE, seq), F32)
    return jnp.concatenate([lead, cos, cos, tail, lead, sin, sin, tail], axis=0)


def _t5_bucket(dist):
    n = jnp.maximum(dist, 0)
    max_exact = REL_BUCKETS // 2
    nf = jnp.maximum(n, 1).astype(F32)
    large = max_exact + (jnp.log(nf / max_exact) / math.log(REL_MAX_DIST / max_exact)
                         * (REL_BUCKETS - max_exact)).astype(jnp.int32)
    large = jnp.minimum(large, REL_BUCKETS - 1)
    return jnp.where(n < max_exact, n, large)


def _swa_bucket_table():
    kj = jnp.arange(2 * BLOCK)[:, None]
    qi = jnp.arange(BLOCK)[None, :]
    dist = qi + BLOCK - kj
    band = (dist >= 0) & (dist < WINDOW)
    return jnp.where(band, _t5_bucket(dist), -1).astype(jnp.int32)


def kernel(x, mem, rel_bias, attn_norm, mem_norm, w_in, b_gate, mla_q_norm, w_uq, mla_kv_norm,
           w_ukv, attn_sinks, w_mem_kv, w_o_mla, w_o_swa, w_o_mem, w_out, mlp_norm, w_up, w_down,
           final_norm):
    batch, seq, d = x.shape
    assert batch == 1 and d == D_MODEL and mem.shape == (1, MEM_LEN, D_MODEL)
    depth = w_in.shape[0]
    xs = x[0]
    mem2 = mem[0]
    tab = _rope_table(seq)
    bkt = _swa_bucket_table()
    rel_bias = rel_bias.astype(F32)

    def rows(p):
        return p.reshape(depth, 1, -1)

    w_t, w_aux_t, wuq, wk, wv = _stacked_weights(w_in, w_uq, w_ukv)
    later_weights = (w_o_mla, w_o_swa, w_o_mem, w_out, w_up, w_down, w_mem_kv)
    attn_norm, mem_norm, mla_q_norm, mla_kv_norm, mlp_norm, b_gate = (
        rows(p) for p in (attn_norm, mem_norm, mla_q_norm, mla_kv_norm, mlp_norm, b_gate))
    half_b_gate = 0.5 * b_gate
    fn = final_norm.reshape(1, d)

    for l in range(depth):
        q, k, vt, qs, ks, ksw, vst, qm, gates, wa, wb, wc, wout, wup, wdn, wmem = _proj_call(
            l, xs, attn_norm, w_t, w_aux_t, mla_q_norm, wuq, mla_kv_norm, wk, wv, half_b_gate, tab,
            later_weights)
        o_mla = _mla_call(q, k, vt)
        km, vmt = _memkv_call(l, mem2, mem_norm, wmem)
        o_swa, o_mem = _local_call(rel_bias, attn_sinks[l], bkt, qs, ks, ksw, vst, qm, km, vmt)
        xs = _post_call(l, xs, o_mla, o_swa, o_mem, gates, wa, wb, wc, wout, mlp_norm, wup, wdn, fn,
                        final=(l == depth - 1))
    return xs[None]
```
